```python
import math
import jax
import jax.numpy as jnp
from jax import lax
import numpy as np

D_MODEL = 1024
BATCH = 2
SEQ = 8192
DEPTH = 2

GRID_W = 64
CTX_LEN = 256
EPS = 1e-6
ROPE_BASE = 10000.0

SSM_HEADS = 16
SSM_HEAD_DIM = 64
SSM_INNER = SSM_HEADS * SSM_HEAD_DIM
SSM_GROUPS = 2
SSM_STATE = 128
SSM_CONV = 5
SSM_CHUNK = 128
SSM_CONV_DIM = SSM_INNER + 2 * SSM_GROUPS * SSM_STATE

SWA_Q_HEADS = 8
SWA_KV_HEADS = 2
SWA_HEAD_DIM = 128
SWA_WINDOW = 128
SWA_BLOCK = 128

MLA_HEADS = 8
MLA_Q_RANK = 384
MLA_KV_RANK = 256
MLA_NOPE = 128
MLA_ROPE = 64
MLA_QK = MLA_NOPE + MLA_ROPE
MLA_V = 128
MLA_Q_BLOCK = 128

N_BRANCH = 3
FFN_HIDDEN = -(-8 * D_MODEL // (3 * 256)) * 256

IN_WIDTHS = (
    SSM_CONV_DIM,
    2 * SSM_HEADS,
    SWA_KV_HEADS * SWA_HEAD_DIM,
    SWA_KV_HEADS * SWA_HEAD_DIM,
    MLA_KV_RANK,
    MLA_ROPE,
    SSM_INNER,
    SWA_Q_HEADS * SWA_HEAD_DIM,
    MLA_Q_RANK,
    N_BRANCH * D_MODEL,
)
N_KV_SPLITS = 6
KV_COLS = sum(IN_WIDTHS[:N_KV_SPLITS])
IN_COLS = sum(IN_WIDTHS)

kernel_name = 'hybrid_ssd_swa_mla_dit_block'

F32 = jnp.float32


def rms_norm(x, g):
    xf = x.astype(F32)
    y = xf * lax.rsqrt(jnp.mean(xf * xf, axis=-1, keepdims=True) + EPS)
    return (y * g.astype(F32)).astype(x.dtype)


def modulate(x, g, shift, scale):
    return rms_norm(x, g) * (1.0 + scale[:, None, :]) + shift[:, None, :]


def split_cols(u, widths):
    offs = np.cumsum(widths)[:-1].tolist()
    return jnp.split(u, offs, axis=-1)


def axial_rope_tables(rows, rot_dim):
    n_freq = rot_dim // 4
    inv = jnp.power(ROPE_BASE, -jnp.arange(n_freq, dtype=F32) / n_freq)
    r, col = jnp.meshgrid(jnp.arange(rows, dtype=F32), jnp.arange(GRID_W, dtype=F32), indexing='ij')
    ang = jnp.stack([r.reshape(-1)[:, None] * inv, col.reshape(-1)[:, None] * inv], axis=1)
    return jnp.cos(ang), jnp.sin(ang)


def apply_axial_rope(x, cos, sin):
    shp = x.shape
    xr = x.astype(F32).reshape(shp[:-1] + (2, 2, shp[-1] // 4))
    x1, x2 = xr[..., 0, :], xr[..., 1, :]
    c = cos[None, :, None]
    s = sin[None, :, None]
    out = jnp.stack([x1 * c - x2 * s, x2 * c + x1 * s], axis=-2)
    return out.reshape(shp).astype(x.dtype)


def centred_dwconv(u, w, b):
    k, ch = w.shape
    out = lax.conv_general_dilated(u, w[:, None, :].astype(u.dtype), window_strides=(1,),
                                   padding=[(k // 2, k // 2)], dimension_numbers=('NWC', 'WIO', 'NWC'),
                                   feature_group_count=ch)
    return out + b


def ssd_scan(xh, dt, a, bm, cm, h0, with_y):
    bsz, L, H, P = xh.shape
    G, N = bm.shape[-2:]
    hpg = H // G
    Q = SSM_CHUNK
    nc = L // Q
    x = xh.astype(F32).reshape(bsz, nc, Q, G, hpg, P)
    dtc = dt.astype(F32).reshape(bsz, nc, Q, G, hpg)
    bc = bm.astype(F32).reshape(bsz, nc, Q, G, N)
    cc = cm.astype(F32).reshape(bsz, nc, Q, G, N)
    acs = jnp.cumsum(dtc * a.astype(F32).reshape(G, hpg), axis=2)
    xdt = x * dtc[..., None]
    decay_end = jnp.exp(acs[:, :, -1:] - acs)
    states = jnp.einsum('bcjgn,bcjghp->bcghpn', bc, xdt * decay_end[..., None])
    chunk_decay = jnp.exp(acs[:, :, -1])

    def step(h, inp):
        s, d = inp
        return h * d[..., None, None] + s, h

    h_t, h_in = lax.scan(step, h0.astype(F32).reshape(bsz, G, hpg, P, N),
                         (jnp.moveaxis(states, 1, 0), jnp.moveaxis(chunk_decay, 1, 0)))
    h_t = h_t.reshape(bsz, H, P, N)
    if not with_y:
        return None, h_t
    h_in = jnp.moveaxis(h_in, 0, 1)
    acs_t = jnp.moveaxis(acs, 2, -1)
    lower = jnp.tril(jnp.ones((Q, Q), dtype=bool))
    seg = jnp.exp(jnp.where(lower, acs_t[..., :, None] - acs_t[..., None, :], -jnp.inf))
    cb = jnp.einsum('bcign,bcjgn->bcgij', cc, bc)
    y_diag = jnp.einsum('bcghij,bcjghp->bcighp', cb[:, :, :, None] * seg, xdt)
    y_off = jnp.einsum('bcign,bcghpn->bcighp', cc, h_in) * jnp.exp(acs)[..., None]
    y = (y_diag + y_off).reshape(bsz, L, H, P)
    return y.astype(xh.dtype), h_t


def ssd_branch(xbc_l, dt_l, z_l, xbc_c, dt_c, z_c, conv_w, conv_b, dt_bias, a_log, d_skip, norm_g):
    a = -jnp.exp(a_log.astype(F32))

    def prep(xbc, dt):
        bsz, n = xbc.shape[:2]
        u = jax.nn.silu(centred_dwconv(xbc, conv_w, conv_b))
        xs, bm, cm = jnp.split(u, [SSM_INNER, SSM_INNER + SSM_GROUPS * SSM_STATE], axis=-1)
        dts = jax.nn.softplus(dt.reshape(bsz, n, 2, SSM_HEADS).astype(F32) + dt_bias.astype(F32))
        return (xs.reshape(bsz, n, SSM_HEADS, SSM_HEAD_DIM), bm.reshape(bsz, n, SSM_GROUPS, SSM_STATE),
                cm.reshape(bsz, n, SSM_GROUPS, SSM_STATE), dts)

    def rev(t):
        return jnp.flip(t, axis=1)

    def bidir(xs, bm, cm, dts, h_f, h_b, with_y):
        y_f, hf = ssd_scan(xs, dts[:, :, 0], a[0], bm, cm, h_f, with_y)
        y_b, hb = ssd_scan(rev(xs), rev(dts[:, :, 1]), a[1], rev(bm), rev(cm), h_b, with_y)
        y = y_f + rev(y_b) + d_skip[:, None] * xs if with_y else None
        return y, hf, hb

    def gated_out(y, z):
        return rms_norm(y.reshape(z.shape) * jax.nn.silu(z), norm_g)

    with_ctx = z_c is not None
    xs_c, b_c, c_c, dts_c = prep(xbc_c, dt_c)
    h0 = jnp.zeros((xs_c.shape[0], SSM_HEADS, SSM_HEAD_DIM, SSM_STATE), F32)
    y_c, hc_f, hc_b = bidir(xs_c, b_c, c_c, dts_c, h0, h0, with_ctx)
    xs_l, b_l, c_l, dts_l = prep(xbc_l, dt_l)
    y_l, _, _ = bidir(xs_l, b_l, c_l, dts_l, hc_f, hc_b, True)
    out_l = gated_out(y_l, z_l)
    out_c = gated_out(y_c, z_c) if with_ctx else None
    return out_l, out_c


def swa_branch(q_l, k_l, v_l, q_c, k_c, v_c, q_g, k_g, sink, rope):
    bsz, L = k_l.shape[:2]
    n_ctx = k_c.shape[1]
    grp = SWA_Q_HEADS // SWA_KV_HEADS
    blk = SWA_BLOCK
    nb = L // blk
    scale = SWA_HEAD_DIM ** -0.5

    def q_heads(q):
        return rms_norm(q.reshape(q.shape[0], q.shape[1], SWA_Q_HEADS, SWA_HEAD_DIM), q_g)

    def kv_heads(k, v):
        shp = (k.shape[0], k.shape[1], SWA_KV_HEADS, SWA_HEAD_DIM)
        return rms_norm(k.reshape(shp), k_g), v.reshape(shp)

    kc, vc = kv_heads(k_c, v_c)
    kl, vl = kv_heads(k_l, v_l)
    kl = apply_axial_rope(kl, *rope)
    ql = apply_axial_rope(q_heads(q_l), *rope)
    sink_f = sink.astype(F32).reshape(SWA_KV_HEADS, grp)

    qb = ql.reshape(bsz, nb, blk, SWA_KV_HEADS, grp, SWA_HEAD_DIM)

    def band(t):
        tp = jnp.pad(t, ((0, 0), (blk, blk), (0, 0), (0, 0))).reshape(bsz, nb + 2, blk, SWA_KV_HEADS, SWA_HEAD_DIM)
        return jnp.concatenate([tp[:, :-2], tp[:, 1:-1], tp[:, 2:]], axis=2)

    kb, vb = band(kl), band(vl)
    s_win = jnp.einsum('bnqhgd,bnkhd->bnhgqk', qb, kb).astype(F32) * scale
    q_idx = jnp.arange(blk)[:, None] + blk
    k_idx = jnp.arange(3 * blk)[None, :]
    k_pos = (jnp.arange(nb)[:, None, None] - 1) * blk + k_idx
    mask = (jnp.abs(k_idx - q_idx) <= SWA_WINDOW)[None] & (k_pos >= 0) & (k_pos < L)
    s_win = jnp.where(mask[None, :, None, None], s_win, -jnp.inf)
    s_ctx = jnp.einsum('bnqhgd,bkhd->bnhgqk', qb, kc).astype(F32) * scale
    s_sink = jnp.broadcast_to(sink_f[None, None, :, :, None, None], s_win.shape[:-1] + (1,))
    p = jax.nn.softmax(jnp.concatenate([s_win, s_ctx, s_sink], axis=-1), axis=-1).astype(vb.dtype)
    o = (jnp.einsum('bnhgqk,bnkhd->bnqhgd', p[..., :3 * blk], vb)
         + jnp.einsum('bnhgqk,bkhd->bnqhgd', p[..., 3 * blk:3 * blk + n_ctx], vc))
    out_l = o.reshape(bsz, L, SWA_Q_HEADS * SWA_HEAD_DIM)
    if q_c is None:
        return out_l, None
    qc = q_heads(q_c).reshape(bsz, n_ctx, SWA_KV_HEADS, grp, SWA_HEAD_DIM)
    s_c = jnp.einsum('bqhgd,bkhd->bhgqk', qc, kc).astype(F32) * scale
    s_cs = jnp.broadcast_to(sink_f[None, :, :, None, None], s_c.shape[:-1] + (1,))
    pc = jax.nn.softmax(jnp.concatenate([s_c, s_cs], axis=-1), axis=-1).astype(vc.dtype)
    out_c = jnp.einsum('bhgqk,bkhd->bqhgd', pc[..., :n_ctx], vc).reshape(bsz, n_ctx, SWA_Q_HEADS * SWA_HEAD_DIM)
    return out_l, out_c


def mla_keys(ckv, kr, kv_lat_g, w_ukv, k_g, rope):
    bsz, n = ckv.shape[:2]
    kv = (rms_norm(ckv, kv_lat_g) @ w_ukv).reshape(bsz, n, MLA_HEADS, MLA_NOPE + MLA_V)
    k_nope = rms_norm(kv[..., :MLA_NOPE], k_g[:MLA_NOPE])
    k_rope = rms_norm(kr[:, :, None, :], k_g[MLA_NOPE:])
    if rope is not None:
        k_rope = apply_axial_rope(k_rope, *rope)
    k = jnp.concatenate([k_nope, jnp.broadcast_to(k_rope, (bsz, n, MLA_HEADS, MLA_ROPE))], axis=-1)
    return k, kv[..., MLA_NOPE:]


def mla_queries(cq, q_lat_g, w_uq, q_g, rope):
    bsz, n = cq.shape[:2]
    q = (rms_norm(cq, q_lat_g) @ w_uq).reshape(bsz, n, MLA_HEADS, MLA_QK)
    q_nope = rms_norm(q[..., :MLA_NOPE], q_g[:MLA_NOPE])
    q_rope = rms_norm(q[..., MLA_NOPE:], q_g[MLA_NOPE:])
    if rope is not None:
        q_rope = apply_axial_rope(q_rope, *rope)
    return jnp.concatenate([q_nope, q_rope], axis=-1)


def mla_branch(cq_l, ckv_l, kr_l, cq_c, ckv_c, kr_c, q_lat_g, kv_lat_g, w_uq, w_ukv, q_g, k_g, rope):
    bsz, L = cq_l.shape[:2]
    n_ctx = ckv_c.shape[1]
    scale = MLA_QK ** -0.5
    k_c, v_c = mla_keys(ckv_c, kr_c, kv_lat_g, w_ukv, k_g, None)
    k_l, v_l = mla_keys(ckv_l, kr_l, kv_lat_g, w_ukv, k_g, rope)
    q_l = mla_queries(cq_l, q_lat_g, w_uq, q_g, rope)
    k_all = jnp.concatenate([k_l, k_c], axis=1)
    v_all = jnp.concatenate([v_l, v_c], axis=1)
    nb = L // MLA_Q_BLOCK
    qb = jnp.moveaxis(q_l.reshape(bsz, nb, MLA_Q_BLOCK, MLA_HEADS, MLA_QK), 1, 0)

    def attend(qblk):
        s = jnp.einsum('bqhd,bkhd->bhqk', qblk, k_all).astype(F32) * scale
        p = jax.nn.softmax(s, axis=-1).astype(v_all.dtype)
        return jnp.einsum('bhqk,bkhd->bqhd', p, v_all)

    out_l = jnp.moveaxis(lax.map(attend, qb), 0, 1).reshape(bsz, L, MLA_HEADS * MLA_V)
    if cq_c is None:
        return out_l, None
    q_c = mla_queries(cq_c, q_lat_g, w_uq, q_g, None)
    s = jnp.einsum('bqhd,bkhd->bhqk', q_c, k_c).astype(F32) * scale
    p = jax.nn.softmax(s, axis=-1).astype(v_c.dtype)
    out_c = jnp.einsum('bhqk,bkhd->bqhd', p, v_c).reshape(bsz, n_ctx, MLA_HEADS * MLA_V)
    return out_l, out_c


def merge_branches(gates, y_ssm, y_swa, y_mla, w_p_ssm, w_p_swa, w_p_mla, w_o):
    g = jax.nn.sigmoid(gates.astype(F32)).astype(gates.dtype)
    g_ssm, g_swa, g_mla = jnp.split(g, N_BRANCH, axis=-1)
    merged = g_ssm * (y_ssm @ w_p_ssm) + g_swa * (y_swa @ w_p_swa) + g_mla * (y_mla @ w_p_mla)
    return merged @ w_o


def swiglu(h, w_in, w_out):
    g, u = jnp.split(h @ w_in, 2, axis=-1)
    return (jax.nn.silu(g) * u) @ w_out


def setup_inputs(seed: int = 0) -> dict:
    key = jax.random.key(seed)
    ks = iter(jax.random.split(key, 40))

    def nrm(shape, s):
        return jax.random.normal(next(ks), shape, F32) * s

    def gain(shape):
        return 1.0 + nrm(shape, 0.02)

    x = nrm((BATCH, SEQ, D_MODEL), 1.0)
    c = nrm((BATCH, D_MODEL), 1.0)
    ctx = nrm((BATCH, CTX_LEN, D_MODEL), 1.0)
    c_ctx = nrm((D_MODEL,), 1.0)
    w_mod = nrm((DEPTH, D_MODEL, 6 * D_MODEL), 0.5 * D_MODEL ** -0.5)
    b_mod = nrm((DEPTH, 6 * D_MODEL), 0.01)
    norm1_g = gain((DEPTH, D_MODEL))
    norm2_g = gain((DEPTH, D_MODEL))
    w_in = nrm((DEPTH, D_MODEL, IN_COLS), D_MODEL ** -0.5)
    ssm_conv_w = nrm((DEPTH, SSM_CONV, SSM_CONV_DIM), SSM_CONV ** -0.5)
    ssm_conv_b = nrm((DEPTH, SSM_CONV_DIM), 0.01)
    dt0 = jnp.exp(jax.random.uniform(next(ks), (DEPTH, 2, SSM_HEADS), F32, math.log(1e-3), math.log(1e-1)))
    ssm_dt_bias = dt0 + jnp.log(-jnp.expm1(-dt0))
    ssm_a_log = jnp.log(jax.random.uniform(next(ks), (DEPTH, 2, SSM_HEADS), F32, 1.0, 16.0))
    ssm_d = gain((DEPTH, SSM_HEADS))
    ssm_norm_g = gain((DEPTH, SSM_INNER))
    swa_q_norm_g = gain((DEPTH, SWA_HEAD_DIM))
    swa_k_norm_g = gain((DEPTH, SWA_HEAD_DIM))
    swa_sink = nrm((DEPTH, SWA_Q_HEADS), 0.5)
    mla_q_lat_g = gain((DEPTH, MLA_Q_RANK))
    mla_kv_lat_g = gain((DEPTH, MLA_KV_RANK))
    w_mla_uq = nrm((DEPTH, MLA_Q_RANK, MLA_HEADS * MLA_QK), MLA_Q_RANK ** -0.5)
    w_mla_ukv = nrm((DEPTH, MLA_KV_RANK, MLA_HEADS * (MLA_NOPE + MLA_V)), MLA_KV_RANK ** -0.5)
    mla_q_norm_g = gain((DEPTH, MLA_QK))
    mla_k_norm_g = gain((DEPTH, MLA_QK))
    w_p_ssm = nrm((DEPTH, SSM_INNER, D_MODEL), SSM_INNER ** -0.5)
    w_p_swa = nrm((DEPTH, SWA_Q_HEADS * SWA_HEAD_DIM, D_MODEL), (SWA_Q_HEADS * SWA_HEAD_DIM) ** -0.5)
    w_p_mla = nrm((DEPTH, MLA_HEADS * MLA_V, D_MODEL), (MLA_HEADS * MLA_V) ** -0.5)
    w_out = nrm((DEPTH, D_MODEL, D_MODEL), D_MODEL ** -0.5)
    w_ffn_in = nrm((DEPTH, D_MODEL, 2 * FFN_HIDDEN), D_MODEL ** -0.5)
    w_ffn_out = nrm((DEPTH, FFN_HIDDEN, D_MODEL), FFN_HIDDEN ** -0.5)
    return {'x': x, 'c': c, 'ctx': ctx, 'c_ctx': c_ctx, 'w_mod': w_mod, 'b_mod': b_mod,
            'norm1_g': norm1_g, 'norm2_g': norm2_g, 'w_in': w_in,
            'ssm_conv_w': ssm_conv_w, 'ssm_conv_b': ssm_conv_b, 'ssm_dt_bias': ssm_dt_bias,
            'ssm_a_log': ssm_a_log, 'ssm_d': ssm_d, 'ssm_norm_g': ssm_norm_g,
            'swa_q_norm_g': swa_q_norm_g, 'swa_k_norm_g': swa_k_norm_g, 'swa_sink': swa_sink,
            'mla_q_lat_g': mla_q_lat_g, 'mla_kv_lat_g': mla_kv_lat_g, 'w_mla_uq': w_mla_uq,
            'w_mla_ukv': w_mla_ukv, 'mla_q_norm_g': mla_q_norm_g, 'mla_k_norm_g': mla_k_norm_g,
            'w_p_ssm': w_p_ssm, 'w_p_swa': w_p_swa, 'w_p_mla': w_p_mla, 'w_out': w_out,
            'w_ffn_in': w_ffn_in, 'w_ffn_out': w_ffn_out}


def reference(x, c, ctx, c_ctx, w_mod, b_mod, norm1_g, norm2_g, w_in,
              ssm_conv_w, ssm_conv_b, ssm_dt_bias, ssm_a_log, ssm_d, ssm_norm_g,
              swa_q_norm_g, swa_k_norm_g, swa_sink,
              mla_q_lat_g, mla_kv_lat_g, w_mla_uq, w_mla_ukv, mla_q_norm_g, mla_k_norm_g,
              w_p_ssm, w_p_swa, w_p_mla, w_out, w_ffn_in, w_ffn_out):
    n_lat = x.shape[1]
    rows = n_lat // GRID_W
    rope_swa = axial_rope_tables(rows, SWA_HEAD_DIM)
    rope_mla = axial_rope_tables(rows, MLA_ROPE)
    silu_c = jax.nn.silu(c)
    silu_cc = jax.nn.silu(c_ctx)[None]
    x_l, x_c = x, ctx
    for i in range(DEPTH):
        last = i == DEPTH - 1
        sh1, sc1, gt1, sh2, sc2, gt2 = jnp.split(silu_c @ w_mod[i] + b_mod[i], 6, axis=-1)
        csh1, csc1, cgt1, csh2, csc2, cgt2 = jnp.split(silu_cc @ w_mod[i] + b_mod[i], 6, axis=-1)

        h_l = modulate(x_l, norm1_g[i], sh1, sc1)
        h_c = modulate(x_c, norm1_g[i], csh1, csc1)
        (xbc_l, dt_l, k_l, v_l, ckv_l, kr_l, z_l, q_l, cq_l, gates_l) = split_cols(h_l @ w_in[i], IN_WIDTHS)
        if last:
            (xbc_c, dt_c, k_c, v_c, ckv_c, kr_c) = split_cols(h_c @ w_in[i][:, :KV_COLS], IN_WIDTHS[:N_KV_SPLITS])
            z_c = q_c = cq_c = gates_c = None
        else:
            (xbc_c, dt_c, k_c, v_c, ckv_c, kr_c, z_c, q_c, cq_c, gates_c) = split_cols(h_c @ w_in[i], IN_WIDTHS)

        y_ssm_l, y_ssm_c = ssd_branch(xbc_l, dt_l, z_l, xbc_c, dt_c, z_c, ssm_conv_w[i], ssm_conv_b[i],
                                      ssm_dt_bias[i], ssm_a_log[i], ssm_d[i], ssm_norm_g[i])
        y_swa_l, y_swa_c = swa_branch(q_l, k_l, v_l, q_c, k_c, v_c, swa_q_norm_g[i], swa_k_norm_g[i],
                                      swa_sink[i], rope_swa)
        y_mla_l, y_mla_c = mla_branch(cq_l, ckv_l, kr_l, cq_c, ckv_c, kr_c, mla_q_lat_g[i], mla_kv_lat_g[i],
                                      w_mla_uq[i], w_mla_ukv[i], mla_q_norm_g[i], mla_k_norm_g[i], rope_mla)

        x_l = x_l + gt1[:, None] * merge_branches(gates_l, y_ssm_l, y_swa_l, y_mla_l,
                                                  w_p_ssm[i], w_p_swa[i], w_p_mla[i], w_out[i])
        x_l = x_l + gt2[:, None] * swiglu(modulate(x_l, norm2_g[i], sh2, sc2), w_ffn_in[i], w_ffn_out[i])

        if not last:
            x_c = x_c + cgt1[:, None] * merge_branches(gates_c, y_ssm_c, y_swa_c, y_mla_c,
                                                      w_p_ssm[i], w_p_swa[i], w_p_mla[i], w_out[i])
            x_c = x_c + cgt2[:, None] * swiglu(modulate(x_c, norm2_g[i], csh2, csc2), w_ffn_in[i], w_ffn_out[i])
    return x_l
```

```python
import functools
import math

import jax
import jax.numpy as jnp
from jax import lax
from jax.experimental import pallas as pl
from jax.experimental.pallas import tpu as pltpu

F32 = jnp.float32
BF16 = jnp.bfloat16
HIGHEST = lax.Precision.HIGHEST

EPS = 1e-6
ROPE_BASE = 10000.0
GRID_W = 64

SSM_HEADS = 16
SSM_HEAD_DIM = 64
SSM_INNER = SSM_HEADS * SSM_HEAD_DIM
SSM_GROUPS = 2
SSM_STATE = 128
SSM_CONV = 5
SSM_CHUNK = 128
SSM_BC = SSM_GROUPS * SSM_STATE
SSM_CONV_DIM = SSM_INNER + 2 * SSM_BC
SSM_HPG = SSM_HEADS // SSM_GROUPS

SWA_Q_HEADS = 8
SWA_KV_HEADS = 2
SWA_HEAD_DIM = 128
SWA_WINDOW = 128
SWA_BLOCK = 128
SWA_GRP = SWA_Q_HEADS // SWA_KV_HEADS

MLA_HEADS = 8
MLA_Q_RANK = 384
MLA_KV_RANK = 256
MLA_NOPE = 128
MLA_ROPE = 64
MLA_QK = MLA_NOPE + MLA_ROPE
MLA_V = 128

N_BRANCH = 3
LANE = 128
SUBLANE = 8
HALO = SUBLANE
TOKEN_BLOCK = 256
VMEM_LIMIT = 56 * 1024 * 1024


def _cparams(n_axes):
    return pltpu.CompilerParams(
        dimension_semantics=("arbitrary",) * n_axes, vmem_limit_bytes=VMEM_LIMIT)


def _resident(shape):
    nd = len(shape)
    return pl.BlockSpec(shape, lambda *_: (0,) * nd, pipeline_mode=pl.Buffered(1))


def _rms(x, g):
    return x * lax.rsqrt(jnp.mean(x * x, axis=-1, keepdims=True) + EPS) * g


def _silu(x):
    return x * jax.nn.sigmoid(x)


def _bdot(a, b):
    return jnp.dot(a.astype(BF16), b.astype(BF16), preferred_element_type=F32)


def _bdot_nt(a, b):
    return lax.dot_general(a.astype(BF16), b.astype(BF16), (((1,), (1,)), ((), ())),
                           preferred_element_type=F32)


def _xdot(a, b):
    return jnp.dot(a, b, precision=HIGHEST, preferred_element_type=F32)


def _mod_kernel(c_ref, w_ref, b_ref, o_ref):
    o_ref[0] = _xdot(_silu(c_ref[...]), w_ref[0]) + b_ref[0]


def _modulation(cvec, w_mod, b_mod):
    depth, d, d6 = w_mod.shape
    rows = cvec.shape[0]
    return pl.pallas_call(
        _mod_kernel,
        grid=(depth, d6 // d),
        in_specs=[pl.BlockSpec((rows, d), lambda i, j: (0, 0)),
                  pl.BlockSpec((1, d, d), lambda i, j: (i, 0, j)),
                  pl.BlockSpec((1, 1, d), lambda i, j: (i, 0, j))],
        out_specs=pl.BlockSpec((1, rows, d), lambda i, j: (i, 0, j)),
        out_shape=jax.ShapeDtypeStruct((depth, rows, d6), F32),
        compiler_params=_cparams(2),
        name="modulation",
    )(cvec, w_mod, b_mod.reshape(depth, 1, d6))


IN_GROUPS = (
    ("xbc", SSM_CONV_DIM, F32),
    ("small", LANE, F32),
    ("swa_kv", 2 * SWA_KV_HEADS * SWA_HEAD_DIM, F32),
    ("ckv", MLA_KV_RANK, F32),
    ("z", SSM_INNER, F32),
    ("swa_q", SWA_Q_HEADS * SWA_HEAD_DIM, F32),
    ("cq", MLA_Q_RANK, F32),
    ("gates", None, F32),
)


def _inproj_kernel(x_ref, mod_ref, g_ref, w_ref, *out_refs, bounds):
    x = x_ref[0]
    mod = mod_ref[0, 0]
    h = _rms(x, g_ref[...]) * (1.0 + mod[1:2]) + mod[0:1]
    hb = h.astype(BF16)
    for o_ref, (lo, hi) in zip(out_refs, bounds):
        o_ref[0] = jnp.dot(hb, w_ref[:, lo:hi], preferred_element_type=F32).astype(o_ref.dtype)


def _in_projection(x_all, modsel, norm_g, w_perm, n_ctx_blocks):
    bsz, t, d = x_all.shape
    tm = TOKEN_BLOCK
    widths = [w if w is not None else N_BRANCH * d for _, w, _ in IN_GROUPS]
    offs = [0]
    for w in widths:
        offs.append(offs[-1] + w)
    bounds = tuple((offs[i], offs[i + 1]) for i in range(len(widths)))
    assert w_perm.shape == (d, offs[-1])
    out_shape = [jax.ShapeDtypeStruct((bsz, t, w), dt) for w, (_, _, dt) in zip(widths, IN_GROUPS)]
    out_specs = [pl.BlockSpec((1, tm, w), lambda b, i: (b, i, 0)) for w in widths]
    return pl.pallas_call(
        functools.partial(_inproj_kernel, bounds=bounds),
        grid=(bsz, t // tm),
        in_specs=[pl.BlockSpec((1, tm, d), lambda b, i: (b, i, 0)),
                  pl.BlockSpec((1, 1, SUBLANE, d),
                               lambda b, i: (b, jnp.where(i < n_ctx_blocks, 0, 1), 0, 0)),
                  _resident((1, d)),
                  _resident(w_perm.shape)],
        out_specs=out_specs,
        out_shape=out_shape,
        compiler_params=_cparams(2),
        name="in_projection",
    )(x_all, modsel, norm_g.reshape(1, d), w_perm)


def _permute_in_weights(w_in_l, d):
    o = [0]
    for w in (SSM_CONV_DIM, 2 * SSM_HEADS, SWA_KV_HEADS * SWA_HEAD_DIM, SWA_KV_HEADS * SWA_HEAD_DIM,
              MLA_KV_RANK, MLA_ROPE, SSM_INNER, SWA_Q_HEADS * SWA_HEAD_DIM, MLA_Q_RANK, N_BRANCH * d):
        o.append(o[-1] + w)
    xbc, dt, k, v, ckv, kr, z, q, cq, gates = (w_in_l[:, o[i]:o[i + 1]] for i in range(10))
    pad = jnp.zeros((d, LANE - 2 * SSM_HEADS - MLA_ROPE), w_in_l.dtype)
    return jnp.concatenate([xbc, dt, kr, pad, k, v, ckv, z, q, cq, gates], axis=1).astype(BF16)


def _ssd_scalars(small, dtb_ref, alog_ref):
    q = small.shape[0]
    lane = lax.broadcasted_iota(jnp.int32, (1, LANE), 1)
    raw = small + dtb_ref[...]
    dts = jnp.maximum(raw, 0.0) + jnp.log1p(jnp.exp(-jnp.abs(raw)))
    a = jnp.where(lane < 2 * SSM_HEADS, -jnp.exp(alog_ref[...]), 0.0)
    dta = dts * a
    ri = lax.broadcasted_iota(jnp.int32, (q, q), 0)
    ci = lax.broadcasted_iota(jnp.int32, (q, q), 1)
    tri = (ci <= ri).astype(F32)
    acs = _xdot(tri, dta)
    ecs = acs - dta
    return dts, acs, ecs, dts.T, acs.T, ecs.T


def _expand_matrix(first_row):
    r = lax.broadcasted_iota(jnp.int32, (LANE, SSM_INNER), 0)
    c = lax.broadcasted_iota(jnp.int32, (LANE, SSM_INNER), 1)
    return (c // SSM_HEAD_DIM + first_row == r).astype(F32)


def _ssd_direction(backward, xs, bm, cm, scal, state_ref):
    dts, acs, ecs, dts_t, acs_t, ecs_t = scal
    q = xs.shape[0]
    base = SSM_HEADS if backward else 0
    expand = _expand_matrix(base)
    tot = acs[q - 1:q, :]
    if backward:
        dec_in = jnp.exp(tot - ecs)
        w_out = jnp.exp(ecs) * dts
        pos, pos_t = ecs, ecs_t
    else:
        dec_in = jnp.exp(acs)
        w_out = jnp.exp(tot - acs) * dts
        pos, pos_t = acs, acs_t
    dec_e = _xdot(dec_in, expand)
    w_e = _xdot(w_out, expand)
    tot_e = _xdot(jnp.broadcast_to(jnp.exp(tot), (SUBLANE, LANE)), expand)[0:1]
    xw = (xs * w_e).astype(BF16)
    xb = xs.astype(BF16)
    ri = lax.broadcasted_iota(jnp.int32, (q, q), 0)
    ci = lax.broadcasted_iota(jnp.int32, (q, q), 1)
    keep = (ci >= ri) if backward else (ci <= ri)
    gw = SSM_HPG * SSM_HEAD_DIM
    ys = []
    for g in range(SSM_GROUPS):
        b_g = bm[:, g * SSM_STATE:(g + 1) * SSM_STATE]
        c_g = cm[:, g * SSM_STATE:(g + 1) * SSM_STATE].astype(BF16)
        b_t = b_g.T.astype(BF16)
        cb = jnp.dot(c_g, b_t, preferred_element_type=F32)
        st = state_ref[g]
        y_off = jnp.dot(c_g, st.astype(BF16), preferred_element_type=F32) * dec_e[:, g * gw:(g + 1) * gw]
        state_ref[g] = st * tot_e[:, g * gw:(g + 1) * gw] + jnp.dot(
            b_t, xw[:, g * gw:(g + 1) * gw], preferred_element_type=F32)
        heads = []
        for hh in range(SSM_HPG):
            h = g * SSM_HPG + hh
            col = pos[:, base + h:base + h + 1]
            row = pos_t[base + h:base + h + 1, :]
            diff = (row - col) if backward else (col - row)
            seg = jnp.exp(jnp.where(keep, diff, -jnp.inf))
            m = (cb * seg * dts_t[base + h:base + h + 1, :]).astype(BF16)
            heads.append(jnp.dot(m, xb[:, h * SSM_HEAD_DIM:(h + 1) * SSM_HEAD_DIM],
                                 preferred_element_type=F32))
        ys.append(jnp.concatenate(heads, axis=1) + y_off)
    return jnp.concatenate(ys, axis=1)


def _ssd_fwd_kernel(xc_ref, xp_ref, xn_ref, small_ref, cw_ref, cb_ref, dtb_ref, alog_ref, dskip_ref,
                    y_ref, u_ref, state_ref, *, n_ctx_chunks, n_chunks):
    c = pl.program_id(1)

    @pl.when(c == 0)
    def _():
        state_ref[...] = jnp.zeros_like(state_ref)

    prev_ok = jnp.logical_and(c != 0, c != n_ctx_chunks)
    next_ok = jnp.logical_and(c != n_ctx_chunks - 1, c != n_chunks - 1)
    xp = jnp.where(prev_ok, xp_ref[0], 0.0)
    xn = jnp.where(next_ok, xn_ref[0], 0.0)
    xc = xc_ref[0]
    q = xc.shape[0]
    ext = jnp.concatenate([xp, xc, xn], axis=0)
    half = SSM_CONV // 2
    acc = jnp.zeros_like(xc) + cb_ref[...]
    for k in range(SSM_CONV):
        lo = HALO - half + k
        acc = acc + ext[lo:lo + q, :] * cw_ref[k:k + 1, :]
    u = _silu(acc)
    u_ref[0] = u.astype(u_ref.dtype)
    xs = u[:, :SSM_INNER]
    bm = u[:, SSM_INNER:SSM_INNER + SSM_BC]
    cm = u[:, SSM_INNER + SSM_BC:]
    scal = _ssd_scalars(small_ref[0], dtb_ref, alog_ref)
    y = _ssd_direction(False, xs, bm, cm, scal, state_ref)
    y_ref[0] = y + dskip_ref[...] * xs


def _ssd_bwd_kernel(u_ref, small_ref, z_ref, yf_ref, dtb_ref, alog_ref, ng_ref, o_ref, state_ref):
    @pl.when(pl.program_id(1) == 0)
    def _():
        state_ref[...] = jnp.zeros_like(state_ref)

    u = u_ref[0].astype(F32)
    xs = u[:, :SSM_INNER]
    bm = u[:, SSM_INNER:SSM_INNER + SSM_BC]
    cm = u[:, SSM_INNER + SSM_BC:]
    scal = _ssd_scalars(small_ref[0], dtb_ref, alog_ref)
    y = yf_ref[0] + _ssd_direction(True, xs, bm, cm, scal, state_ref)
    o_ref[0] = _rms(y * _silu(z_ref[0]), ng_ref[...]).astype(o_ref.dtype)


def _ssd_branch(xbc, small, z, conv_w, conv_b, dt_bias, a_log, d_skip, norm_g, n_ctx):
    bsz, t, _ = xbc.shape
    q = SSM_CHUNK
    n_chunks = t // q
    n_ctx_chunks = n_ctx // q
    hb = q // HALO
    n_halo = t // HALO
    pad32 = LANE - 2 * SSM_HEADS
    dtb = jnp.pad(dt_bias.reshape(1, -1), ((0, 0), (0, pad32)))
    alog = jnp.pad(a_log.reshape(1, -1), ((0, 0), (0, pad32)))
    dskip = jnp.repeat(d_skip, SSM_HEAD_DIM).reshape(1, SSM_INNER)
    state = pltpu.VMEM((SSM_GROUPS, SSM_STATE, SSM_HPG * SSM_HEAD_DIM), F32)

    def chunk(width):
        return pl.BlockSpec((1, q, width), lambda b, c: (b, c, 0))

    y_f, u = pl.pallas_call(
        functools.partial(_ssd_fwd_kernel, n_ctx_chunks=n_ctx_chunks, n_chunks=n_chunks),
        grid=(bsz, n_chunks),
        in_specs=[chunk(SSM_CONV_DIM),
                  pl.BlockSpec((1, HALO, SSM_CONV_DIM), lambda b, c: (b, jnp.maximum(c * hb - 1, 0), 0)),
                  pl.BlockSpec((1, HALO, SSM_CONV_DIM),
                               lambda b, c: (b, jnp.minimum((c + 1) * hb, n_halo - 1), 0)),
                  chunk(LANE),
                  _resident((SSM_CONV, SSM_CONV_DIM)), _resident((1, SSM_CONV_DIM)),
                  _resident((1, LANE)), _resident((1, LANE)), _resident((1, SSM_INNER))],
        out_specs=[chunk(SSM_INNER), chunk(SSM_CONV_DIM)],
        out_shape=[jax.ShapeDtypeStruct((bsz, t, SSM_INNER), F32),
                   jax.ShapeDtypeStruct((bsz, t, SSM_CONV_DIM), BF16)],
        scratch_shapes=[state],
        compiler_params=_cparams(2),
        name="ssd_forward",
    )(xbc, xbc, xbc, small, conv_w, conv_b.reshape(1, -1), dtb, alog, dskip)

    def rchunk(width):
        return pl.BlockSpec(
            (1, q, width),
            lambda b, s: (b, jnp.where(s < n_ctx_chunks, n_ctx_chunks - 1 - s,
                                       n_chunks + n_ctx_chunks - 1 - s), 0))

    return pl.pallas_call(
        _ssd_bwd_kernel,
        grid=(bsz, n_chunks),
        in_specs=[rchunk(SSM_CONV_DIM), rchunk(LANE), rchunk(SSM_INNER), rchunk(SSM_INNER),
                  _resident((1, LANE)), _resident((1, LANE)), _resident((1, SSM_INNER))],
        out_specs=rchunk(SSM_INNER),
        out_shape=jax.ShapeDtypeStruct((bsz, t, SSM_INNER), BF16),
        scratch_shapes=[state],
        compiler_params=_cparams(2),
        name="ssd_backward",
    )(u, small, z, y_f, dtb, alog, norm_g.reshape(1, -1))


def _rope_tables(n_lat, n_ctx, rot_dim):
    n_freq = rot_dim // 4
    inv = jnp.power(ROPE_BASE, -jnp.arange(n_freq, dtype=F32) / n_freq)
    t = jnp.arange(n_lat)
    r = (t // GRID_W).astype(F32)[:, None] * inv
    col = (t % GRID_W).astype(F32)[:, None] * inv
    cos2 = jnp.concatenate([jnp.cos(r), jnp.cos(r), jnp.cos(col), jnp.cos(col)], axis=1)
    sin2 = jnp.concatenate([-jnp.sin(r), jnp.sin(r), -jnp.sin(col), jnp.sin(col)], axis=1)
    cos2 = jnp.concatenate([jnp.ones((n_ctx, rot_dim), F32), cos2], axis=0)
    sin2 = jnp.concatenate([jnp.zeros((n_ctx, rot_dim), F32), sin2], axis=0)
    return cos2, sin2


def _rotate_half(x):
    f = x.shape[-1] // 4
    return jnp.concatenate([x[:, f:2 * f], x[:, :f], x[:, 3 * f:], x[:, 2 * f:3 * f]], axis=1)


def _swa_prep_kernel(q_ref, kv_ref, qg_ref, kg_ref, cos_ref, sin_ref, qo_ref, ko_ref, vo_ref):
    cos, sin = cos_ref[...], sin_ref[...]
    scale = SWA_HEAD_DIM ** -0.5
    qf = q_ref[0]
    kvf = kv_ref[0]
    dh = SWA_HEAD_DIM
    for h in range(SWA_Q_HEADS):
        x = _rms(qf[:, h * dh:(h + 1) * dh], qg_ref[...])
        x = x * cos + _rotate_half(x) * sin
        qo_ref[0, :, h * dh:(h + 1) * dh] = (x * scale).astype(qo_ref.dtype)
    for h in range(SWA_KV_HEADS):
        x = _rms(kvf[:, h * dh:(h + 1) * dh], kg_ref[...])
        x = x * cos + _rotate_half(x) * sin
        ko_ref[0, :, h * dh:(h + 1) * dh] = x.astype(ko_ref.dtype)
    vo_ref[0] = kvf[:, SWA_KV_HEADS * dh:].astype(vo_ref.dtype)


def _swa_prep(q, kv, q_g, k_g, cos, sin):
    bsz, t, _ = q.shape
    tm = TOKEN_BLOCK
    dq = SWA_Q_HEADS * SWA_HEAD_DIM
    dkv = SWA_KV_HEADS * SWA_HEAD_DIM
    return pl.pallas_call(
        _swa_prep_kernel,
        grid=(bsz, t // tm),
        in_specs=[pl.BlockSpec((1, tm, dq), lambda b, i: (b, i, 0)),
                  pl.BlockSpec((1, tm, 2 * dkv), lambda b, i: (b, i, 0)),
                  _resident((1, SWA_HEAD_DIM)), _resident((1, SWA_HEAD_DIM)),
                  pl.BlockSpec((tm, SWA_HEAD_DIM), lambda b, i: (i, 0)),
                  pl.BlockSpec((tm, SWA_HEAD_DIM), lambda b, i: (i, 0))],
        out_specs=[pl.BlockSpec((1, tm, dq), lambda b, i: (b, i, 0)),
                   pl.BlockSpec((1, tm, dkv), lambda b, i: (b, i, 0)),
                   pl.BlockSpec((1, tm, dkv), lambda b, i: (b, i, 0))],
        out_shape=[jax.ShapeDtypeStruct((bsz, t, dq), BF16),
                   jax.ShapeDtypeStruct((bsz, t, dkv), BF16),
                   jax.ShapeDtypeStruct((bsz, t, dkv), BF16)],
        compiler_params=_cparams(2),
        name="swa_prep",
    )(q, kv, q_g.reshape(1, -1), k_g.reshape(1, -1), cos, sin)


def _swa_kernel(sink_ref, q_ref, k_ref, v_ref, o_ref, *, first_block, n_ctx, t):
    hk = pl.program_id(1)
    blk = pl.program_id(2) + first_block
    n_ctx_blocks = n_ctx // SWA_BLOCK
    bq = SWA_BLOCK
    win = 3 * SWA_BLOCK
    dh = SWA_HEAD_DIM
    qb = q_ref[0]
    q4 = jnp.concatenate([qb[:, g * dh:(g + 1) * dh] for g in range(SWA_GRP)], axis=0)
    start = jnp.clip((blk - 1) * bq, 0, t - win)
    start = pl.multiple_of(start, bq)
    kw = k_ref[0, pl.ds(start, win), :]
    vw = v_ref[0, pl.ds(start, win), :]
    kc = k_ref[0, 0:n_ctx, :]
    vc = v_ref[0, 0:n_ctx, :]
    s_w = _bdot_nt(q4, kw)
    s_c = _bdot_nt(q4, kc)
    rows = lax.broadcasted_iota(jnp.int32, (SWA_GRP * bq, win), 0)
    cols = lax.broadcasted_iota(jnp.int32, (SWA_GRP * bq, win), 1)
    qpos = (blk - n_ctx_blocks) * bq + (rows & (bq - 1))
    kpos = start - n_ctx + cols
    ok = (jnp.abs(kpos - qpos) <= SWA_WINDOW) & (kpos >= 0) & (blk >= n_ctx_blocks)
    s_w = jnp.where(ok, s_w, -jnp.inf)
    r1 = lax.broadcasted_iota(jnp.int32, (SWA_GRP * bq, 1), 0)
    sink = jnp.zeros((SWA_GRP * bq, 1), F32)
    for g in range(SWA_GRP):
        sink = jnp.where(r1 // bq == g, sink_ref[hk * SWA_GRP + g], sink)
    m = jnp.maximum(jnp.maximum(jnp.max(s_w, axis=-1, keepdims=True),
                                jnp.max(s_c, axis=-1, keepdims=True)), sink)
    p_w = jnp.exp(s_w - m)
    p_c = jnp.exp(s_c - m)
    den = (jnp.sum(p_w, axis=-1, keepdims=True) + jnp.sum(p_c, axis=-1, keepdims=True)
           + jnp.exp(sink - m))
    o = (_bdot(p_w, vw) + _bdot(p_c, vc)) / den
    for g in range(SWA_GRP):
        o_ref[0, :, g * dh:(g + 1) * dh] = o[g * bq:(g + 1) * bq, :].astype(o_ref.dtype)


def _swa_attention(q, k, v, sink, n_ctx, first_block):
    bsz, t, dq = q.shape
    bq = SWA_BLOCK
    gw = SWA_GRP * SWA_HEAD_DIM
    nblk = t // bq - first_block
    return pl.pallas_call(
        functools.partial(_swa_kernel, first_block=first_block, n_ctx=n_ctx, t=t),
        grid=(bsz, SWA_KV_HEADS, nblk),
        in_specs=[pl.BlockSpec(memory_space=pltpu.SMEM),
                  pl.BlockSpec((1, bq, gw), lambda b, h, n: (b, n + first_block, h)),
                  pl.BlockSpec((1, t, SWA_HEAD_DIM), lambda b, h, n: (b, 0, h)),
                  pl.BlockSpec((1, t, SWA_HEAD_DIM), lambda b, h, n: (b, 0, h))],
        out_specs=pl.BlockSpec((1, bq, gw), lambda b, h, n: (b, n + first_block, h)),
        out_shape=jax.ShapeDtypeStruct((bsz, t, dq), BF16),
        compiler_params=_cparams(3),
        name="swa_attention",
    )(sink, q, k, v)


def _mla_prep_kernel(cq_ref, ckv_ref, small_ref, wq_ref, wkv_ref, qlg_ref, kvlg_ref,
                     qgn_ref, qgr_ref, kgn_ref, kgr_ref, cos_ref, sin_ref, qo_ref, ko_ref, vo_ref):
    cos, sin = cos_ref[...], sin_ref[...]
    scale = MLA_QK ** -0.5
    nh, dn, dr, dv = MLA_HEADS, MLA_NOPE, MLA_ROPE, MLA_V

    def rope(x):
        return x * cos + _rotate_half(x) * sin

    qf = jnp.dot(_rms(cq_ref[0], qlg_ref[...]).astype(BF16), wq_ref[...], preferred_element_type=F32)
    for h in range(nh):
        qn = _rms(qf[:, h * dn:(h + 1) * dn], qgn_ref[...])
        qr = rope(_rms(qf[:, nh * dn + h * dr:nh * dn + (h + 1) * dr], qgr_ref[...]))
        qo_ref[0, h] = (jnp.concatenate([qn, qr], axis=1) * scale).astype(qo_ref.dtype)
    kvf = jnp.dot(_rms(ckv_ref[0], kvlg_ref[...]).astype(BF16), wkv_ref[...], preferred_element_type=F32)
    kr0 = 2 * SSM_HEADS
    kr = rope(_rms(small_ref[0][:, kr0:kr0 + dr], kgr_ref[...]))
    for h in range(nh):
        kn = _rms(kvf[:, h * dn:(h + 1) * dn], kgn_ref[...])
        ko_ref[0, h] = jnp.concatenate([kn, kr], axis=1).astype(ko_ref.dtype)
        vo_ref[0, h] = kvf[:, nh * dn + h * dv:nh * dn + (h + 1) * dv].astype(vo_ref.dtype)


def _mla_prep(cq, ckv, small, wq, wkv, q_lat_g, kv_lat_g, q_g, k_g, cos, sin):
    bsz, t, _ = cq.shape
    tm = TOKEN_BLOCK
    nh = MLA_HEADS

    def tok(width):
        return pl.BlockSpec((1, tm, width), lambda b, i: (b, i, 0))

    def head_out(width):
        return pl.BlockSpec((1, nh, tm, width), lambda b, i: (b, 0, i, 0))

    return pl.pallas_call(
        _mla_prep_kernel,
        grid=(bsz, t // tm),
        in_specs=[tok(MLA_Q_RANK), tok(MLA_KV_RANK), tok(LANE),
                  _resident(wq.shape), _resident(wkv.shape),
                  _resident((1, MLA_Q_RANK)), _resident((1, MLA_KV_RANK)),
                  _resident((1, MLA_NOPE)), _resident((1, MLA_ROPE)),
                  _resident((1, MLA_NOPE)), _resident((1, MLA_ROPE)),
                  pl.BlockSpec((tm, MLA_ROPE), lambda b, i: (i, 0)),
                  pl.BlockSpec((tm, MLA_ROPE), lambda b, i: (i, 0))],
        out_specs=[head_out(MLA_QK), head_out(MLA_QK), head_out(MLA_V)],
        out_shape=[jax.ShapeDtypeStruct((bsz, nh, t, MLA_QK), BF16),
                   jax.ShapeDtypeStruct((bsz, nh, t, MLA_QK), BF16),
                   jax.ShapeDtypeStruct((bsz, nh, t, MLA_V), BF16)],
        compiler_params=_cparams(2),
        name="mla_prep",
    )(cq, ckv, small, wq, wkv, q_lat_g.reshape(1, -1), kv_lat_g.reshape(1, -1),
      q_g[:MLA_NOPE].reshape(1, -1), q_g[MLA_NOPE:].reshape(1, -1),
      k_g[:MLA_NOPE].reshape(1, -1), k_g[MLA_NOPE:].reshape(1, -1), cos, sin)


def _mla_kernel(q_ref, k_ref, v_ref, o_ref, *, first_block, n_ctx, t, tk):
    blk = pl.program_id(2) + first_block
    q = q_ref[0, 0]
    tq = q.shape[0]

    def step(kk, vv, carry):
        m, l, acc = carry
        s = _bdot_nt(q, kk)
        m_new = jnp.maximum(m, jnp.max(s, axis=-1, keepdims=True))
        alpha = jnp.exp(m - m_new)
        p = jnp.exp(s - m_new)
        l = alpha * l + jnp.sum(p, axis=-1, keepdims=True)
        acc = alpha * acc + jnp.dot(p.astype(BF16), vv, preferred_element_type=F32)
        return m_new, l, acc

    init = (jnp.full((tq, 1), -jnp.inf, F32), jnp.zeros((tq, 1), F32), jnp.zeros((tq, MLA_V), F32))

    def finish(carry):
        _, l, acc = carry
        o_ref[0] = (acc / l).astype(o_ref.dtype)

    @pl.when(blk * tq < n_ctx)
    def _():
        finish(step(k_ref[0, 0, 0:n_ctx, :], v_ref[0, 0, 0:n_ctx, :], init))

    @pl.when(blk * tq >= n_ctx)
    def _():
        def body(j, carry):
            off = pl.multiple_of(j * tk, tk)
            return step(k_ref[0, 0, pl.ds(off, tk), :], v_ref[0, 0, pl.ds(off, tk), :], carry)
        finish(lax.fori_loop(0, t // tk, body, init))


def _mla_attention(q, k, v, n_ctx, first_block):
    bsz, nh, t, _ = q.shape
    tq = TOKEN_BLOCK
    tk = next(c for c in (768, 512, 384, 256, 128) if t % c == 0)
    nblk = t // tq - first_block
    return pl.pallas_call(
        functools.partial(_mla_kernel, first_block=first_block, n_ctx=n_ctx, t=t, tk=tk),
        grid=(bsz, nh, nblk),
        in_specs=[pl.BlockSpec((1, 1, tq, MLA_QK), lambda b, h, i: (b, h, i + first_block, 0)),
                  pl.BlockSpec((1, 1, t, MLA_QK), lambda b, h, i: (b, h, 0, 0)),
                  pl.BlockSpec((1, 1, t, MLA_V), lambda b, h, i: (b, h, 0, 0))],
        out_specs=pl.BlockSpec((1, tq, MLA_V), lambda b, h, i: (b, i + first_block, h)),
        out_shape=jax.ShapeDtypeStruct((bsz, t, nh * MLA_V), BF16),
        compiler_params=_cparams(3),
        name="mla_attention",
    )(q, k, v)


def _ffn_chunks(hidden):
    step = 512
    return tuple((lo, min(lo + step, hidden)) for lo in range(0, hidden, step))


def _merge_ffn_kernel(x_ref, mod_ref, gates_ref, ys_ref, yw_ref, ym_ref, wps_ref, wpw_ref, wpm_ref,
                      wo_ref, g2_ref, wfi_ref, wfo_ref, o_ref):
    d = x_ref.shape[-1]
    hidden = wfo_ref.shape[0]
    mod = mod_ref[0, 0]
    gt1, sh2, sc2, gt2 = mod[2:3], mod[3:4], mod[4:5], mod[5:6]
    gates = jax.nn.sigmoid(gates_ref[0])
    merged = (gates[:, 0:d] * jnp.dot(ys_ref[0], wps_ref[...], preferred_element_type=F32)
              + gates[:, d:2 * d] * jnp.dot(yw_ref[0], wpw_ref[...], preferred_element_type=F32)
              + gates[:, 2 * d:3 * d] * jnp.dot(ym_ref[0], wpm_ref[...], preferred_element_type=F32))
    x1 = x_ref[0] + gt1 * jnp.dot(merged.astype(BF16), wo_ref[...], preferred_element_type=F32)
    hb = (_rms(x1, g2_ref[...]) * (1.0 + sc2) + sh2).astype(BF16)
    acc = jnp.zeros_like(x1)
    for lo, hi in _ffn_chunks(hidden):
        gate = jnp.dot(hb, wfi_ref[:, lo:hi], preferred_element_type=F32)
        up = jnp.dot(hb, wfi_ref[:, hidden + lo:hidden + hi], preferred_element_type=F32)
        acc = acc + jnp.dot((_silu(gate) * up).astype(BF16), wfo_ref[lo:hi, :], preferred_element_type=F32)
    o_ref[0] = x1 + gt2 * acc


def _merge_ffn(x_all, modsel, gates, y_ssm, y_swa, y_mla, wps, wpw, wpm, wo, norm2_g, wfi, wfo,
               n_ctx_blocks, first_block):
    bsz, t, d = x_all.shape
    tm = TOKEN_BLOCK
    nblk = t // tm - first_block

    def tok(width):
        return pl.BlockSpec((1, tm, width), lambda b, i: (b, i + first_block, 0))

    return pl.pallas_call(
        _merge_ffn_kernel,
        grid=(bsz, nblk),
        in_specs=[tok(d),
                  pl.BlockSpec((1, 1, SUBLANE, d),
                               lambda b, i: (b, jnp.where(i + first_block < n_ctx_blocks, 0, 1), 0, 0)),
                  tok(N_BRANCH * d), tok(y_ssm.shape[-1]), tok(y_swa.shape[-1]), tok(y_mla.shape[-1]),
                  _resident(wps.shape), _resident(wpw.shape), _resident(wpm.shape), _resident(wo.shape),
                  _resident((1, d)), _resident(wfi.shape), _resident(wfo.shape)],
        out_specs=pl.BlockSpec((1, tm, d), lambda b, i: (b, i, 0)),
        out_shape=jax.ShapeDtypeStruct((bsz, nblk * tm, d), F32),
        compiler_params=_cparams(2),
        name="merge_ffn",
    )(x_all, modsel, gates, y_ssm, y_swa, y_mla, wps, wpw, wpm, wo, norm2_g.reshape(1, d), wfi, wfo)


def _head_major(w, n_heads, parts):
    k = w.shape[0]
    w = w.reshape(k, n_heads, sum(parts))
    out, lo = [], 0
    for p in parts:
        out.append(w[:, :, lo:lo + p].reshape(k, n_heads * p))
        lo += p
    return jnp.concatenate(out, axis=1)


def kernel(x, c, ctx, c_ctx, w_mod, b_mod, norm1_g, norm2_g, w_in, ssm_conv_w, ssm_conv_b, ssm_dt_bias,
           ssm_a_log, ssm_d, ssm_norm_g, swa_q_norm_g, swa_k_norm_g, swa_sink, mla_q_lat_g, mla_kv_lat_g,
           w_mla_uq, w_mla_ukv, mla_q_norm_g, mla_k_norm_g, w_p_ssm, w_p_swa, w_p_mla, w_out, w_ffn_in,
           w_ffn_out):
    bsz, n_lat, d = x.shape
    n_ctx = ctx.shape[1]
    depth = w_mod.shape[0]
    assert n_ctx % TOKEN_BLOCK == 0 and n_lat % TOKEN_BLOCK == 0 and n_lat % GRID_W == 0
    assert bsz + 1 <= SUBLANE
    n_ctx_blocks = n_ctx // TOKEN_BLOCK

    cvec = jnp.concatenate([c, c_ctx[None], jnp.zeros((SUBLANE - bsz - 1, d), F32)], axis=0)
    mods = _modulation(cvec, w_mod, b_mod).reshape(depth, SUBLANE, 6, d)
    cos_swa, sin_swa = _rope_tables(n_lat, n_ctx, SWA_HEAD_DIM)
    cos_mla, sin_mla = _rope_tables(n_lat, n_ctx, MLA_ROPE)

    x_all = jnp.concatenate([ctx, x], axis=1)
    for i in range(depth):
        last = i == depth - 1
        ctx_mod = jnp.broadcast_to(mods[i, bsz][None], (bsz, 6, d))
        modsel = jnp.pad(jnp.stack([ctx_mod, mods[i, :bsz]], axis=1), ((0, 0), (0, 0), (0, SUBLANE - 6), (0, 0)))

        xbc, small, swa_kv, ckv, z, swa_q, cq, gates = _in_projection(
            x_all, modsel, norm1_g[i], _permute_in_weights(w_in[i], d), n_ctx_blocks)

        y_ssm = _ssd_branch(xbc, small, z, ssm_conv_w[i], ssm_conv_b[i], ssm_dt_bias[i], ssm_a_log[i],
                            ssm_d[i], ssm_norm_g[i], n_ctx)

        first_swa = n_ctx // SWA_BLOCK if last else 0
        qs, ks, vs = _swa_prep(swa_q, swa_kv, swa_q_norm_g[i], swa_k_norm_g[i], cos_swa, sin_swa)
        y_swa = _swa_attention(qs, ks, vs, swa_sink[i], n_ctx, first_swa)

        first_tok = n_ctx_blocks if last else 0
        wq = _head_major(w_mla_uq[i], MLA_HEADS, (MLA_NOPE, MLA_ROPE)).astype(BF16)
        wkv = _head_major(w_mla_ukv[i], MLA_HEADS, (MLA_NOPE, MLA_V)).astype(BF16)
        qm, km, vm = _mla_prep(cq, ckv, small, wq, wkv, mla_q_lat_g[i], mla_kv_lat_g[i],
                               mla_q_norm_g[i], mla_k_norm_g[i], cos_mla, sin_mla)
        y_mla = _mla_attention(qm, km, vm, n_ctx, first_tok)

        x_all = _merge_ffn(x_all, modsel, gates, y_ssm, y_swa, y_mla,
                           w_p_ssm[i].astype(BF16), w_p_swa[i].astype(BF16), w_p_mla[i].astype(BF16),
                           w_out[i].astype(BF16), norm2_g[i], w_ffn_in[i].astype(BF16),
                           w_ffn_out[i].astype(BF16), n_ctx_blocks, first_tok)
    return x_all
```

```python
import functools
import math

import jax
import jax.numpy as jnp
from jax import lax
from jax.experimental import pallas as pl
from jax.experimental.pallas import tpu as pltpu

F32 = jnp.float32
BF16 = jnp.bfloat16
HIGHEST = lax.Precision.HIGHEST

EPS = 1e-6
ROPE_BASE = 10000.0
GRID_W = 64

SSM_HEADS = 16
SSM_HEAD_DIM = 64
SSM_INNER = SSM_HEADS * SSM_HEAD_DIM
SSM_GROUPS = 2
SSM_STATE = 128
SSM_CONV = 5
SSM_CHUNK = 128
SSM_BC = SSM_GROUPS * SSM_STATE
SSM_CONV_DIM = SSM_INNER + 2 * SSM_BC
SSM_HPG = SSM_HEADS // SSM_GROUPS

SWA_Q_HEADS = 8
SWA_KV_HEADS = 2
SWA_HEAD_DIM = 128
SWA_WINDOW = 128
SWA_BLOCK = 128
SWA_GRP = SWA_Q_HEADS // SWA_KV_HEADS

MLA_HEADS = 8
MLA_Q_RANK = 384
MLA_KV_RANK = 256
MLA_NOPE = 128
MLA_ROPE = 64
MLA_QK = MLA_NOPE + MLA_ROPE
MLA_V = 128

N_BRANCH = 3
LANE = 128
SUBLANE = 8
HALO = SUBLANE
TOKEN_BLOCK = 256
VMEM_LIMIT = 56 * 1024 * 1024


def _cparams(n_axes):
    return pltpu.CompilerParams(
        dimension_semantics=("arbitrary",) * n_axes, vmem_limit_bytes=VMEM_LIMIT)


def _resident(shape):
    nd = len(shape)
    return pl.BlockSpec(shape, lambda *_: (0,) * nd, pipeline_mode=pl.Buffered(1))


def _rms(x, g):
    return x * lax.rsqrt(jnp.mean(x * x, axis=-1, keepdims=True) + EPS) * g


def _silu(x):
    return x * jax.nn.sigmoid(x)


def _bdot(a, b):
    return jnp.dot(a.astype(BF16), b.astype(BF16), preferred_element_type=F32)


def _bdot_nt(a, b):
    return lax.dot_general(a.astype(BF16), b.astype(BF16), (((1,), (1,)), ((), ())),
                           preferred_element_type=F32)


def _xdot(a, b):
    return jnp.dot(a, b, precision=HIGHEST, preferred_element_type=F32)


def _mod_kernel(c_ref, w_ref, b_ref, o_ref):
    o_ref[0] = _xdot(_silu(c_ref[...]), w_ref[0]) + b_ref[0]


def _modulation(cvec, w_mod, b_mod):
    depth, d, d6 = w_mod.shape
    rows = cvec.shape[0]
    return pl.pallas_call(
        _mod_kernel,
        grid=(depth, d6 // d),
        in_specs=[pl.BlockSpec((rows, d), lambda i, j: (0, 0)),
                  pl.BlockSpec((1, d, d), lambda i, j: (i, 0, j)),
                  pl.BlockSpec((1, 1, d), lambda i, j: (i, 0, j))],
        out_specs=pl.BlockSpec((1, rows, d), lambda i, j: (i, 0, j)),
        out_shape=jax.ShapeDtypeStruct((depth, rows, d6), F32),
        compiler_params=_cparams(2),
        name="modulation",
    )(cvec, w_mod, b_mod.reshape(depth, 1, d6))


IN_GROUPS = (
    ("xbc", SSM_CONV_DIM, F32),
    ("small", LANE, F32),
    ("swa_kv", 2 * SWA_KV_HEADS * SWA_HEAD_DIM, F32),
    ("ckv", MLA_KV_RANK, F32),
    ("z", SSM_INNER, F32),
    ("swa_q", SWA_Q_HEADS * SWA_HEAD_DIM, F32),
    ("cq", MLA_Q_RANK, F32),
    ("gates", None, F32),
)


def _inproj_kernel(x_ref, mod_ref, g_ref, w_ref, *out_refs, bounds):
    x = x_ref[0]
    mod = mod_ref[0, 0]
    h = _rms(x, g_ref[...]) * (1.0 + mod[1:2]) + mod[0:1]
    hb = h.astype(BF16)
    for o_ref, (lo, hi) in zip(out_refs, bounds):
        o_ref[0] = jnp.dot(hb, w_ref[:, lo:hi], preferred_element_type=F32).astype(o_ref.dtype)


def _in_projection(x_all, modsel, norm_g, w_perm, n_ctx_blocks):
    bsz, t, d = x_all.shape
    tm = TOKEN_BLOCK
    widths = [w if w is not None else N_BRANCH * d for _, w, _ in IN_GROUPS]
    offs = [0]
    for w in widths:
        offs.append(offs[-1] + w)
    bounds = tuple((offs[i], offs[i + 1]) for i in range(len(widths)))
    assert w_perm.shape == (d, offs[-1])
    out_shape = [jax.ShapeDtypeStruct((bsz, t, w), dt) for w, (_, _, dt) in zip(widths, IN_GROUPS)]
    out_specs = [pl.BlockSpec((1, tm, w), lambda b, i: (b, i, 0)) for w in widths]
    return pl.pallas_call(
        functools.partial(_inproj_kernel, bounds=bounds),
        grid=(bsz, t // tm),
        in_specs=[pl.BlockSpec((1, tm, d), lambda b, i: (b, i, 0)),
                  pl.BlockSpec((1, 1, SUBLANE, d),
                               lambda b, i: (b, jnp.where(i < n_ctx_blocks, 0, 1), 0, 0)),
                  _resident((1, d)),
                  _resident(w_perm.shape)],
        out_specs=out_specs,
        out_shape=out_shape,
        compiler_params=_cparams(2),
        name="in_projection",
    )(x_all, modsel, norm_g.reshape(1, d), w_perm)


def _permute_in_weights(w_in_l, d):
    o = [0]
    for w in (SSM_CONV_DIM, 2 * SSM_HEADS, SWA_KV_HEADS * SWA_HEAD_DIM, SWA_KV_HEADS * SWA_HEAD_DIM,
              MLA_KV_RANK, MLA_ROPE, SSM_INNER, SWA_Q_HEADS * SWA_HEAD_DIM, MLA_Q_RANK, N_BRANCH * d):
        o.append(o[-1] + w)
    xbc, dt, k, v, ckv, kr, z, q, cq, gates = (w_in_l[:, o[i]:o[i + 1]] for i in range(10))
    pad = jnp.zeros((d, LANE - 2 * SSM_HEADS - MLA_ROPE), w_in_l.dtype)
    return jnp.concatenate([xbc, dt, kr, pad, k, v, ckv, z, q, cq, gates], axis=1).astype(BF16)


def _ssd_scalars(small, dtb_ref, alog_ref):
    q = small.shape[0]
    lane = lax.broadcasted_iota(jnp.int32, (1, LANE), 1)
    raw = small + dtb_ref[...]
    dts = jnp.maximum(raw, 0.0) + jnp.log1p(jnp.exp(-jnp.abs(raw)))
    a = jnp.where(lane < 2 * SSM_HEADS, -jnp.exp(alog_ref[...]), 0.0)
    dta = dts * a
    ri = lax.broadcasted_iota(jnp.int32, (q, q), 0)
    ci = lax.broadcasted_iota(jnp.int32, (q, q), 1)
    tri = (ci <= ri).astype(F32)
    acs = _xdot(tri, dta)
    ecs = acs - dta
    return dts, acs, ecs, dts.T, acs.T, ecs.T


def _expand_matrix(first_row):
    r = lax.broadcasted_iota(jnp.int32, (LANE, SSM_INNER), 0)
    c = lax.broadcasted_iota(jnp.int32, (LANE, SSM_INNER), 1)
    return (c // SSM_HEAD_DIM + first_row == r).astype(F32)


def _ssd_direction(backward, xs, bm, cm, scal, state_ref):
    dts, acs, ecs, dts_t, acs_t, ecs_t = scal
    q = xs.shape[0]
    base = SSM_HEADS if backward else 0
    expand = _expand_matrix(base)
    tot = acs[q - 1:q, :]
    if backward:
        dec_in = jnp.exp(tot - ecs)
        w_out = jnp.exp(ecs) * dts
        pos, pos_t = ecs, ecs_t
    else:
        dec_in = jnp.exp(acs)
        w_out = jnp.exp(tot - acs) * dts
        pos, pos_t = acs, acs_t
    dec_e = _xdot(dec_in, expand)
    w_e = _xdot(w_out, expand)
    tot_e = _xdot(jnp.broadcast_to(jnp.exp(tot), (SUBLANE, LANE)), expand)[0:1]
    xw = (xs * w_e).astype(BF16)
    xb = xs.astype(BF16)
    ri = lax.broadcasted_iota(jnp.int32, (q, q), 0)
    ci = lax.broadcasted_iota(jnp.int32, (q, q), 1)
    keep = (ci >= ri) if backward else (ci <= ri)
    gw = SSM_HPG * SSM_HEAD_DIM
    ys = []
    for g in range(SSM_GROUPS):
        b_g = bm[:, g * SSM_STATE:(g + 1) * SSM_STATE]
        c_g = cm[:, g * SSM_STATE:(g + 1) * SSM_STATE].astype(BF16)
        b_t = b_g.T.astype(BF16)
        cb = jnp.dot(c_g, b_t, preferred_element_type=F32)
        st = state_ref[g]
        y_off = jnp.dot(c_g, st.astype(BF16), preferred_element_type=F32) * dec_e[:, g * gw:(g + 1) * gw]
        state_ref[g] = st * tot_e[:, g * gw:(g + 1) * gw] + jnp.dot(
            b_t, xw[:, g * gw:(g + 1) * gw], preferred_element_type=F32)
        heads = []
        for hh in range(SSM_HPG):
            h = g * SSM_HPG + hh
            col = pos[:, base + h:base + h + 1]
            row = pos_t[base + h:base + h + 1, :]
            diff = (row - col) if backward else (col - row)
            seg = jnp.exp(jnp.where(keep, diff, -jnp.inf))
            m = (cb * seg * dts_t[base + h:base + h + 1, :]).astype(BF16)
            heads.append(jnp.dot(m, xb[:, h * SSM_HEAD_DIM:(h + 1) * SSM_HEAD_DIM],
                                 preferred_element_type=F32))
        ys.append(jnp.concatenate(heads, axis=1) + y_off)
    return jnp.concatenate(ys, axis=1)


def _ssd_fwd_kernel(xc_ref, xp_ref, xn_ref, small_ref, cw_ref, cb_ref, dtb_ref, alog_ref, dskip_ref,
                    y_ref, u_ref, state_ref, *, n_ctx_chunks, n_chunks):
    c = pl.program_id(1)

    @pl.when(c == 0)
    def _():
        state_ref[...] = jnp.zeros_like(state_ref)

    prev_ok = jnp.logical_and(c != 0, c != n_ctx_chunks)
    next_ok = jnp.logical_and(c != n_ctx_chunks - 1, c != n_chunks - 1)
    xp = jnp.where(prev_ok, xp_ref[0], 0.0)
    xn = jnp.where(next_ok, xn_ref[0], 0.0)
    xc = xc_ref[0]
    q = xc.shape[0]
    ext = jnp.concatenate([xp, xc, xn], axis=0)
    half = SSM_CONV // 2
    acc = jnp.zeros_like(xc) + cb_ref[...]
    for k in range(SSM_CONV):
        lo = HALO - half + k
        acc = acc + ext[lo:lo + q, :] * cw_ref[k:k + 1, :]
    u = _silu(acc)
    u_ref[0] = u.astype(u_ref.dtype)
    xs = u[:, :SSM_INNER]
    bm = u[:, SSM_INNER:SSM_INNER + SSM_BC]
    cm = u[:, SSM_INNER + SSM_BC:]
    scal = _ssd_scalars(small_ref[0], dtb_ref, alog_ref)
    y = _ssd_direction(False, xs, bm, cm, scal, state_ref)
    y_ref[0] = y + dskip_ref[...] * xs


def _ssd_bwd_kernel(u_ref, small_ref, z_ref, yf_ref, dtb_ref, alog_ref, ng_ref, o_ref, state_ref):
    @pl.when(pl.program_id(1) == 0)
    def _():
        state_ref[...] = jnp.zeros_like(state_ref)

    u = u_ref[0].astype(F32)
    xs = u[:, :SSM_INNER]
    bm = u[:, SSM_INNER:SSM_INNER + SSM_BC]
    cm = u[:, SSM_INNER + SSM_BC:]
    scal = _ssd_scalars(small_ref[0], dtb_ref, alog_ref)
    y = yf_ref[0] + _ssd_direction(True, xs, bm, cm, scal, state_ref)
    o_ref[0] = _rms(y * _silu(z_ref[0]), ng_ref[...]).astype(o_ref.dtype)


def _ssd_branch(xbc, small, z, conv_w, conv_b, dt_bias, a_log, d_skip, norm_g, n_ctx):
    bsz, t, _ = xbc.shape
    q = SSM_CHUNK
    n_chunks = t // q
    n_ctx_chunks = n_ctx // q
    hb = q // HALO
    n_halo = t // HALO
    pad32 = LANE - 2 * SSM_HEADS
    dtb = jnp.pad(dt_bias.reshape(1, -1), ((0, 0), (0, pad32)))
    alog = jnp.pad(a_log.reshape(1, -1), ((0, 0), (0, pad32)))
    dskip = jnp.repeat(d_skip, SSM_HEAD_DIM).reshape(1, SSM_INNER)
    state = pltpu.VMEM((SSM_GROUPS, SSM_STATE, SSM_HPG * SSM_HEAD_DIM), F32)

    def chunk(width):
        return pl.BlockSpec((1, q, width), lambda b, c: (b, c, 0))

    y_f, u = pl.pallas_call(
        functools.partial(_ssd_fwd_kernel, n_ctx_chunks=n_ctx_chunks, n_chunks=n_chunks),
        grid=(bsz, n_chunks),
        in_specs=[chunk(SSM_CONV_DIM),
                  pl.BlockSpec((1, HALO, SSM_CONV_DIM), lambda b, c: (b, jnp.maximum(c * hb - 1, 0), 0)),
                  pl.BlockSpec((1, HALO, SSM_CONV_DIM),
                               lambda b, c: (b, jnp.minimum((c + 1) * hb, n_halo - 1), 0)),
                  chunk(LANE),
                  _resident((SSM_CONV, SSM_CONV_DIM)), _resident((1, SSM_CONV_DIM)),
                  _resident((1, LANE)), _resident((1, LANE)), _resident((1, SSM_INNER))],
        out_specs=[chunk(SSM_INNER), chunk(SSM_CONV_DIM)],
        out_shape=[jax.ShapeDtypeStruct((bsz, t, SSM_INNER), F32),
                   jax.ShapeDtypeStruct((bsz, t, SSM_CONV_DIM), BF16)],
        scratch_shapes=[state],
        compiler_params=_cparams(2),
        name="ssd_forward",
    )(xbc, xbc, xbc, small, conv_w, conv_b.reshape(1, -1), dtb, alog, dskip)

    def rchunk(width):
        return pl.BlockSpec(
            (1, q, width),
            lambda b, s: (b, jnp.where(s < n_ctx_chunks, n_ctx_chunks - 1 - s,
                                       n_chunks + n_ctx_chunks - 1 - s), 0))

    return pl.pallas_call(
        _ssd_bwd_kernel,
        grid=(bsz, n_chunks),
        in_specs=[rchunk(SSM_CONV_DIM), rchunk(LANE), rchunk(SSM_INNER), rchunk(SSM_INNER),
                  _resident((1, LANE)), _resident((1, LANE)), _resident((1, SSM_INNER))],
        out_specs=rchunk(SSM_INNER),
        out_shape=jax.ShapeDtypeStruct((bsz, t, SSM_INNER), BF16),
        scratch_shapes=[state],
        compiler_params=_cparams(2),
        name="ssd_backward",
    )(u, small, z, y_f, dtb, alog, norm_g.reshape(1, -1))


def _rope_tables(n_lat, n_ctx, rot_dim):
    n_freq = rot_dim // 4
    inv = jnp.power(ROPE_BASE, -jnp.arange(n_freq, dtype=F32) / n_freq)
    t = jnp.arange(n_lat)
    r = (t // GRID_W).astype(F32)[:, None] * inv
    col = (t % GRID_W).astype(F32)[:, None] * inv
    cos2 = jnp.concatenate([jnp.cos(r), jnp.cos(r), jnp.cos(col), jnp.cos(col)], axis=1)
    sin2 = jnp.concatenate([-jnp.sin(r), jnp.sin(r), -jnp.sin(col), jnp.sin(col)], axis=1)
    cos2 = jnp.concatenate([jnp.ones((n_ctx, rot_dim), F32), cos2], axis=0)
    sin2 = jnp.concatenate([jnp.zeros((n_ctx, rot_dim), F32), sin2], axis=0)
    return cos2, sin2


def _rotate_half(x):
    f = x.shape[-1] // 4
    return jnp.concatenate([x[:, f:2 * f], x[:, :f], x[:, 3 * f:], x[:, 2 * f:3 * f]], axis=1)


def _swa_prep_kernel(q_ref, kv_ref, qg_ref, kg_ref, cos_ref, sin_ref, qo_ref, ko_ref, vo_ref):
    cos, sin = cos_ref[...], sin_ref[...]
    scale = SWA_HEAD_DIM ** -0.5
    qf = q_ref[0]
    kvf = kv_ref[0]
    dh = SWA_HEAD_DIM
    for h in range(SWA_Q_HEADS):
        x = _rms(qf[:, h * dh:(h + 1) * dh], qg_ref[...])
        x = x * cos + _rotate_half(x) * sin
        qo_ref[0, :, h * dh:(h + 1) * dh] = (x * scale).astype(qo_ref.dtype)
    for h in range(SWA_KV_HEADS):
        x = _rms(kvf[:, h * dh:(h + 1) * dh], kg_ref[...])
        x = x * cos + _rotate_half(x) * sin
        ko_ref[0, :, h * dh:(h + 1) * dh] = x.astype(ko_ref.dtype)
    vo_ref[0] = kvf[:, SWA_KV_HEADS * dh:].astype(vo_ref.dtype)


def _swa_prep(q, kv, q_g, k_g, cos, sin):
    bsz, t, _ = q.shape
    tm = TOKEN_BLOCK
    dq = SWA_Q_HEADS * SWA_HEAD_DIM
    dkv = SWA_KV_HEADS * SWA_HEAD_DIM
    return pl.pallas_call(
        _swa_prep_kernel,
        grid=(bsz, t // tm),
        in_specs=[pl.BlockSpec((1, tm, dq), lambda b, i: (b, i, 0)),
                  pl.BlockSpec((1, tm, 2 * dkv), lambda b, i: (b, i, 0)),
                  _resident((1, SWA_HEAD_DIM)), _resident((1, SWA_HEAD_DIM)),
                  pl.BlockSpec((tm, SWA_HEAD_DIM), lambda b, i: (i, 0)),
                  pl.BlockSpec((tm, SWA_HEAD_DIM), lambda b, i: (i, 0))],
        out_specs=[pl.BlockSpec((1, tm, dq), lambda b, i: (b, i, 0)),
                   pl.BlockSpec((1, tm, dkv), lambda b, i: (b, i, 0)),
                   pl.BlockSpec((1, tm, dkv), lambda b, i: (b, i, 0))],
        out_shape=[jax.ShapeDtypeStruct((bsz, t, dq), BF16),
                   jax.ShapeDtypeStruct((bsz, t, dkv), BF16),
                   jax.ShapeDtypeStruct((bsz, t, dkv), BF16)],
        compiler_params=_cparams(2),
        name="swa_prep",
    )(q, kv, q_g.reshape(1, -1), k_g.reshape(1, -1), cos, sin)


def _swa_kernel(sink_ref, q_ref, k_ref, v_ref, o_ref, *, first_block, n_ctx, t):
    hk = pl.program_id(1)
    blk = pl.program_id(2) + first_block
    n_ctx_blocks = n_ctx // SWA_BLOCK
    bq = SWA_BLOCK
    win = 3 * SWA_BLOCK
    dh = SWA_HEAD_DIM
    qb = q_ref[0]
    q4 = jnp.concatenate([qb[:, g * dh:(g + 1) * dh] for g in range(SWA_GRP)], axis=0)
    start = jnp.clip((blk - 1) * bq, 0, t - win)
    start = pl.multiple_of(start, bq)
    kw = k_ref[0, pl.ds(start, win), :]
    vw = v_ref[0, pl.ds(start, win), :]
    kc = k_ref[0, 0:n_ctx, :]
    vc = v_ref[0, 0:n_ctx, :]
    s_w = _bdot_nt(q4, kw)
    s_c = _bdot_nt(q4, kc)
    rows = lax.broadcasted_iota(jnp.int32, (SWA_GRP * bq, win), 0)
    cols = lax.broadcasted_iota(jnp.int32, (SWA_GRP * bq, win), 1)
    qpos = (blk - n_ctx_blocks) * bq + (rows & (bq - 1))
    kpos = start - n_ctx + cols
    ok = (jnp.abs(kpos - qpos) <= SWA_WINDOW) & (kpos >= 0) & (blk >= n_ctx_blocks)
    s_w = jnp.where(ok, s_w, -jnp.inf)
    r1 = lax.broadcasted_iota(jnp.int32, (SWA_GRP * bq, 1), 0)
    sink = jnp.zeros((SWA_GRP * bq, 1), F32)
    for g in range(SWA_GRP):
        sink = jnp.where(r1 // bq == g, sink_ref[hk * SWA_GRP + g], sink)
    m = jnp.maximum(jnp.maximum(jnp.max(s_w, axis=-1, keepdims=True),
                                jnp.max(s_c, axis=-1, keepdims=True)), sink)
    p_w = jnp.exp(s_w - m)
    p_c = jnp.exp(s_c - m)
    den = (jnp.sum(p_w, axis=-1, keepdims=True) + jnp.sum(p_c, axis=-1, keepdims=True)
           + jnp.exp(sink - m))
    o = (_bdot(p_w, vw) + _bdot(p_c, vc)) / den
    for g in range(SWA_GRP):
        o_ref[0, :, g * dh:(g + 1) * dh] = o[g * bq:(g + 1) * bq, :].astype(o_ref.dtype)


def _swa_attention(q, k, v, sink, n_ctx, first_block):
    bsz, t, dq = q.shape
    bq = SWA_BLOCK
    gw = SWA_GRP * SWA_HEAD_DIM
    nblk = t // bq - first_block
    return pl.pallas_call(
        functools.partial(_swa_kernel, first_block=first_block, n_ctx=n_ctx, t=t),
        grid=(bsz, SWA_KV_HEADS, nblk),
        in_specs=[pl.BlockSpec(memory_space=pltpu.SMEM),
                  pl.BlockSpec((1, bq, gw), lambda b, h, n: (b, n + first_block, h)),
                  pl.BlockSpec((1, t, SWA_HEAD_DIM), lambda b, h, n: (b, 0, h)),
                  pl.BlockSpec((1, t, SWA_HEAD_DIM), lambda b, h, n: (b, 0, h))],
        out_specs=pl.BlockSpec((1, bq, gw), lambda b, h, n: (b, n + first_block, h)),
        out_shape=jax.ShapeDtypeStruct((bsz, t, dq), BF16),
        compiler_params=_cparams(3),
        name="swa_attention",
    )(sink, q, k, v)


def _rms_cols(x, g):
    return x * lax.rsqrt(jnp.mean(x * x, axis=0, keepdims=True) + EPS) * g


def _rotate_half_rows(x):
    f = x.shape[0] // 4
    return jnp.concatenate([x[f:2 * f], x[:f], x[3 * f:], x[2 * f:3 * f]], axis=0)


def _mla_prep_kernel(cq_ref, ckv_ref, small_ref, wqt_ref, wk_ref, wvt_ref, qlg_ref, kvlg_ref,
                     qgn_ref, qgr_ref, kgn_ref, kgr_ref, cos_ref, sin_ref, cost_ref, sint_ref,
                     qo_ref, ko_ref, vo_ref):
    qscale = MLA_QK ** -0.5 * math.log2(math.e)
    nh, dn, dr, dv = MLA_HEADS, MLA_NOPE, MLA_ROPE, MLA_V

    cqn_t = _rms(cq_ref[0], qlg_ref[...]).T.astype(BF16)
    qf_t = jnp.dot(wqt_ref[...], cqn_t, preferred_element_type=F32)
    cos_t, sin_t = cost_ref[...], sint_ref[...]
    for h in range(nh):
        qn = _rms_cols(qf_t[h * dn:(h + 1) * dn], qgn_ref[...])
        qr = _rms_cols(qf_t[nh * dn + h * dr:nh * dn + (h + 1) * dr], qgr_ref[...])
        qr = qr * cos_t + _rotate_half_rows(qr) * sin_t
        qo_ref[0, h, 0:dn, :] = (qn * qscale).astype(qo_ref.dtype)
        qo_ref[0, h, dn:dn + dr, :] = (qr * qscale).astype(qo_ref.dtype)

    ckvn = _rms(ckv_ref[0], kvlg_ref[...])
    kf = jnp.dot(ckvn.astype(BF16), wk_ref[...], preferred_element_type=F32)
    kr0 = 2 * SSM_HEADS
    kr = _rms(small_ref[0][:, kr0:kr0 + dr], kgr_ref[...])
    kr = kr * cos_ref[...] + _rotate_half(kr) * sin_ref[...]
    for h in range(nh):
        kn = _rms(kf[:, h * dn:(h + 1) * dn], kgn_ref[...])
        ko_ref[0, h] = jnp.concatenate([kn, kr], axis=1).astype(ko_ref.dtype)
    vf_t = jnp.dot(wvt_ref[...], ckvn.T.astype(BF16), preferred_element_type=F32)
    for h in range(nh):
        vo_ref[0, h, 0] = vf_t[h * dv:(h + 1) * dv].astype(vo_ref.dtype)


def _mla_prep(cq, ckv, small, w_uq, w_ukv, q_lat_g, kv_lat_g, q_g, k_g, cos, sin):
    bsz, t, _ = cq.shape
    tm = TOKEN_BLOCK
    nh = MLA_HEADS
    wq_t = _head_major(w_uq, nh, (MLA_NOPE, MLA_ROPE)).T.astype(BF16)
    wkv = _head_major(w_ukv, nh, (MLA_NOPE, MLA_V))
    wk = wkv[:, :nh * MLA_NOPE].astype(BF16)
    wv_t = wkv[:, nh * MLA_NOPE:].T.astype(BF16)

    def tok(width):
        return pl.BlockSpec((1, tm, width), lambda b, i: (b, i, 0))

    def cols(g):
        return jnp.broadcast_to(g[:, None], (g.shape[0], tm))

    return pl.pallas_call(
        _mla_prep_kernel,
        grid=(bsz, t // tm),
        in_specs=[tok(MLA_Q_RANK), tok(MLA_KV_RANK), tok(LANE),
                  _resident(wq_t.shape), _resident(wk.shape), _resident(wv_t.shape),
                  _resident((1, MLA_Q_RANK)), _resident((1, MLA_KV_RANK)),
                  _resident((MLA_NOPE, tm)), _resident((MLA_ROPE, tm)),
                  _resident((1, MLA_NOPE)), _resident((1, MLA_ROPE)),
                  pl.BlockSpec((tm, MLA_ROPE), lambda b, i: (i, 0)),
                  pl.BlockSpec((tm, MLA_ROPE), lambda b, i: (i, 0)),
                  pl.BlockSpec((MLA_ROPE, tm), lambda b, i: (0, i)),
                  pl.BlockSpec((MLA_ROPE, tm), lambda b, i: (0, i))],
        out_specs=[pl.BlockSpec((1, nh, MLA_QK, tm), lambda b, i: (b, 0, 0, i)),
                   pl.BlockSpec((1, nh, tm, MLA_QK), lambda b, i: (b, 0, i, 0)),
                   pl.BlockSpec((1, nh, 1, MLA_V, tm), lambda b, i: (b, 0, i, 0, 0))],
        out_shape=[jax.ShapeDtypeStruct((bsz, nh, MLA_QK, t), BF16),
                   jax.ShapeDtypeStruct((bsz, nh, t, MLA_QK), BF16),
                   jax.ShapeDtypeStruct((bsz, nh, t // tm, MLA_V, tm), BF16)],
        compiler_params=_cparams(2),
        name="mla_prep",
    )(cq, ckv, small, wq_t, wk, wv_t, q_lat_g.reshape(1, -1), kv_lat_g.reshape(1, -1),
      cols(q_g[:MLA_NOPE]), cols(q_g[MLA_NOPE:]),
      k_g[:MLA_NOPE].reshape(1, -1), k_g[MLA_NOPE:].reshape(1, -1), cos, sin, cos.T, sin.T)


MLA_HEADS_PER_STEP = 2
MLA_KEY_SUBBLOCKS = 3


def _mla_kernel(qt_ref, k_ref, vt_ref, o_ref, s_scr, p_scr, a_scr, acc_scr, m_scr, l_scr, *,
                first_block, n_ctx_blocks, n_key_blocks, sub):
    blk = pl.program_id(2) + first_block
    heads = qt_ref.shape[1]
    kb = vt_ref.shape[4]
    tk = sub * kb
    n_chunks = n_key_blocks // sub

    def softmax_update(c, s):
        m_prev = m_scr[c]
        m_new = jnp.maximum(m_prev, jnp.max(s, axis=0, keepdims=True))
        alpha = jnp.exp2(m_prev - m_new)
        p = jnp.exp2(s - m_new)
        l_scr[c] = alpha * l_scr[c] + jnp.sum(p, axis=0, keepdims=True)
        m_scr[c] = m_new
        return alpha, p.astype(BF16)

    def value_update(c, alpha, pb, blk0, nsub):
        pv = jnp.dot(vt_ref[0, c, blk0], pb[0:kb], preferred_element_type=F32)
        for u in range(1, nsub):
            pv = pv + jnp.dot(vt_ref[0, c, blk0 + u], pb[u * kb:(u + 1) * kb], preferred_element_type=F32)
        acc_scr[c] = alpha * acc_scr[c] + pv

    def stage_scores(c, t, slot):
        row0 = t * tk if isinstance(t, int) else pl.multiple_of(t * tk, tk)
        s_scr[c, slot] = jnp.dot(k_ref[0, c, pl.ds(row0, tk), :], qt_ref[0, c], preferred_element_type=F32)

    def stage_softmax(c, slot):
        alpha, pb = softmax_update(c, s_scr[c, slot])
        a_scr[c, slot] = alpha
        p_scr[c, slot] = pb

    def stage_values(c, t, slot):
        value_update(c, a_scr[c, slot], p_scr[c, slot], t * sub, sub)

    def tick(t, parity):
        static = isinstance(t, int)
        for c in range(heads):
            if not static or t < n_chunks:
                stage_scores(c, t, parity)
            if not static or 1 <= t <= n_chunks:
                stage_softmax(c, 1 - parity)
            if not static or 2 <= t <= n_chunks + 1:
                stage_values(c, t - 2, parity)

    def reset():
        m_scr[...] = jnp.full(m_scr.shape, -jnp.inf, F32)
        l_scr[...] = jnp.zeros(l_scr.shape, F32)
        acc_scr[...] = jnp.zeros(acc_scr.shape, F32)

    def finish():
        for c in range(heads):
            o_ref[0, :, c * MLA_V:(c + 1) * MLA_V] = (acc_scr[c] / l_scr[c]).T.astype(o_ref.dtype)

    @pl.when(blk < n_ctx_blocks)
    def _():
        reset()
        for c in range(heads):
            s = jnp.dot(k_ref[0, c, 0:n_ctx_blocks * kb, :], qt_ref[0, c], preferred_element_type=F32)
            alpha, pb = softmax_update(c, s)
            value_update(c, alpha, pb, 0, n_ctx_blocks)
        finish()

    @pl.when(blk >= n_ctx_blocks)
    def _():
        reset()
        n_pairs = max(n_chunks - 2, 0) // 2
        for t in range(0, min(2, n_chunks + 2)):
            tick(t, t % 2)

        def body(i, carry):
            t0 = 2 + 2 * i
            tick(t0, 0)
            tick(t0 + 1, 1)
            return carry

        lax.fori_loop(0, n_pairs, body, 0)
        for t in range(2 + 2 * n_pairs, n_chunks + 2):
            tick(t, t % 2)
        finish()


def _mla_attention(qt, k, vt, n_ctx, first_block):
    bsz, nh, dqk, t = qt.shape
    tq = TOKEN_BLOCK
    kb = vt.shape[-1]
    n_key_blocks = t // kb
    sub = MLA_KEY_SUBBLOCKS if n_key_blocks % MLA_KEY_SUBBLOCKS == 0 else 1
    g = MLA_HEADS_PER_STEP
    nblk = t // tq - first_block
    return pl.pallas_call(
        functools.partial(_mla_kernel, first_block=first_block, n_ctx_blocks=n_ctx // kb,
                          n_key_blocks=n_key_blocks, sub=sub),
        grid=(bsz, nh // g, nblk),
        in_specs=[pl.BlockSpec((1, g, dqk, tq), lambda b, h, i: (b, h, 0, i + first_block)),
                  pl.BlockSpec((1, g, t, dqk), lambda b, h, i: (b, h, 0, 0)),
                  pl.BlockSpec((1, g, n_key_blocks, MLA_V, kb), lambda b, h, i: (b, h, 0, 0, 0))],
        out_specs=pl.BlockSpec((1, tq, g * MLA_V), lambda b, h, i: (b, i + first_block, h)),
        out_shape=jax.ShapeDtypeStruct((bsz, t, nh * MLA_V), BF16),
        scratch_shapes=[pltpu.VMEM((g, 2, sub * kb, tq), F32),
                        pltpu.VMEM((g, 2, sub * kb, tq), BF16),
                        pltpu.VMEM((g, 2, 1, tq), F32),
                        pltpu.VMEM((g, MLA_V, tq), F32),
                        pltpu.VMEM((g, 1, tq), F32),
                        pltpu.VMEM((g, 1, tq), F32)],
        compiler_params=_cparams(3),
        name="mla_attention",
    )(qt, k, vt)


def _ffn_chunks(hidden):
    step = 512
    return tuple((lo, min(lo + step, hidden)) for lo in range(0, hidden, step))


def _merge_ffn_kernel(x_ref, mod_ref, gates_ref, ys_ref, yw_ref, ym_ref, wps_ref, wpw_ref, wpm_ref,
                      wo_ref, g2_ref, wfi_ref, wfo_ref, o_ref):
    d = x_ref.shape[-1]
    hidden = wfo_ref.shape[0]
    mod = mod_ref[0, 0]
    gt1, sh2, sc2, gt2 = mod[2:3], mod[3:4], mod[4:5], mod[5:6]
    gates = jax.nn.sigmoid(gates_ref[0])
    merged = (gates[:, 0:d] * jnp.dot(ys_ref[0], wps_ref[...], preferred_element_type=F32)
              + gates[:, d:2 * d] * jnp.dot(yw_ref[0], wpw_ref[...], preferred_element_type=F32)
              + gates[:, 2 * d:3 * d] * jnp.dot(ym_ref[0], wpm_ref[...], preferred_element_type=F32))
    x1 = x_ref[0] + gt1 * jnp.dot(merged.astype(BF16), wo_ref[...], preferred_element_type=F32)
    hb = (_rms(x1, g2_ref[...]) * (1.0 + sc2) + sh2).astype(BF16)
    acc = jnp.zeros_like(x1)
    for lo, hi in _ffn_chunks(hidden):
        gate = jnp.dot(hb, wfi_ref[:, lo:hi], preferred_element_type=F32)
        up = jnp.dot(hb, wfi_ref[:, hidden + lo:hidden + hi], preferred_element_type=F32)
        acc = acc + jnp.dot((_silu(gate) * up).astype(BF16), wfo_ref[lo:hi, :], preferred_element_type=F32)
    o_ref[0] = x1 + gt2 * acc


def _merge_ffn(x_all, modsel, gates, y_ssm, y_swa, y_mla, wps, wpw, wpm, wo, norm2_g, wfi, wfo,
               n_ctx_blocks, first_block):
    bsz, t, d = x_all.shape
    tm = TOKEN_BLOCK
    nblk = t // tm - first_block

    def tok(width):
        return pl.BlockSpec((1, tm, width), lambda b, i: (b, i + first_block, 0))

    return pl.pallas_call(
        _merge_ffn_kernel,
        grid=(bsz, nblk),
        in_specs=[tok(d),
                  pl.BlockSpec((1, 1, SUBLANE, d),
                               lambda b, i: (b, jnp.where(i + first_block < n_ctx_blocks, 0, 1), 0, 0)),
                  tok(N_BRANCH * d), tok(y_ssm.shape[-1]), tok(y_swa.shape[-1]), tok(y_mla.shape[-1]),
                  _resident(wps.shape), _resident(wpw.shape), _resident(wpm.shape), _resident(wo.shape),
                  _resident((1, d)), _resident(wfi.shape), _resident(wfo.shape)],
        out_specs=pl.BlockSpec((1, tm, d), lambda b, i: (b, i, 0)),
        out_shape=jax.ShapeDtypeStruct((bsz, nblk * tm, d), F32),
        compiler_params=_cparams(2),
        name="merge_ffn",
    )(x_all, modsel, gates, y_ssm, y_swa, y_mla, wps, wpw, wpm, wo, norm2_g.reshape(1, d), wfi, wfo)


def _head_major(w, n_heads, parts):
    k = w.shape[0]
    w = w.reshape(k, n_heads, sum(parts))
    out, lo = [], 0
    for p in parts:
        out.append(w[:, :, lo:lo + p].reshape(k, n_heads * p))
        lo += p
    return jnp.concatenate(out, axis=1)


def kernel(x, c, ctx, c_ctx, w_mod, b_mod, norm1_g, norm2_g, w_in, ssm_conv_w, ssm_conv_b, ssm_dt_bias,
           ssm_a_log, ssm_d, ssm_norm_g, swa_q_norm_g, swa_k_norm_g, swa_sink, mla_q_lat_g, mla_kv_lat_g,
           w_mla_uq, w_mla_ukv, mla_q_norm_g, mla_k_norm_g, w_p_ssm, w_p_swa, w_p_mla, w_out, w_ffn_in,
           w_ffn_out):
    bsz, n_lat, d = x.shape
    n_ctx = ctx.shape[1]
    depth = w_mod.shape[0]
    assert n_ctx % TOKEN_BLOCK == 0 and n_lat % TOKEN_BLOCK == 0 and n_lat % GRID_W == 0
    assert bsz + 1 <= SUBLANE
    n_ctx_blocks = n_ctx // TOKEN_BLOCK

    cvec = jnp.concatenate([c, c_ctx[None], jnp.zeros((SUBLANE - bsz - 1, d), F32)], axis=0)
    mods = _modulation(cvec, w_mod, b_mod).reshape(depth, SUBLANE, 6, d)
    cos_swa, sin_swa = _rope_tables(n_lat, n_ctx, SWA_HEAD_DIM)
    cos_mla, sin_mla = _rope_tables(n_lat, n_ctx, MLA_ROPE)

    x_all = jnp.concatenate([ctx, x], axis=1)
    for i in range(depth):
        last = i == depth - 1
        ctx_mod = jnp.broadcast_to(mods[i, bsz][None], (bsz, 6, d))
        modsel = jnp.pad(jnp.stack([ctx_mod, mods[i, :bsz]], axis=1), ((0, 0), (0, 0), (0, SUBLANE - 6), (0, 0)))

        xbc, small, swa_kv, ckv, z, swa_q, cq, gates = _in_projection(
            x_all, modsel, norm1_g[i], _permute_in_weights(w_in[i], d), n_ctx_blocks)

        y_ssm = _ssd_branch(xbc, small, z, ssm_conv_w[i], ssm_conv_b[i], ssm_dt_bias[i], ssm_a_log[i],
                            ssm_d[i], ssm_norm_g[i], n_ctx)

        first_swa = n_ctx // SWA_BLOCK if last else 0
        qs, ks, vs = _swa_prep(swa_q, swa_kv, swa_q_norm_g[i], swa_k_norm_g[i], cos_swa, sin_swa)
        y_swa = _swa_attention(qs, ks, vs, swa_sink[i], n_ctx, first_swa)

        first_tok = n_ctx_blocks if last else 0
        qm, km, vm = _mla_prep(cq, ckv, small, w_mla_uq[i], w_mla_ukv[i], mla_q_lat_g[i], mla_kv_lat_g[i],
                               mla_q_norm_g[i], mla_k_norm_g[i], cos_mla, sin_mla)
        y_mla = _mla_attention(qm, km, vm, n_ctx, first_tok)

        x_all = _merge_ffn(x_all, modsel, gates, y_ssm, y_swa, y_mla,
                           w_p_ssm[i].astype(BF16), w_p_swa[i].astype(BF16), w_p_mla[i].astype(BF16),
                           w_out[i].astype(BF16), norm2_g[i], w_ffn_in[i].astype(BF16),
                           w_ffn_out[i].astype(BF16), n_ctx_blocks, first_tok)
    return x_all
```

```python
import functools
import math

import jax
import jax.numpy as jnp
from jax import lax
from jax.experimental import pallas as pl
from jax.experimental.pallas import tpu as pltpu

F32 = jnp.float32
BF16 = jnp.bfloat16
HIGHEST = lax.Precision.HIGHEST

EPS = 1e-6
ROPE_BASE = 10000.0
GRID_W = 64

SSM_HEADS = 16
SSM_HEAD_DIM = 64
SSM_INNER = SSM_HEADS * SSM_HEAD_DIM
SSM_GROUPS = 2
SSM_STATE = 128
SSM_CONV = 5
SSM_CHUNK = 128
SSM_BC = SSM_GROUPS * SSM_STATE
SSM_CONV_DIM = SSM_INNER + 2 * SSM_BC
SSM_HPG = SSM_HEADS // SSM_GROUPS

SWA_Q_HEADS = 8
SWA_KV_HEADS = 2
SWA_HEAD_DIM = 128
SWA_WINDOW = 128
SWA_BLOCK = 128
SWA_GRP = SWA_Q_HEADS // SWA_KV_HEADS

MLA_HEADS = 8
MLA_Q_RANK = 384
MLA_KV_RANK = 256
MLA_NOPE = 128
MLA_ROPE = 64
MLA_QK = MLA_NOPE + MLA_ROPE
MLA_V = 128
BF16_SUBLANES = 16
MLA_VT_ROWS = MLA_V + BF16_SUBLANES

N_BRANCH = 3
LANE = 128
SUBLANE = 8
HALO = SUBLANE
TOKEN_BLOCK = 256
VMEM_LIMIT = 56 * 1024 * 1024


def _cparams(n_axes):
    return pltpu.CompilerParams(
        dimension_semantics=("arbitrary",) * n_axes, vmem_limit_bytes=VMEM_LIMIT)


def _resident(shape):
    nd = len(shape)
    return pl.BlockSpec(shape, lambda *_: (0,) * nd, pipeline_mode=pl.Buffered(1))


def _rms(x, g):
    return x * lax.rsqrt(jnp.mean(x * x, axis=-1, keepdims=True) + EPS) * g


def _silu(x):
    return x * jax.nn.sigmoid(x)


def _bdot(a, b):
    return jnp.dot(a.astype(BF16), b.astype(BF16), preferred_element_type=F32)


def _bdot_nt(a, b):
    return lax.dot_general(a.astype(BF16), b.astype(BF16), (((1,), (1,)), ((), ())),
                           preferred_element_type=F32)


def _xdot(a, b):
    return jnp.dot(a, b, precision=HIGHEST, preferred_element_type=F32)


def _mod_kernel(c_ref, w_ref, b_ref, o_ref):
    o_ref[0] = _xdot(_silu(c_ref[...]), w_ref[0]) + b_ref[0]


def _modulation(cvec, w_mod, b_mod):
    depth, d, d6 = w_mod.shape
    rows = cvec.shape[0]
    return pl.pallas_call(
        _mod_kernel,
        grid=(depth, d6 // d),
        in_specs=[pl.BlockSpec((rows, d), lambda i, j: (0, 0)),
                  pl.BlockSpec((1, d, d), lambda i, j: (i, 0, j)),
                  pl.BlockSpec((1, 1, d), lambda i, j: (i, 0, j))],
        out_specs=pl.BlockSpec((1, rows, d), lambda i, j: (i, 0, j)),
        out_shape=jax.ShapeDtypeStruct((depth, rows, d6), F32),
        compiler_params=_cparams(2),
        name="modulation",
    )(cvec, w_mod, b_mod.reshape(depth, 1, d6))


IN_GROUPS = (
    ("xbc", SSM_CONV_DIM, F32),
    ("small", LANE, F32),
    ("swa_kv", 2 * SWA_KV_HEADS * SWA_HEAD_DIM, F32),
    ("ckv", MLA_KV_RANK, F32),
    ("z", SSM_INNER, F32),
    ("swa_q", SWA_Q_HEADS * SWA_HEAD_DIM, F32),
    ("cq", MLA_Q_RANK, F32),
    ("gates", None, F32),
)


def _inproj_kernel(x_ref, mod_ref, g_ref, w_ref, *out_refs, bounds):
    x = x_ref[0]
    mod = mod_ref[0, 0]
    h = _rms(x, g_ref[...]) * (1.0 + mod[1:2]) + mod[0:1]
    hb = h.astype(BF16)
    for o_ref, (lo, hi) in zip(out_refs, bounds):
        o_ref[0] = jnp.dot(hb, w_ref[:, lo:hi], preferred_element_type=F32).astype(o_ref.dtype)


def _in_projection(x_all, modsel, norm_g, w_perm, n_ctx_blocks):
    bsz, t, d = x_all.shape
    tm = TOKEN_BLOCK
    widths = [w if w is not None else N_BRANCH * d for _, w, _ in IN_GROUPS]
    offs = [0]
    for w in widths:
        offs.append(offs[-1] + w)
    bounds = tuple((offs[i], offs[i + 1]) for i in range(len(widths)))
    assert w_perm.shape == (d, offs[-1])
    out_shape = [jax.ShapeDtypeStruct((bsz, t, w), dt) for w, (_, _, dt) in zip(widths, IN_GROUPS)]
    out_specs = [pl.BlockSpec((1, tm, w), lambda b, i: (b, i, 0)) for w in widths]
    return pl.pallas_call(
        functools.partial(_inproj_kernel, bounds=bounds),
        grid=(bsz, t // tm),
        in_specs=[pl.BlockSpec((1, tm, d), lambda b, i: (b, i, 0)),
                  pl.BlockSpec((1, 1, SUBLANE, d),
                               lambda b, i: (b, jnp.where(i < n_ctx_blocks, 0, 1), 0, 0)),
                  _resident((1, d)),
                  _resident(w_perm.shape)],
        out_specs=out_specs,
        out_shape=out_shape,
        compiler_params=_cparams(2),
        name="in_projection",
    )(x_all, modsel, norm_g.reshape(1, d), w_perm)


def _permute_in_weights(w_in_l, d):
    o = [0]
    for w in (SSM_CONV_DIM, 2 * SSM_HEADS, SWA_KV_HEADS * SWA_HEAD_DIM, SWA_KV_HEADS * SWA_HEAD_DIM,
              MLA_KV_RANK, MLA_ROPE, SSM_INNER, SWA_Q_HEADS * SWA_HEAD_DIM, MLA_Q_RANK, N_BRANCH * d):
        o.append(o[-1] + w)
    xbc, dt, k, v, ckv, kr, z, q, cq, gates = (w_in_l[:, o[i]:o[i + 1]] for i in range(10))
    pad = jnp.zeros((d, LANE - 2 * SSM_HEADS - MLA_ROPE), w_in_l.dtype)
    return jnp.concatenate([xbc, dt, kr, pad, k, v, ckv, z, q, cq, gates], axis=1).astype(BF16)


def _ssd_scalars(small, dtb_ref, alog_ref):
    q = small.shape[0]
    lane = lax.broadcasted_iota(jnp.int32, (1, LANE), 1)
    raw = small + dtb_ref[...]
    dts = jnp.maximum(raw, 0.0) + jnp.log1p(jnp.exp(-jnp.abs(raw)))
    a = jnp.where(lane < 2 * SSM_HEADS, -jnp.exp(alog_ref[...]), 0.0)
    dta = dts * a
    ri = lax.broadcasted_iota(jnp.int32, (q, q), 0)
    ci = lax.broadcasted_iota(jnp.int32, (q, q), 1)
    tri = (ci <= ri).astype(F32)
    acs = _xdot(tri, dta)
    ecs = acs - dta
    return dts, acs, ecs, dts.T, acs.T, ecs.T


def _expand_matrix(first_row):
    r = lax.broadcasted_iota(jnp.int32, (LANE, SSM_INNER), 0)
    c = lax.broadcasted_iota(jnp.int32, (LANE, SSM_INNER), 1)
    return (c // SSM_HEAD_DIM + first_row == r).astype(F32)


def _ssd_direction(backward, xs, bm, cm, scal, state_ref):
    dts, acs, ecs, dts_t, acs_t, ecs_t = scal
    q = xs.shape[0]
    base = SSM_HEADS if backward else 0
    expand = _expand_matrix(base)
    tot = acs[q - 1:q, :]
    if backward:
        dec_in = jnp.exp(tot - ecs)
        w_out = jnp.exp(ecs) * dts
        pos, pos_t = ecs, ecs_t
    else:
        dec_in = jnp.exp(acs)
        w_out = jnp.exp(tot - acs) * dts
        pos, pos_t = acs, acs_t
    dec_e = _xdot(dec_in, expand)
    w_e = _xdot(w_out, expand)
    tot_e = _xdot(jnp.broadcast_to(jnp.exp(tot), (SUBLANE, LANE)), expand)[0:1]
    xw = (xs * w_e).astype(BF16)
    xb = xs.astype(BF16)
    ri = lax.broadcasted_iota(jnp.int32, (q, q), 0)
    ci = lax.broadcasted_iota(jnp.int32, (q, q), 1)
    keep = (ci >= ri) if backward else (ci <= ri)
    gw = SSM_HPG * SSM_HEAD_DIM
    ys = []
    for g in range(SSM_GROUPS):
        b_g = bm[:, g * SSM_STATE:(g + 1) * SSM_STATE]
        c_g = cm[:, g * SSM_STATE:(g + 1) * SSM_STATE].astype(BF16)
        b_t = b_g.T.astype(BF16)
        cb = jnp.dot(c_g, b_t, preferred_element_type=F32)
        st = state_ref[g]
        y_off = jnp.dot(c_g, st.astype(BF16), preferred_element_type=F32) * dec_e[:, g * gw:(g + 1) * gw]
        state_ref[g] = st * tot_e[:, g * gw:(g + 1) * gw] + jnp.dot(
            b_t, xw[:, g * gw:(g + 1) * gw], preferred_element_type=F32)
        heads = []
        for hh in range(SSM_HPG):
            h = g * SSM_HPG + hh
            col = pos[:, base + h:base + h + 1]
            row = pos_t[base + h:base + h + 1, :]
            diff = (row - col) if backward else (col - row)
            seg = jnp.exp(jnp.where(keep, diff, -jnp.inf))
            m = (cb * seg * dts_t[base + h:base + h + 1, :]).astype(BF16)
            heads.append(jnp.dot(m, xb[:, h * SSM_HEAD_DIM:(h + 1) * SSM_HEAD_DIM],
                                 preferred_element_type=F32))
        ys.append(jnp.concatenate(heads, axis=1) + y_off)
    return jnp.concatenate(ys, axis=1)


def _ssd_fwd_kernel(xc_ref, xp_ref, xn_ref, small_ref, cw_ref, cb_ref, dtb_ref, alog_ref, dskip_ref,
                    y_ref, u_ref, state_ref, *, n_ctx_chunks, n_chunks):
    c = pl.program_id(1)

    @pl.when(c == 0)
    def _():
        state_ref[...] = jnp.zeros_like(state_ref)

    prev_ok = jnp.logical_and(c != 0, c != n_ctx_chunks)
    next_ok = jnp.logical_and(c != n_ctx_chunks - 1, c != n_chunks - 1)
    xp = jnp.where(prev_ok, xp_ref[0], 0.0)
    xn = jnp.where(next_ok, xn_ref[0], 0.0)
    xc = xc_ref[0]
    q = xc.shape[0]
    ext = jnp.concatenate([xp, xc, xn], axis=0)
    half = SSM_CONV // 2
    acc = jnp.zeros_like(xc) + cb_ref[...]
    for k in range(SSM_CONV):
        lo = HALO - half + k
        acc = acc + ext[lo:lo + q, :] * cw_ref[k:k + 1, :]
    u = _silu(acc)
    u_ref[0] = u.astype(u_ref.dtype)
    xs = u[:, :SSM_INNER]
    bm = u[:, SSM_INNER:SSM_INNER + SSM_BC]
    cm = u[:, SSM_INNER + SSM_BC:]
    scal = _ssd_scalars(small_ref[0], dtb_ref, alog_ref)
    y = _ssd_direction(False, xs, bm, cm, scal, state_ref)
    y_ref[0] = y + dskip_ref[...] * xs


def _ssd_bwd_kernel(u_ref, small_ref, z_ref, yf_ref, dtb_ref, alog_ref, ng_ref, o_ref, state_ref):
    @pl.when(pl.program_id(1) == 0)
    def _():
        state_ref[...] = jnp.zeros_like(state_ref)

    u = u_ref[0].astype(F32)
    xs = u[:, :SSM_INNER]
    bm = u[:, SSM_INNER:SSM_INNER + SSM_BC]
    cm = u[:, SSM_INNER + SSM_BC:]
    scal = _ssd_scalars(small_ref[0], dtb_ref, alog_ref)
    y = yf_ref[0] + _ssd_direction(True, xs, bm, cm, scal, state_ref)
    o_ref[0] = _rms(y * _silu(z_ref[0]), ng_ref[...]).astype(o_ref.dtype)


def _ssd_branch(xbc, small, z, conv_w, conv_b, dt_bias, a_log, d_skip, norm_g, n_ctx):
    bsz, t, _ = xbc.shape
    q = SSM_CHUNK
    n_chunks = t // q
    n_ctx_chunks = n_ctx // q
    hb = q // HALO
    n_halo = t // HALO
    pad32 = LANE - 2 * SSM_HEADS
    dtb = jnp.pad(dt_bias.reshape(1, -1), ((0, 0), (0, pad32)))
    alog = jnp.pad(a_log.reshape(1, -1), ((0, 0), (0, pad32)))
    dskip = jnp.repeat(d_skip, SSM_HEAD_DIM).reshape(1, SSM_INNER)
    state = pltpu.VMEM((SSM_GROUPS, SSM_STATE, SSM_HPG * SSM_HEAD_DIM), F32)

    def chunk(width):
        return pl.BlockSpec((1, q, width), lambda b, c: (b, c, 0))

    y_f, u = pl.pallas_call(
        functools.partial(_ssd_fwd_kernel, n_ctx_chunks=n_ctx_chunks, n_chunks=n_chunks),
        grid=(bsz, n_chunks),
        in_specs=[chunk(SSM_CONV_DIM),
                  pl.BlockSpec((1, HALO, SSM_CONV_DIM), lambda b, c: (b, jnp.maximum(c * hb - 1, 0), 0)),
                  pl.BlockSpec((1, HALO, SSM_CONV_DIM),
                               lambda b, c: (b, jnp.minimum((c + 1) * hb, n_halo - 1), 0)),
                  chunk(LANE),
                  _resident((SSM_CONV, SSM_CONV_DIM)), _resident((1, SSM_CONV_DIM)),
                  _resident((1, LANE)), _resident((1, LANE)), _resident((1, SSM_INNER))],
        out_specs=[chunk(SSM_INNER), chunk(SSM_CONV_DIM)],
        out_shape=[jax.ShapeDtypeStruct((bsz, t, SSM_INNER), F32),
                   jax.ShapeDtypeStruct((bsz, t, SSM_CONV_DIM), BF16)],
        scratch_shapes=[state],
        compiler_params=_cparams(2),
        name="ssd_forward",
    )(xbc, xbc, xbc, small, conv_w, conv_b.reshape(1, -1), dtb, alog, dskip)

    def rchunk(width):
        return pl.BlockSpec(
            (1, q, width),
            lambda b, s: (b, jnp.where(s < n_ctx_chunks, n_ctx_chunks - 1 - s,
                                       n_chunks + n_ctx_chunks - 1 - s), 0))

    return pl.pallas_call(
        _ssd_bwd_kernel,
        grid=(bsz, n_chunks),
        in_specs=[rchunk(SSM_CONV_DIM), rchunk(LANE), rchunk(SSM_INNER), rchunk(SSM_INNER),
                  _resident((1, LANE)), _resident((1, LANE)), _resident((1, SSM_INNER))],
        out_specs=rchunk(SSM_INNER),
        out_shape=jax.ShapeDtypeStruct((bsz, t, SSM_INNER), BF16),
        scratch_shapes=[state],
        compiler_params=_cparams(2),
        name="ssd_backward",
    )(u, small, z, y_f, dtb, alog, norm_g.reshape(1, -1))


def _rope_tables(n_lat, n_ctx, rot_dim):
    n_freq = rot_dim // 4
    inv = jnp.power(ROPE_BASE, -jnp.arange(n_freq, dtype=F32) / n_freq)
    t = jnp.arange(n_lat)
    r = (t // GRID_W).astype(F32)[:, None] * inv
    col = (t % GRID_W).astype(F32)[:, None] * inv
    cos2 = jnp.concatenate([jnp.cos(r), jnp.cos(r), jnp.cos(col), jnp.cos(col)], axis=1)
    sin2 = jnp.concatenate([-jnp.sin(r), jnp.sin(r), -jnp.sin(col), jnp.sin(col)], axis=1)
    cos2 = jnp.concatenate([jnp.ones((n_ctx, rot_dim), F32), cos2], axis=0)
    sin2 = jnp.concatenate([jnp.zeros((n_ctx, rot_dim), F32), sin2], axis=0)
    return cos2, sin2


def _rotate_half(x):
    f = x.shape[-1] // 4
    return jnp.concatenate([x[:, f:2 * f], x[:, :f], x[:, 3 * f:], x[:, 2 * f:3 * f]], axis=1)


def _swa_prep_kernel(q_ref, kv_ref, qg_ref, kg_ref, cos_ref, sin_ref, qo_ref, ko_ref, vo_ref):
    cos, sin = cos_ref[...], sin_ref[...]
    scale = SWA_HEAD_DIM ** -0.5
    qf = q_ref[0]
    kvf = kv_ref[0]
    dh = SWA_HEAD_DIM
    for h in range(SWA_Q_HEADS):
        x = _rms(qf[:, h * dh:(h + 1) * dh], qg_ref[...])
        x = x * cos + _rotate_half(x) * sin
        qo_ref[0, :, h * dh:(h + 1) * dh] = (x * scale).astype(qo_ref.dtype)
    for h in range(SWA_KV_HEADS):
        x = _rms(kvf[:, h * dh:(h + 1) * dh], kg_ref[...])
        x = x * cos + _rotate_half(x) * sin
        ko_ref[0, :, h * dh:(h + 1) * dh] = x.astype(ko_ref.dtype)
    vo_ref[0] = kvf[:, SWA_KV_HEADS * dh:].astype(vo_ref.dtype)


def _swa_prep(q, kv, q_g, k_g, cos, sin):
    bsz, t, _ = q.shape
    tm = TOKEN_BLOCK
    dq = SWA_Q_HEADS * SWA_HEAD_DIM
    dkv = SWA_KV_HEADS * SWA_HEAD_DIM
    return pl.pallas_call(
        _swa_prep_kernel,
        grid=(bsz, t // tm),
        in_specs=[pl.BlockSpec((1, tm, dq), lambda b, i: (b, i, 0)),
                  pl.BlockSpec((1, tm, 2 * dkv), lambda b, i: (b, i, 0)),
                  _resident((1, SWA_HEAD_DIM)), _resident((1, SWA_HEAD_DIM)),
                  pl.BlockSpec((tm, SWA_HEAD_DIM), lambda b, i: (i, 0)),
                  pl.BlockSpec((tm, SWA_HEAD_DIM), lambda b, i: (i, 0))],
        out_specs=[pl.BlockSpec((1, tm, dq), lambda b, i: (b, i, 0)),
                   pl.BlockSpec((1, tm, dkv), lambda b, i: (b, i, 0)),
                   pl.BlockSpec((1, tm, dkv), lambda b, i: (b, i, 0))],
        out_shape=[jax.ShapeDtypeStruct((bsz, t, dq), BF16),
                   jax.ShapeDtypeStruct((bsz, t, dkv), BF16),
                   jax.ShapeDtypeStruct((bsz, t, dkv), BF16)],
        compiler_params=_cparams(2),
        name="swa_prep",
    )(q, kv, q_g.reshape(1, -1), k_g.reshape(1, -1), cos, sin)


def _swa_kernel(sink_ref, q_ref, k_ref, v_ref, o_ref, *, first_block, n_ctx, t):
    hk = pl.program_id(1)
    blk = pl.program_id(2) + first_block
    n_ctx_blocks = n_ctx // SWA_BLOCK
    bq = SWA_BLOCK
    win = 3 * SWA_BLOCK
    dh = SWA_HEAD_DIM
    qb = q_ref[0]
    q4 = jnp.concatenate([qb[:, g * dh:(g + 1) * dh] for g in range(SWA_GRP)], axis=0)
    start = jnp.clip((blk - 1) * bq, 0, t - win)
    start = pl.multiple_of(start, bq)
    kw = k_ref[0, pl.ds(start, win), :]
    vw = v_ref[0, pl.ds(start, win), :]
    kc = k_ref[0, 0:n_ctx, :]
    vc = v_ref[0, 0:n_ctx, :]
    s_w = _bdot_nt(q4, kw)
    s_c = _bdot_nt(q4, kc)
    rows = lax.broadcasted_iota(jnp.int32, (SWA_GRP * bq, win), 0)
    cols = lax.broadcasted_iota(jnp.int32, (SWA_GRP * bq, win), 1)
    qpos = (blk - n_ctx_blocks) * bq + (rows & (bq - 1))
    kpos = start - n_ctx + cols
    ok = (jnp.abs(kpos - qpos) <= SWA_WINDOW) & (kpos >= 0) & (blk >= n_ctx_blocks)
    s_w = jnp.where(ok, s_w, -jnp.inf)
    r1 = lax.broadcasted_iota(jnp.int32, (SWA_GRP * bq, 1), 0)
    sink = jnp.zeros((SWA_GRP * bq, 1), F32)
    for g in range(SWA_GRP):
        sink = jnp.where(r1 // bq == g, sink_ref[hk * SWA_GRP + g], sink)
    m = jnp.maximum(jnp.maximum(jnp.max(s_w, axis=-1, keepdims=True),
                                jnp.max(s_c, axis=-1, keepdims=True)), sink)
    p_w = jnp.exp(s_w - m)
    p_c = jnp.exp(s_c - m)
    den = (jnp.sum(p_w, axis=-1, keepdims=True) + jnp.sum(p_c, axis=-1, keepdims=True)
           + jnp.exp(sink - m))
    o = (_bdot(p_w, vw) + _bdot(p_c, vc)) / den
    for g in range(SWA_GRP):
        o_ref[0, :, g * dh:(g + 1) * dh] = o[g * bq:(g + 1) * bq, :].astype(o_ref.dtype)


def _swa_attention(q, k, v, sink, n_ctx, first_block):
    bsz, t, dq = q.shape
    bq = SWA_BLOCK
    gw = SWA_GRP * SWA_HEAD_DIM
    nblk = t // bq - first_block
    return pl.pallas_call(
        functools.partial(_swa_kernel, first_block=first_block, n_ctx=n_ctx, t=t),
        grid=(bsz, SWA_KV_HEADS, nblk),
        in_specs=[pl.BlockSpec(memory_space=pltpu.SMEM),
                  pl.BlockSpec((1, bq, gw), lambda b, h, n: (b, n + first_block, h)),
                  pl.BlockSpec((1, t, SWA_HEAD_DIM), lambda b, h, n: (b, 0, h)),
                  pl.BlockSpec((1, t, SWA_HEAD_DIM), lambda b, h, n: (b, 0, h))],
        out_specs=pl.BlockSpec((1, bq, gw), lambda b, h, n: (b, n + first_block, h)),
        out_shape=jax.ShapeDtypeStruct((bsz, t, dq), BF16),
        compiler_params=_cparams(3),
        name="swa_attention",
    )(sink, q, k, v)


def _rms_cols(x, g):
    return x * lax.rsqrt(jnp.mean(x * x, axis=0, keepdims=True) + EPS) * g


def _rotate_half_rows(x):
    f = x.shape[0] // 4
    return jnp.concatenate([x[f:2 * f], x[:f], x[3 * f:], x[2 * f:3 * f]], axis=0)


def _mla_prep_kernel(cq_ref, ckv_ref, small_ref, wqt_ref, wk_ref, wvt_ref, qlg_ref, kvlg_ref,
                     qgn_ref, qgr_ref, kgn_ref, kgr_ref, cos_ref, sin_ref, cost_ref, sint_ref,
                     qo_ref, ko_ref, vo_ref):
    qscale = MLA_QK ** -0.5 * math.log2(math.e)
    nh, dn, dr, dv = MLA_HEADS, MLA_NOPE, MLA_ROPE, MLA_V

    cqn_t = _rms(cq_ref[0], qlg_ref[...]).T.astype(BF16)
    qf_t = jnp.dot(wqt_ref[...], cqn_t, preferred_element_type=F32)
    cos_t, sin_t = cost_ref[...], sint_ref[...]
    for h in range(nh):
        qn = _rms_cols(qf_t[h * dn:(h + 1) * dn], qgn_ref[...])
        qr = _rms_cols(qf_t[nh * dn + h * dr:nh * dn + (h + 1) * dr], qgr_ref[...])
        qr = qr * cos_t + _rotate_half_rows(qr) * sin_t
        qo_ref[0, h, 0:dn, :] = (qn * qscale).astype(qo_ref.dtype)
        qo_ref[0, h, dn:dn + dr, :] = (qr * qscale).astype(qo_ref.dtype)

    ckvn = _rms(ckv_ref[0], kvlg_ref[...])
    kf = jnp.dot(ckvn.astype(BF16), wk_ref[...], preferred_element_type=F32)
    kr0 = 2 * SSM_HEADS
    kr = _rms(small_ref[0][:, kr0:kr0 + dr], kgr_ref[...])
    kr = kr * cos_ref[...] + _rotate_half(kr) * sin_ref[...]
    for h in range(nh):
        kn = _rms(kf[:, h * dn:(h + 1) * dn], kgn_ref[...])
        ko_ref[0, h] = jnp.concatenate([kn, kr], axis=1).astype(ko_ref.dtype)
    vf_t = jnp.dot(wvt_ref[...], ckvn.T.astype(BF16), preferred_element_type=F32)
    for h in range(nh):
        vo_ref[0, h, 0, 0:dv, :] = vf_t[h * dv:(h + 1) * dv].astype(vo_ref.dtype)
        vo_ref[0, h, 0, dv:, :] = jnp.ones((MLA_VT_ROWS - dv, vo_ref.shape[-1]), vo_ref.dtype)


def _mla_prep(cq, ckv, small, w_uq, w_ukv, q_lat_g, kv_lat_g, q_g, k_g, cos, sin):
    bsz, t, _ = cq.shape
    tm = TOKEN_BLOCK
    nh = MLA_HEADS
    wq_t = _head_major(w_uq, nh, (MLA_NOPE, MLA_ROPE)).T.astype(BF16)
    wkv = _head_major(w_ukv, nh, (MLA_NOPE, MLA_V))
    wk = wkv[:, :nh * MLA_NOPE].astype(BF16)
    wv_t = wkv[:, nh * MLA_NOPE:].T.astype(BF16)

    def tok(width):
        return pl.BlockSpec((1, tm, width), lambda b, i: (b, i, 0))

    def cols(g):
        return jnp.broadcast_to(g[:, None], (g.shape[0], tm))

    return pl.pallas_call(
        _mla_prep_kernel,
        grid=(bsz, t // tm),
        in_specs=[tok(MLA_Q_RANK), tok(MLA_KV_RANK), tok(LANE),
                  _resident(wq_t.shape), _resident(wk.shape), _resident(wv_t.shape),
                  _resident((1, MLA_Q_RANK)), _resident((1, MLA_KV_RANK)),
                  _resident((MLA_NOPE, tm)), _resident((MLA_ROPE, tm)),
                  _resident((1, MLA_NOPE)), _resident((1, MLA_ROPE)),
                  pl.BlockSpec((tm, MLA_ROPE), lambda b, i: (i, 0)),
                  pl.BlockSpec((tm, MLA_ROPE), lambda b, i: (i, 0)),
                  pl.BlockSpec((MLA_ROPE, tm), lambda b, i: (0, i)),
                  pl.BlockSpec((MLA_ROPE, tm), lambda b, i: (0, i))],
        out_specs=[pl.BlockSpec((1, nh, MLA_QK, tm), lambda b, i: (b, 0, 0, i)),
                   pl.BlockSpec((1, nh, tm, MLA_QK), lambda b, i: (b, 0, i, 0)),
                   pl.BlockSpec((1, nh, 1, MLA_VT_ROWS, tm), lambda b, i: (b, 0, i, 0, 0))],
        out_shape=[jax.ShapeDtypeStruct((bsz, nh, MLA_QK, t), BF16),
                   jax.ShapeDtypeStruct((bsz, nh, t, MLA_QK), BF16),
                   jax.ShapeDtypeStruct((bsz, nh, t // tm, MLA_VT_ROWS, tm), BF16)],
        compiler_params=_cparams(2),
        name="mla_prep",
    )(cq, ckv, small, wq_t, wk, wv_t, q_lat_g.reshape(1, -1), kv_lat_g.reshape(1, -1),
      cols(q_g[:MLA_NOPE]), cols(q_g[MLA_NOPE:]),
      k_g[:MLA_NOPE].reshape(1, -1), k_g[MLA_NOPE:].reshape(1, -1), cos, sin, cos.T, sin.T)


MLA_HEADS_PER_STEP = 2
MLA_KEY_SUBBLOCKS = 3
MLA_UNROLL_CHUNKS = 16


def _mla_kernel(qt_ref, k_ref, vt_ref, o_ref, s_scr, p_scr, a_scr, mx_scr, acc_scr, m_scr, *,
                first_block, n_ctx_blocks, n_key_blocks, sub):
    blk = pl.program_id(2) + first_block
    heads = qt_ref.shape[1]
    kb = vt_ref.shape[4]
    tk = sub * kb
    n_chunks = n_key_blocks // sub

    def softmax_update(c, s, s_max):
        m_prev = m_scr[c]
        m_new = jnp.maximum(m_prev, s_max)
        m_scr[c] = m_new
        return jnp.exp2(m_prev - m_new), jnp.exp2(s - m_new).astype(BF16)

    def value_update(c, alpha, pb, blk0, nsub):
        pv = jnp.dot(vt_ref[0, c, blk0], pb[0:kb], preferred_element_type=F32)
        for u in range(1, nsub):
            pv = pv + jnp.dot(vt_ref[0, c, blk0 + u], pb[u * kb:(u + 1) * kb], preferred_element_type=F32)
        acc_scr[c] = alpha * acc_scr[c] + pv

    def stage_scores(c, t, slot):
        row0 = t * tk if isinstance(t, int) else pl.multiple_of(t * tk, tk)
        s = jnp.dot(k_ref[0, c, pl.ds(row0, tk), :], qt_ref[0, c], preferred_element_type=F32)
        s_scr[c, slot] = s
        mx_scr[c, slot] = jnp.max(s, axis=0, keepdims=True)

    def stage_softmax(c, slot):
        alpha, pb = softmax_update(c, s_scr[c, slot], mx_scr[c, slot])
        a_scr[c, slot] = alpha
        p_scr[c, slot] = pb

    def stage_values(c, t, slot):
        value_update(c, a_scr[c, slot], p_scr[c, slot], t * sub, sub)

    def tick(t, parity):
        static = isinstance(t, int)
        for c in range(heads):
            if not static or t < n_chunks:
                stage_scores(c, t, parity)
            if not static or 1 <= t <= n_chunks:
                stage_softmax(c, 1 - parity)
            if not static or 2 <= t <= n_chunks + 1:
                stage_values(c, t - 2, parity)

    def reset():
        m_scr[...] = jnp.full(m_scr.shape, -jnp.inf, F32)
        acc_scr[...] = jnp.zeros(acc_scr.shape, F32)

    def finish():
        for c in range(heads):
            acc = acc_scr[c]
            o = acc[0:MLA_V] / acc[MLA_V:MLA_V + 1]
            o_ref[0, :, c * MLA_V:(c + 1) * MLA_V] = o.T.astype(o_ref.dtype)

    @pl.when(blk < n_ctx_blocks)
    def _():
        reset()
        for c in range(heads):
            s = jnp.dot(k_ref[0, c, 0:n_ctx_blocks * kb, :], qt_ref[0, c], preferred_element_type=F32)
            alpha, pb = softmax_update(c, s, jnp.max(s, axis=0, keepdims=True))
            value_update(c, alpha, pb, 0, n_ctx_blocks)
        finish()

    @pl.when(blk >= n_ctx_blocks)
    def _():
        reset()
        n_pairs = 0 if n_chunks <= MLA_UNROLL_CHUNKS else (n_chunks - 2) // 2
        for t in range(0, min(2, n_chunks + 2)):
            tick(t, t % 2)

        def body(i, carry):
            t0 = 2 + 2 * i
            tick(t0, 0)
            tick(t0 + 1, 1)
            return carry

        lax.fori_loop(0, n_pairs, body, 0)
        for t in range(2 + 2 * n_pairs, n_chunks + 2):
            tick(t, t % 2)
        finish()


def _mla_attention(qt, k, vt, n_ctx, first_block):
    bsz, nh, dqk, t = qt.shape
    tq = TOKEN_BLOCK
    kb = vt.shape[-1]
    n_key_blocks = t // kb
    sub = MLA_KEY_SUBBLOCKS if n_key_blocks % MLA_KEY_SUBBLOCKS == 0 else 1
    g = MLA_HEADS_PER_STEP
    nblk = t // tq - first_block
    return pl.pallas_call(
        functools.partial(_mla_kernel, first_block=first_block, n_ctx_blocks=n_ctx // kb,
                          n_key_blocks=n_key_blocks, sub=sub),
        grid=(bsz, nh // g, nblk),
        in_specs=[pl.BlockSpec((1, g, dqk, tq), lambda b, h, i: (b, h, 0, i + first_block)),
                  pl.BlockSpec((1, g, t, dqk), lambda b, h, i: (b, h, 0, 0)),
                  pl.BlockSpec((1, g, n_key_blocks, MLA_VT_ROWS, kb), lambda b, h, i: (b, h, 0, 0, 0))],
        out_specs=pl.BlockSpec((1, tq, g * MLA_V), lambda b, h, i: (b, i + first_block, h)),
        out_shape=jax.ShapeDtypeStruct((bsz, t, nh * MLA_V), BF16),
        scratch_shapes=[pltpu.VMEM((g, 2, sub * kb, tq), F32),
                        pltpu.VMEM((g, 2, sub * kb, tq), BF16),
                        pltpu.VMEM((g, 2, 1, tq), F32),
                        pltpu.VMEM((g, 2, 1, tq), F32),
                        pltpu.VMEM((g, MLA_VT_ROWS, tq), F32),
                        pltpu.VMEM((g, 1, tq), F32)],
        compiler_params=_cparams(3),
        name="mla_attention",
    )(qt, k, vt)


def _ffn_chunks(hidden):
    step = 512
    return tuple((lo, min(lo + step, hidden)) for lo in range(0, hidden, step))


def _merge_ffn_kernel(x_ref, mod_ref, gates_ref, ys_ref, yw_ref, ym_ref, wps_ref, wpw_ref, wpm_ref,
                      wo_ref, g2_ref, wfi_ref, wfo_ref, o_ref):
    d = x_ref.shape[-1]
    hidden = wfo_ref.shape[0]
    mod = mod_ref[0, 0]
    gt1, sh2, sc2, gt2 = mod[2:3], mod[3:4], mod[4:5], mod[5:6]
    gates = jax.nn.sigmoid(gates_ref[0])
    merged = (gates[:, 0:d] * jnp.dot(ys_ref[0], wps_ref[...], preferred_element_type=F32)
              + gates[:, d:2 * d] * jnp.dot(yw_ref[0], wpw_ref[...], preferred_element_type=F32)
              + gates[:, 2 * d:3 * d] * jnp.dot(ym_ref[0], wpm_ref[...], preferred_element_type=F32))
    x1 = x_ref[0] + gt1 * jnp.dot(merged.astype(BF16), wo_ref[...], preferred_element_type=F32)
    hb = (_rms(x1, g2_ref[...]) * (1.0 + sc2) + sh2).astype(BF16)
    acc = jnp.zeros_like(x1)
    for lo, hi in _ffn_chunks(hidden):
        gate = jnp.dot(hb, wfi_ref[:, lo:hi], preferred_element_type=F32)
        up = jnp.dot(hb, wfi_ref[:, hidden + lo:hidden + hi], preferred_element_type=F32)
        acc = acc + jnp.dot((_silu(gate) * up).astype(BF16), wfo_ref[lo:hi, :], preferred_element_type=F32)
    o_ref[0] = x1 + gt2 * acc


def _merge_ffn(x_all, modsel, gates, y_ssm, y_swa, y_mla, wps, wpw, wpm, wo, norm2_g, wfi, wfo,
               n_ctx_blocks, first_block):
    bsz, t, d = x_all.shape
    tm = TOKEN_BLOCK
    nblk = t // tm - first_block

    def tok(width):
        return pl.BlockSpec((1, tm, width), lambda b, i: (b, i + first_block, 0))

    return pl.pallas_call(
        _merge_ffn_kernel,
        grid=(bsz, nblk),
        in_specs=[tok(d),
                  pl.BlockSpec((1, 1, SUBLANE, d),
                               lambda b, i: (b, jnp.where(i + first_block < n_ctx_blocks, 0, 1), 0, 0)),
                  tok(N_BRANCH * d), tok(y_ssm.shape[-1]), tok(y_swa.shape[-1]), tok(y_mla.shape[-1]),
                  _resident(wps.shape), _resident(wpw.shape), _resident(wpm.shape), _resident(wo.shape),
                  _resident((1, d)), _resident(wfi.shape), _resident(wfo.shape)],
        out_specs=pl.BlockSpec((1, tm, d), lambda b, i: (b, i, 0)),
        out_shape=jax.ShapeDtypeStruct((bsz, nblk * tm, d), F32),
        compiler_params=_cparams(2),
        name="merge_ffn",
    )(x_all, modsel, gates, y_ssm, y_swa, y_mla, wps, wpw, wpm, wo, norm2_g.reshape(1, d), wfi, wfo)


def _head_major(w, n_heads, parts):
    k = w.shape[0]
    w = w.reshape(k, n_heads, sum(parts))
    out, lo = [], 0
    for p in parts:
        out.append(w[:, :, lo:lo + p].reshape(k, n_heads * p))
        lo += p
    return jnp.concatenate(out, axis=1)


def kernel(x, c, ctx, c_ctx, w_mod, b_mod, norm1_g, norm2_g, w_in, ssm_conv_w, ssm_conv_b, ssm_dt_bias,
           ssm_a_log, ssm_d, ssm_norm_g, swa_q_norm_g, swa_k_norm_g, swa_sink, mla_q_lat_g, mla_kv_lat_g,
           w_mla_uq, w_mla_ukv, mla_q_norm_g, mla_k_norm_g, w_p_ssm, w_p_swa, w_p_mla, w_out, w_ffn_in,
           w_ffn_out):
    bsz, n_lat, d = x.shape
    n_ctx = ctx.shape[1]
    depth = w_mod.shape[0]
    assert n_ctx % TOKEN_BLOCK == 0 and n_lat % TOKEN_BLOCK == 0 and n_lat % GRID_W == 0
    assert bsz + 1 <= SUBLANE
    n_ctx_blocks = n_ctx // TOKEN_BLOCK

    cvec = jnp.concatenate([c, c_ctx[None], jnp.zeros((SUBLANE - bsz - 1, d), F32)], axis=0)
    mods = _modulation(cvec, w_mod, b_mod).reshape(depth, SUBLANE, 6, d)
    cos_swa, sin_swa = _rope_tables(n_lat, n_ctx, SWA_HEAD_DIM)
    cos_mla, sin_mla = _rope_tables(n_lat, n_ctx, MLA_ROPE)

    x_all = jnp.concatenate([ctx, x], axis=1)
    for i in range(depth):
        last = i == depth - 1
        ctx_mod = jnp.broadcast_to(mods[i, bsz][None], (bsz, 6, d))
        modsel = jnp.pad(jnp.stack([ctx_mod, mods[i, :bsz]], axis=1), ((0, 0), (0, 0), (0, SUBLANE - 6), (0, 0)))

        xbc, small, swa_kv, ckv, z, swa_q, cq, gates = _in_projection(
            x_all, modsel, norm1_g[i], _permute_in_weights(w_in[i], d), n_ctx_blocks)

        y_ssm = _ssd_branch(xbc, small, z, ssm_conv_w[i], ssm_conv_b[i], ssm_dt_bias[i], ssm_a_log[i],
                            ssm_d[i], ssm_norm_g[i], n_ctx)

        first_swa = n_ctx // SWA_BLOCK if last else 0
        qs, ks, vs = _swa_prep(swa_q, swa_kv, swa_q_norm_g[i], swa_k_norm_g[i], cos_swa, sin_swa)
        y_swa = _swa_attention(qs, ks, vs, swa_sink[i], n_ctx, first_swa)

        first_tok = n_ctx_blocks if last else 0
        qm, km, vm = _mla_prep(cq, ckv, small, w_mla_uq[i], w_mla_ukv[i], mla_q_lat_g[i], mla_kv_lat_g[i],
                               mla_q_norm_g[i], mla_k_norm_g[i], cos_mla, sin_mla)
        y_mla = _mla_attention(qm, km, vm, n_ctx, first_tok)

        x_all = _merge_ffn(x_all, modsel, gates, y_ssm, y_swa, y_mla,
                           w_p_ssm[i].astype(BF16), w_p_swa[i].astype(BF16), w_p_mla[i].astype(BF16),
                           w_out[i].astype(BF16), norm2_g[i], w_ffn_in[i].astype(BF16),
                           w_ffn_out[i].astype(BF16), n_ctx_blocks, first_tok)
    return x_all
```

```python
import functools
import math

import jax
import jax.numpy as jnp
from jax import lax
from jax.experimental import pallas as pl
from jax.experimental.pallas import tpu as pltpu

F32 = jnp.float32
BF16 = jnp.bfloat16
HIGHEST = lax.Precision.HIGHEST

EPS = 1e-6
ROPE_BASE = 10000.0
GRID_W = 64

SSM_HEADS = 16
SSM_HEAD_DIM = 64
SSM_INNER = SSM_HEADS * SSM_HEAD_DIM
SSM_GROUPS = 2
SSM_STATE = 128
SSM_CONV = 5
SSM_CHUNK = 128
SSM_BC = SSM_GROUPS * SSM_STATE
SSM_CONV_DIM = SSM_INNER + 2 * SSM_BC
SSM_HPG = SSM_HEADS // SSM_GROUPS

SWA_Q_HEADS = 8
SWA_KV_HEADS = 2
SWA_HEAD_DIM = 128
SWA_WINDOW = 128
SWA_BLOCK = 128
SWA_GRP = SWA_Q_HEADS // SWA_KV_HEADS

MLA_HEADS = 8
MLA_Q_RANK = 384
MLA_KV_RANK = 256
MLA_NOPE = 128
MLA_ROPE = 64
MLA_QK = MLA_NOPE + MLA_ROPE
MLA_V = 128
BF16_SUBLANES = 16
MLA_VT_ROWS = MLA_V + BF16_SUBLANES

N_BRANCH = 3
LANE = 128
SUBLANE = 8
HALO = SUBLANE
TOKEN_BLOCK = 256
VMEM_LIMIT = 56 * 1024 * 1024


def _cparams(n_axes):
    return pltpu.CompilerParams(
        dimension_semantics=("arbitrary",) * n_axes, vmem_limit_bytes=VMEM_LIMIT)


def _resident(shape):
    nd = len(shape)
    return pl.BlockSpec(shape, lambda *_: (0,) * nd, pipeline_mode=pl.Buffered(1))


def _rms(x, g):
    return x * lax.rsqrt(jnp.mean(x * x, axis=-1, keepdims=True) + EPS) * g


def _silu(x):
    return x * jax.nn.sigmoid(x)


def _bdot(a, b):
    return jnp.dot(a.astype(BF16), b.astype(BF16), preferred_element_type=F32)


def _bdot_nt(a, b):
    return lax.dot_general(a.astype(BF16), b.astype(BF16), (((1,), (1,)), ((), ())),
                           preferred_element_type=F32)


def _xdot(a, b):
    return jnp.dot(a, b, precision=HIGHEST, preferred_element_type=F32)


def _mod_kernel(c_ref, w_ref, b_ref, o_ref):
    o_ref[0] = _xdot(_silu(c_ref[...]), w_ref[0]) + b_ref[0]


def _modulation(cvec, w_mod, b_mod):
    depth, d, d6 = w_mod.shape
    rows = cvec.shape[0]
    return pl.pallas_call(
        _mod_kernel,
        grid=(depth, d6 // d),
        in_specs=[pl.BlockSpec((rows, d), lambda i, j: (0, 0)),
                  pl.BlockSpec((1, d, d), lambda i, j: (i, 0, j)),
                  pl.BlockSpec((1, 1, d), lambda i, j: (i, 0, j))],
        out_specs=pl.BlockSpec((1, rows, d), lambda i, j: (i, 0, j)),
        out_shape=jax.ShapeDtypeStruct((depth, rows, d6), F32),
        compiler_params=_cparams(2),
        name="modulation",
    )(cvec, w_mod, b_mod.reshape(depth, 1, d6))


IN_GROUPS = (
    ("xbc", SSM_CONV_DIM, F32),
    ("small", LANE, F32),
    ("swa_kv", 2 * SWA_KV_HEADS * SWA_HEAD_DIM, F32),
    ("ckv", MLA_KV_RANK, F32),
    ("z", SSM_INNER, F32),
    ("swa_q", SWA_Q_HEADS * SWA_HEAD_DIM, F32),
    ("cq", MLA_Q_RANK, F32),
    ("gates", None, F32),
)


def _inproj_kernel(x_ref, mod_ref, g_ref, w_ref, *out_refs, bounds):
    x = x_ref[0]
    mod = mod_ref[0, 0]
    h = _rms(x, g_ref[...]) * (1.0 + mod[1:2]) + mod[0:1]
    hb = h.astype(BF16)
    for o_ref, (lo, hi) in zip(out_refs, bounds):
        o_ref[0] = jnp.dot(hb, w_ref[:, lo:hi], preferred_element_type=F32).astype(o_ref.dtype)


def _in_projection(x_all, modsel, norm_g, w_perm, n_ctx_blocks):
    bsz, t, d = x_all.shape
    tm = TOKEN_BLOCK
    widths = [w if w is not None else N_BRANCH * d for _, w, _ in IN_GROUPS]
    offs = [0]
    for w in widths:
        offs.append(offs[-1] + w)
    bounds = tuple((offs[i], offs[i + 1]) for i in range(len(widths)))
    assert w_perm.shape == (d, offs[-1])
    out_shape = [jax.ShapeDtypeStruct((bsz, t, w), dt) for w, (_, _, dt) in zip(widths, IN_GROUPS)]
    out_specs = [pl.BlockSpec((1, tm, w), lambda b, i: (b, i, 0)) for w in widths]
    return pl.pallas_call(
        functools.partial(_inproj_kernel, bounds=bounds),
        grid=(bsz, t // tm),
        in_specs=[pl.BlockSpec((1, tm, d), lambda b, i: (b, i, 0)),
                  pl.BlockSpec((1, 1, SUBLANE, d),
                               lambda b, i: (b, jnp.where(i < n_ctx_blocks, 0, 1), 0, 0)),
                  _resident((1, d)),
                  _resident(w_perm.shape)],
        out_specs=out_specs,
        out_shape=out_shape,
        compiler_params=_cparams(2),
        name="in_projection",
    )(x_all, modsel, norm_g.reshape(1, d), w_perm)


def _permute_in_weights(w_in_l, d):
    o = [0]
    for w in (SSM_CONV_DIM, 2 * SSM_HEADS, SWA_KV_HEADS * SWA_HEAD_DIM, SWA_KV_HEADS * SWA_HEAD_DIM,
              MLA_KV_RANK, MLA_ROPE, SSM_INNER, SWA_Q_HEADS * SWA_HEAD_DIM, MLA_Q_RANK, N_BRANCH * d):
        o.append(o[-1] + w)
    xbc, dt, k, v, ckv, kr, z, q, cq, gates = (w_in_l[:, o[i]:o[i + 1]] for i in range(10))
    pad = jnp.zeros((d, LANE - 2 * SSM_HEADS - MLA_ROPE), w_in_l.dtype)
    return jnp.concatenate([xbc, dt, kr, pad, k, v, ckv, z, q, cq, gates], axis=1).astype(BF16)


def _ssd_scalars(small, dtb_ref, alog_ref):
    q = small.shape[0]
    lane = lax.broadcasted_iota(jnp.int32, (1, LANE), 1)
    raw = small + dtb_ref[...]
    dts = jnp.maximum(raw, 0.0) + jnp.log1p(jnp.exp(-jnp.abs(raw)))
    a = jnp.where(lane < 2 * SSM_HEADS, -jnp.exp(alog_ref[...]), 0.0)
    dta = dts * a
    ri = lax.broadcasted_iota(jnp.int32, (q, q), 0)
    ci = lax.broadcasted_iota(jnp.int32, (q, q), 1)
    tri = (ci <= ri).astype(BF16)
    parts = jnp.dot(tri, _split3(dta), preferred_element_type=F32)
    acs = parts[:, :LANE] + parts[:, LANE:2 * LANE] + parts[:, 2 * LANE:]
    ecs = acs - dta
    return dts, acs, ecs, dts.T, acs.T, ecs.T


def _split3(x):
    hi = x.astype(BF16)
    r1 = x - hi.astype(F32)
    mid = r1.astype(BF16)
    lo = (r1 - mid.astype(F32)).astype(BF16)
    return jnp.concatenate([hi, mid, lo], axis=1)


def _expand_matrix(first_row):
    r = lax.broadcasted_iota(jnp.int32, (3 * LANE, SSM_INNER), 0)
    c = lax.broadcasted_iota(jnp.int32, (3 * LANE, SSM_INNER), 1)
    return (c // SSM_HEAD_DIM + first_row == (r & (LANE - 1))).astype(BF16)


def _ssd_direction(backward, xs, bm, cm, scal, state_ref):
    dts, acs, ecs, dts_t, acs_t, ecs_t = scal
    q = xs.shape[0]
    base = SSM_HEADS if backward else 0
    expand = _expand_matrix(base)
    tot = acs[q - 1:q, :]
    if backward:
        dec_in = jnp.exp(tot - ecs)
        w_out = jnp.exp(ecs) * dts
        pos, pos_t = ecs, ecs_t
    else:
        dec_in = jnp.exp(acs)
        w_out = jnp.exp(tot - acs) * dts
        pos, pos_t = acs, acs_t
    stacked = jnp.concatenate([dec_in, w_out, jnp.broadcast_to(jnp.exp(tot), (SUBLANE, LANE))], axis=0)
    expanded = jnp.dot(_split3(stacked), expand, preferred_element_type=F32)
    dec_e, w_e, tot_e = expanded[0:q], expanded[q:2 * q], expanded[2 * q:2 * q + 1]
    xw = (xs * w_e).astype(BF16)
    xb = xs.astype(BF16)
    ri = lax.broadcasted_iota(jnp.int32, (q, q), 0)
    ci = lax.broadcasted_iota(jnp.int32, (q, q), 1)
    keep = (ci >= ri) if backward else (ci <= ri)
    gw = SSM_HPG * SSM_HEAD_DIM
    ys = []
    for g in range(SSM_GROUPS):
        b_g = bm[:, g * SSM_STATE:(g + 1) * SSM_STATE]
        c_g = cm[:, g * SSM_STATE:(g + 1) * SSM_STATE].astype(BF16)
        b_t = b_g.T.astype(BF16)
        cb = jnp.dot(c_g, b_t, preferred_element_type=F32)
        st = state_ref[g]
        y_off = jnp.dot(c_g, st.astype(BF16), preferred_element_type=F32) * dec_e[:, g * gw:(g + 1) * gw]
        state_ref[g] = st * tot_e[:, g * gw:(g + 1) * gw] + jnp.dot(
            b_t, xw[:, g * gw:(g + 1) * gw], preferred_element_type=F32)
        heads = []
        for hh in range(SSM_HPG):
            h = g * SSM_HPG + hh
            col = pos[:, base + h:base + h + 1]
            row = pos_t[base + h:base + h + 1, :]
            diff = (row - col) if backward else (col - row)
            seg = jnp.exp(jnp.where(keep, diff, -jnp.inf))
            m = (cb * seg * dts_t[base + h:base + h + 1, :]).astype(BF16)
            heads.append(jnp.dot(m, xb[:, h * SSM_HEAD_DIM:(h + 1) * SSM_HEAD_DIM],
                                 preferred_element_type=F32))
        ys.append(jnp.concatenate(heads, axis=1) + y_off)
    return jnp.concatenate(ys, axis=1)


def _ssd_fwd_kernel(xc_ref, xp_ref, xn_ref, small_ref, cw_ref, cb_ref, dtb_ref, alog_ref, dskip_ref,
                    y_ref, u_ref, state_ref, *, n_ctx_chunks, n_chunks):
    c = pl.program_id(1)

    @pl.when(c == 0)
    def _():
        state_ref[...] = jnp.zeros_like(state_ref)

    prev_ok = jnp.logical_and(c != 0, c != n_ctx_chunks)
    next_ok = jnp.logical_and(c != n_ctx_chunks - 1, c != n_chunks - 1)
    xp = jnp.where(prev_ok, xp_ref[0], 0.0)
    xn = jnp.where(next_ok, xn_ref[0], 0.0)
    xc = xc_ref[0]
    q = xc.shape[0]
    ext = jnp.concatenate([xp, xc, xn], axis=0)
    half = SSM_CONV // 2
    acc = jnp.zeros_like(xc) + cb_ref[...]
    for k in range(SSM_CONV):
        lo = HALO - half + k
        acc = acc + ext[lo:lo + q, :] * cw_ref[k:k + 1, :]
    u = _silu(acc)
    u_ref[0] = u.astype(u_ref.dtype)
    xs = u[:, :SSM_INNER]
    bm = u[:, SSM_INNER:SSM_INNER + SSM_BC]
    cm = u[:, SSM_INNER + SSM_BC:]
    scal = _ssd_scalars(small_ref[0], dtb_ref, alog_ref)
    y = _ssd_direction(False, xs, bm, cm, scal, state_ref)
    y_ref[0] = y + dskip_ref[...] * xs


def _ssd_bwd_kernel(u_ref, small_ref, z_ref, yf_ref, dtb_ref, alog_ref, ng_ref, o_ref, state_ref):
    @pl.when(pl.program_id(1) == 0)
    def _():
        state_ref[...] = jnp.zeros_like(state_ref)

    u = u_ref[0].astype(F32)
    xs = u[:, :SSM_INNER]
    bm = u[:, SSM_INNER:SSM_INNER + SSM_BC]
    cm = u[:, SSM_INNER + SSM_BC:]
    scal = _ssd_scalars(small_ref[0], dtb_ref, alog_ref)
    y = yf_ref[0] + _ssd_direction(True, xs, bm, cm, scal, state_ref)
    o_ref[0] = _rms(y * _silu(z_ref[0]), ng_ref[...]).astype(o_ref.dtype)


def _ssd_branch(xbc, small, z, conv_w, conv_b, dt_bias, a_log, d_skip, norm_g, n_ctx):
    bsz, t, _ = xbc.shape
    q = SSM_CHUNK
    n_chunks = t // q
    n_ctx_chunks = n_ctx // q
    hb = q // HALO
    n_halo = t // HALO
    pad32 = LANE - 2 * SSM_HEADS
    dtb = jnp.pad(dt_bias.reshape(1, -1), ((0, 0), (0, pad32)))
    alog = jnp.pad(a_log.reshape(1, -1), ((0, 0), (0, pad32)))
    dskip = jnp.repeat(d_skip, SSM_HEAD_DIM).reshape(1, SSM_INNER)
    state = pltpu.VMEM((SSM_GROUPS, SSM_STATE, SSM_HPG * SSM_HEAD_DIM), F32)

    def chunk(width):
        return pl.BlockSpec((1, q, width), lambda b, c: (b, c, 0))

    y_f, u = pl.pallas_call(
        functools.partial(_ssd_fwd_kernel, n_ctx_chunks=n_ctx_chunks, n_chunks=n_chunks),
        grid=(bsz, n_chunks),
        in_specs=[chunk(SSM_CONV_DIM),
                  pl.BlockSpec((1, HALO, SSM_CONV_DIM), lambda b, c: (b, jnp.maximum(c * hb - 1, 0), 0)),
                  pl.BlockSpec((1, HALO, SSM_CONV_DIM),
                               lambda b, c: (b, jnp.minimum((c + 1) * hb, n_halo - 1), 0)),
                  chunk(LANE),
                  _resident((SSM_CONV, SSM_CONV_DIM)), _resident((1, SSM_CONV_DIM)),
                  _resident((1, LANE)), _resident((1, LANE)), _resident((1, SSM_INNER))],
        out_specs=[chunk(SSM_INNER), chunk(SSM_CONV_DIM)],
        out_shape=[jax.ShapeDtypeStruct((bsz, t, SSM_INNER), F32),
                   jax.ShapeDtypeStruct((bsz, t, SSM_CONV_DIM), BF16)],
        scratch_shapes=[state],
        compiler_params=_cparams(2),
        name="ssd_forward",
    )(xbc, xbc, xbc, small, conv_w, conv_b.reshape(1, -1), dtb, alog, dskip)

    def rchunk(width):
        return pl.BlockSpec(
            (1, q, width),
            lambda b, s: (b, jnp.where(s < n_ctx_chunks, n_ctx_chunks - 1 - s,
                                       n_chunks + n_ctx_chunks - 1 - s), 0))

    return pl.pallas_call(
        _ssd_bwd_kernel,
        grid=(bsz, n_chunks),
        in_specs=[rchunk(SSM_CONV_DIM), rchunk(LANE), rchunk(SSM_INNER), rchunk(SSM_INNER),
                  _resident((1, LANE)), _resident((1, LANE)), _resident((1, SSM_INNER))],
        out_specs=rchunk(SSM_INNER),
        out_shape=jax.ShapeDtypeStruct((bsz, t, SSM_INNER), BF16),
        scratch_shapes=[state],
        compiler_params=_cparams(2),
        name="ssd_backward",
    )(u, small, z, y_f, dtb, alog, norm_g.reshape(1, -1))


def _rope_tables(n_lat, n_ctx, rot_dim):
    n_freq = rot_dim // 4
    inv = jnp.power(ROPE_BASE, -jnp.arange(n_freq, dtype=F32) / n_freq)
    t = jnp.arange(n_lat)
    r = (t // GRID_W).astype(F32)[:, None] * inv
    col = (t % GRID_W).astype(F32)[:, None] * inv
    cos2 = jnp.concatenate([jnp.cos(r), jnp.cos(r), jnp.cos(col), jnp.cos(col)], axis=1)
    sin2 = jnp.concatenate([-jnp.sin(r), jnp.sin(r), -jnp.sin(col), jnp.sin(col)], axis=1)
    cos2 = jnp.concatenate([jnp.ones((n_ctx, rot_dim), F32), cos2], axis=0)
    sin2 = jnp.concatenate([jnp.zeros((n_ctx, rot_dim), F32), sin2], axis=0)
    return cos2, sin2


def _rotate_half(x):
    f = x.shape[-1] // 4
    return jnp.concatenate([x[:, f:2 * f], x[:, :f], x[:, 3 * f:], x[:, 2 * f:3 * f]], axis=1)


def _swa_prep_kernel(q_ref, kv_ref, qg_ref, kg_ref, cos_ref, sin_ref, qo_ref, ko_ref, vo_ref):
    cos, sin = cos_ref[...], sin_ref[...]
    scale = SWA_HEAD_DIM ** -0.5
    qf = q_ref[0]
    kvf = kv_ref[0]
    dh = SWA_HEAD_DIM
    for h in range(SWA_Q_HEADS):
        x = _rms(qf[:, h * dh:(h + 1) * dh], qg_ref[...])
        x = x * cos + _rotate_half(x) * sin
        qo_ref[0, :, h * dh:(h + 1) * dh] = (x * scale).astype(qo_ref.dtype)
    for h in range(SWA_KV_HEADS):
        x = _rms(kvf[:, h * dh:(h + 1) * dh], kg_ref[...])
        x = x * cos + _rotate_half(x) * sin
        ko_ref[0, :, h * dh:(h + 1) * dh] = x.astype(ko_ref.dtype)
    vo_ref[0] = kvf[:, SWA_KV_HEADS * dh:].astype(vo_ref.dtype)


def _swa_prep(q, kv, q_g, k_g, cos, sin):
    bsz, t, _ = q.shape
    tm = TOKEN_BLOCK
    dq = SWA_Q_HEADS * SWA_HEAD_DIM
    dkv = SWA_KV_HEADS * SWA_HEAD_DIM
    return pl.pallas_call(
        _swa_prep_kernel,
        grid=(bsz, t // tm),
        in_specs=[pl.BlockSpec((1, tm, dq), lambda b, i: (b, i, 0)),
                  pl.BlockSpec((1, tm, 2 * dkv), lambda b, i: (b, i, 0)),
                  _resident((1, SWA_HEAD_DIM)), _resident((1, SWA_HEAD_DIM)),
                  pl.BlockSpec((tm, SWA_HEAD_DIM), lambda b, i: (i, 0)),
                  pl.BlockSpec((tm, SWA_HEAD_DIM), lambda b, i: (i, 0))],
        out_specs=[pl.BlockSpec((1, tm, dq), lambda b, i: (b, i, 0)),
                   pl.BlockSpec((1, tm, dkv), lambda b, i: (b, i, 0)),
                   pl.BlockSpec((1, tm, dkv), lambda b, i: (b, i, 0))],
        out_shape=[jax.ShapeDtypeStruct((bsz, t, dq), BF16),
                   jax.ShapeDtypeStruct((bsz, t, dkv), BF16),
                   jax.ShapeDtypeStruct((bsz, t, dkv), BF16)],
        compiler_params=_cparams(2),
        name="swa_prep",
    )(q, kv, q_g.reshape(1, -1), k_g.reshape(1, -1), cos, sin)


def _swa_kernel(sink_ref, q_ref, k_ref, v_ref, o_ref, *, first_block, n_ctx, t):
    hk = pl.program_id(1)
    blk = pl.program_id(2) + first_block
    n_ctx_blocks = n_ctx // SWA_BLOCK
    bq = SWA_BLOCK
    win = 3 * SWA_BLOCK
    dh = SWA_HEAD_DIM
    qb = q_ref[0]
    q4 = jnp.concatenate([qb[:, g * dh:(g + 1) * dh] for g in range(SWA_GRP)], axis=0)
    start = jnp.clip((blk - 1) * bq, 0, t - win)
    start = pl.multiple_of(start, bq)
    kw = k_ref[0, pl.ds(start, win), :]
    vw = v_ref[0, pl.ds(start, win), :]
    kc = k_ref[0, 0:n_ctx, :]
    vc = v_ref[0, 0:n_ctx, :]
    s_w = _bdot_nt(q4, kw)
    s_c = _bdot_nt(q4, kc)
    rows = lax.broadcasted_iota(jnp.int32, (SWA_GRP * bq, win), 0)
    cols = lax.broadcasted_iota(jnp.int32, (SWA_GRP * bq, win), 1)
    qpos = (blk - n_ctx_blocks) * bq + (rows & (bq - 1))
    kpos = start - n_ctx + cols
    ok = (jnp.abs(kpos - qpos) <= SWA_WINDOW) & (kpos >= 0) & (blk >= n_ctx_blocks)
    s_w = jnp.where(ok, s_w, -jnp.inf)
    r1 = lax.broadcasted_iota(jnp.int32, (SWA_GRP * bq, 1), 0)
    sink = jnp.zeros((SWA_GRP * bq, 1), F32)
    for g in range(SWA_GRP):
        sink = jnp.where(r1 // bq == g, sink_ref[hk * SWA_GRP + g], sink)
    m = jnp.maximum(jnp.maximum(jnp.max(s_w, axis=-1, keepdims=True),
                                jnp.max(s_c, axis=-1, keepdims=True)), sink)
    p_w = jnp.exp(s_w - m)
    p_c = jnp.exp(s_c - m)
    den = (jnp.sum(p_w, axis=-1, keepdims=True) + jnp.sum(p_c, axis=-1, keepdims=True)
           + jnp.exp(sink - m))
    o = (_bdot(p_w, vw) + _bdot(p_c, vc)) / den
    for g in range(SWA_GRP):
        o_ref[0, :, g * dh:(g + 1) * dh] = o[g * bq:(g + 1) * bq, :].astype(o_ref.dtype)


def _swa_attention(q, k, v, sink, n_ctx, first_block):
    bsz, t, dq = q.shape
    bq = SWA_BLOCK
    gw = SWA_GRP * SWA_HEAD_DIM
    nblk = t // bq - first_block
    return pl.pallas_call(
        functools.partial(_swa_kernel, first_block=first_block, n_ctx=n_ctx, t=t),
        grid=(bsz, SWA_KV_HEADS, nblk),
        in_specs=[pl.BlockSpec(memory_space=pltpu.SMEM),
                  pl.BlockSpec((1, bq, gw), lambda b, h, n: (b, n + first_block, h)),
                  pl.BlockSpec((1, t, SWA_HEAD_DIM), lambda b, h, n: (b, 0, h)),
                  pl.BlockSpec((1, t, SWA_HEAD_DIM), lambda b, h, n: (b, 0, h))],
        out_specs=pl.BlockSpec((1, bq, gw), lambda b, h, n: (b, n + first_block, h)),
        out_shape=jax.ShapeDtypeStruct((bsz, t, dq), BF16),
        compiler_params=_cparams(3),
        name="swa_attention",
    )(sink, q, k, v)


def _rms_cols(x, g):
    return x * lax.rsqrt(jnp.mean(x * x, axis=0, keepdims=True) + EPS) * g


def _rotate_half_rows(x):
    f = x.shape[0] // 4
    return jnp.concatenate([x[f:2 * f], x[:f], x[3 * f:], x[2 * f:3 * f]], axis=0)


def _mla_prep_kernel(cq_ref, ckv_ref, small_ref, wqt_ref, wk_ref, wvt_ref, qlg_ref, kvlg_ref,
                     qgn_ref, qgr_ref, kgn_ref, kgr_ref, cos_ref, sin_ref, cost_ref, sint_ref,
                     qo_ref, ko_ref, vo_ref):
    qscale = MLA_QK ** -0.5 * math.log2(math.e)
    nh, dn, dr, dv = MLA_HEADS, MLA_NOPE, MLA_ROPE, MLA_V

    cqn_t = _rms(cq_ref[0], qlg_ref[...]).T.astype(BF16)
    qf_t = jnp.dot(wqt_ref[...], cqn_t, preferred_element_type=F32)
    cos_t, sin_t = cost_ref[...], sint_ref[...]
    for h in range(nh):
        qn = _rms_cols(qf_t[h * dn:(h + 1) * dn], qgn_ref[...])
        qr = _rms_cols(qf_t[nh * dn + h * dr:nh * dn + (h + 1) * dr], qgr_ref[...])
        qr = qr * cos_t + _rotate_half_rows(qr) * sin_t
        qo_ref[0, h, 0:dn, :] = (qn * qscale).astype(qo_ref.dtype)
        qo_ref[0, h, dn:dn + dr, :] = (qr * qscale).astype(qo_ref.dtype)

    ckvn = _rms(ckv_ref[0], kvlg_ref[...])
    kf = jnp.dot(ckvn.astype(BF16), wk_ref[...], preferred_element_type=F32)
    kr0 = 2 * SSM_HEADS
    kr = _rms(small_ref[0][:, kr0:kr0 + dr], kgr_ref[...])
    kr = kr * cos_ref[...] + _rotate_half(kr) * sin_ref[...]
    for h in range(nh):
        kn = _rms(kf[:, h * dn:(h + 1) * dn], kgn_ref[...])
        ko_ref[0, h] = jnp.concatenate([kn, kr], axis=1).astype(ko_ref.dtype)
    vf_t = jnp.dot(wvt_ref[...], ckvn.T.astype(BF16), preferred_element_type=F32)
    for h in range(nh):
        vo_ref[0, h, 0, 0:dv, :] = vf_t[h * dv:(h + 1) * dv].astype(vo_ref.dtype)
        vo_ref[0, h, 0, dv:, :] = jnp.ones((MLA_VT_ROWS - dv, vo_ref.shape[-1]), vo_ref.dtype)


def _mla_prep(cq, ckv, small, w_uq, w_ukv, q_lat_g, kv_lat_g, q_g, k_g, cos, sin):
    bsz, t, _ = cq.shape
    tm = TOKEN_BLOCK
    nh = MLA_HEADS
    wq_t = _head_major(w_uq, nh, (MLA_NOPE, MLA_ROPE)).T.astype(BF16)
    wkv = _head_major(w_ukv, nh, (MLA_NOPE, MLA_V))
    wk = wkv[:, :nh * MLA_NOPE].astype(BF16)
    wv_t = wkv[:, nh * MLA_NOPE:].T.astype(BF16)

    def tok(width):
        return pl.BlockSpec((1, tm, width), lambda b, i: (b, i, 0))

    def cols(g):
        return jnp.broadcast_to(g[:, None], (g.shape[0], tm))

    return pl.pallas_call(
        _mla_prep_kernel,
        grid=(bsz, t // tm),
        in_specs=[tok(MLA_Q_RANK), tok(MLA_KV_RANK), tok(LANE),
                  _resident(wq_t.shape), _resident(wk.shape), _resident(wv_t.shape),
                  _resident((1, MLA_Q_RANK)), _resident((1, MLA_KV_RANK)),
                  _resident((MLA_NOPE, tm)), _resident((MLA_ROPE, tm)),
                  _resident((1, MLA_NOPE)), _resident((1, MLA_ROPE)),
                  pl.BlockSpec((tm, MLA_ROPE), lambda b, i: (i, 0)),
                  pl.BlockSpec((tm, MLA_ROPE), lambda b, i: (i, 0)),
                  pl.BlockSpec((MLA_ROPE, tm), lambda b, i: (0, i)),
                  pl.BlockSpec((MLA_ROPE, tm), lambda b, i: (0, i))],
        out_specs=[pl.BlockSpec((1, nh, MLA_QK, tm), lambda b, i: (b, 0, 0, i)),
                   pl.BlockSpec((1, nh, tm, MLA_QK), lambda b, i: (b, 0, i, 0)),
                   pl.BlockSpec((1, nh, 1, MLA_VT_ROWS, tm), lambda b, i: (b, 0, i, 0, 0))],
        out_shape=[jax.ShapeDtypeStruct((bsz, nh, MLA_QK, t), BF16),
                   jax.ShapeDtypeStruct((bsz, nh, t, MLA_QK), BF16),
                   jax.ShapeDtypeStruct((bsz, nh, t // tm, MLA_VT_ROWS, tm), BF16)],
        compiler_params=_cparams(2),
        name="mla_prep",
    )(cq, ckv, small, wq_t, wk, wv_t, q_lat_g.reshape(1, -1), kv_lat_g.reshape(1, -1),
      cols(q_g[:MLA_NOPE]), cols(q_g[MLA_NOPE:]),
      k_g[:MLA_NOPE].reshape(1, -1), k_g[MLA_NOPE:].reshape(1, -1), cos, sin, cos.T, sin.T)


MLA_HEADS_PER_STEP = 2
MLA_KEY_SUBBLOCKS = 1
MLA_UNROLL_CHUNKS = 40


def _mla_kernel(qt_ref, k_ref, vt_ref, o_ref, s_scr, p_scr, a_scr, mx_scr, acc_scr, m_scr, *,
                first_block, n_ctx_blocks, n_key_blocks, sub):
    blk = pl.program_id(2) + first_block
    heads = qt_ref.shape[1]
    kb = vt_ref.shape[4]
    tk = sub * kb
    n_chunks = n_key_blocks // sub

    def softmax_update(c, s, s_max):
        m_prev = m_scr[c]
        m_new = jnp.maximum(m_prev, s_max)
        m_scr[c] = m_new
        return jnp.exp2(m_prev - m_new), jnp.exp2(s - m_new).astype(BF16)

    def value_update(c, alpha, pb, blk0, nsub):
        pv = jnp.dot(vt_ref[0, c, blk0], pb[0:kb], preferred_element_type=F32)
        for u in range(1, nsub):
            pv = pv + jnp.dot(vt_ref[0, c, blk0 + u], pb[u * kb:(u + 1) * kb], preferred_element_type=F32)
        acc_scr[c] = alpha * acc_scr[c] + pv

    def stage_scores(c, t, slot):
        row0 = t * tk if isinstance(t, int) else pl.multiple_of(t * tk, tk)
        s = jnp.dot(k_ref[0, c, pl.ds(row0, tk), :], qt_ref[0, c], preferred_element_type=F32)
        s_scr[c, slot] = s
        mx_scr[c, slot] = jnp.max(s, axis=0, keepdims=True)

    def stage_softmax(c, slot):
        alpha, pb = softmax_update(c, s_scr[c, slot], mx_scr[c, slot])
        a_scr[c, slot] = alpha
        p_scr[c, slot] = pb

    def stage_values(c, t, slot):
        value_update(c, a_scr[c, slot], p_scr[c, slot], t * sub, sub)

    def tick(t, parity):
        static = isinstance(t, int)
        for c in range(heads):
            if not static or t < n_chunks:
                stage_scores(c, t, parity)
            if not static or 1 <= t <= n_chunks:
                stage_softmax(c, 1 - parity)
            if not static or 2 <= t <= n_chunks + 1:
                stage_values(c, t - 2, parity)

    def reset():
        m_scr[...] = jnp.full(m_scr.shape, -jnp.inf, F32)
        acc_scr[...] = jnp.zeros(acc_scr.shape, F32)

    def finish():
        for c in range(heads):
            acc = acc_scr[c]
            o = acc[0:MLA_V] / acc[MLA_V:MLA_V + 1]
            o_ref[0, :, c * MLA_V:(c + 1) * MLA_V] = o.T.astype(o_ref.dtype)

    @pl.when(blk < n_ctx_blocks)
    def _():
        reset()
        for c in range(heads):
            s = jnp.dot(k_ref[0, c, 0:n_ctx_blocks * kb, :], qt_ref[0, c], preferred_element_type=F32)
            alpha, pb = softmax_update(c, s, jnp.max(s, axis=0, keepdims=True))
            value_update(c, alpha, pb, 0, n_ctx_blocks)
        finish()

    @pl.when(blk >= n_ctx_blocks)
    def _():
        reset()
        n_pairs = 0 if n_chunks <= MLA_UNROLL_CHUNKS else (n_chunks - 2) // 2
        for t in range(0, min(2, n_chunks + 2)):
            tick(t, t % 2)

        def body(i, carry):
            t0 = 2 + 2 * i
            tick(t0, 0)
            tick(t0 + 1, 1)
            return carry

        lax.fori_loop(0, n_pairs, body, 0)
        for t in range(2 + 2 * n_pairs, n_chunks + 2):
            tick(t, t % 2)
        finish()


def _mla_attention(qt, k, vt, n_ctx, first_block):
    bsz, nh, dqk, t = qt.shape
    tq = TOKEN_BLOCK
    kb = vt.shape[-1]
    n_key_blocks = t // kb
    sub = MLA_KEY_SUBBLOCKS if n_key_blocks % MLA_KEY_SUBBLOCKS == 0 else 1
    g = MLA_HEADS_PER_STEP
    nblk = t // tq - first_block
    return pl.pallas_call(
        functools.partial(_mla_kernel, first_block=first_block, n_ctx_blocks=n_ctx // kb,
                          n_key_blocks=n_key_blocks, sub=sub),
        grid=(bsz, nh // g, nblk),
        in_specs=[pl.BlockSpec((1, g, dqk, tq), lambda b, h, i: (b, h, 0, i + first_block)),
                  pl.BlockSpec((1, g, t, dqk), lambda b, h, i: (b, h, 0, 0)),
                  pl.BlockSpec((1, g, n_key_blocks, MLA_VT_ROWS, kb), lambda b, h, i: (b, h, 0, 0, 0))],
        out_specs=pl.BlockSpec((1, tq, g * MLA_V), lambda b, h, i: (b, i + first_block, h)),
        out_shape=jax.ShapeDtypeStruct((bsz, t, nh * MLA_V), BF16),
        scratch_shapes=[pltpu.VMEM((g, 2, sub * kb, tq), F32),
                        pltpu.VMEM((g, 2, sub * kb, tq), BF16),
                        pltpu.VMEM((g, 2, 1, tq), F32),
                        pltpu.VMEM((g, 2, 1, tq), F32),
                        pltpu.VMEM((g, MLA_VT_ROWS, tq), F32),
                        pltpu.VMEM((g, 1, tq), F32)],
        compiler_params=_cparams(3),
        name="mla_attention",
    )(qt, k, vt)


def _ffn_chunks(hidden):
    step = 512
    return tuple((lo, min(lo + step, hidden)) for lo in range(0, hidden, step))


def _merge_ffn_kernel(x_ref, mod_ref, gates_ref, ys_ref, yw_ref, ym_ref, wps_ref, wpw_ref, wpm_ref,
                      wo_ref, g2_ref, wfi_ref, wfo_ref, o_ref):
    d = x_ref.shape[-1]
    hidden = wfo_ref.shape[0]
    mod = mod_ref[0, 0]
    gt1, sh2, sc2, gt2 = mod[2:3], mod[3:4], mod[4:5], mod[5:6]
    gates = jax.nn.sigmoid(gates_ref[0])
    merged = (gates[:, 0:d] * jnp.dot(ys_ref[0], wps_ref[...], preferred_element_type=F32)
              + gates[:, d:2 * d] * jnp.dot(yw_ref[0], wpw_ref[...], preferred_element_type=F32)
              + gates[:, 2 * d:3 * d] * jnp.dot(ym_ref[0], wpm_ref[...], preferred_element_type=F32))
    x1 = x_ref[0] + gt1 * jnp.dot(merged.astype(BF16), wo_ref[...], preferred_element_type=F32)
    hb = (_rms(x1, g2_ref[...]) * (1.0 + sc2) + sh2).astype(BF16)
    acc = jnp.zeros_like(x1)
    for lo, hi in _ffn_chunks(hidden):
        gate = jnp.dot(hb, wfi_ref[:, lo:hi], preferred_element_type=F32)
        up = jnp.dot(hb, wfi_ref[:, hidden + lo:hidden + hi], preferred_element_type=F32)
        acc = acc + jnp.dot((_silu(gate) * up).astype(BF16), wfo_ref[lo:hi, :], preferred_element_type=F32)
    o_ref[0] = x1 + gt2 * acc


def _merge_ffn(x_all, modsel, gates, y_ssm, y_swa, y_mla, wps, wpw, wpm, wo, norm2_g, wfi, wfo,
               n_ctx_blocks, first_block):
    bsz, t, d = x_all.shape
    tm = TOKEN_BLOCK
    nblk = t // tm - first_block

    def tok(width):
        return pl.BlockSpec((1, tm, width), lambda b, i: (b, i + first_block, 0))

    return pl.pallas_call(
        _merge_ffn_kernel,
        grid=(bsz, nblk),
        in_specs=[tok(d),
                  pl.BlockSpec((1, 1, SUBLANE, d),
                               lambda b, i: (b, jnp.where(i + first_block < n_ctx_blocks, 0, 1), 0, 0)),
                  tok(N_BRANCH * d), tok(y_ssm.shape[-1]), tok(y_swa.shape[-1]), tok(y_mla.shape[-1]),
                  _resident(wps.shape), _resident(wpw.shape), _resident(wpm.shape), _resident(wo.shape),
                  _resident((1, d)), _resident(wfi.shape), _resident(wfo.shape)],
        out_specs=pl.BlockSpec((1, tm, d), lambda b, i: (b, i, 0)),
        out_shape=jax.ShapeDtypeStruct((bsz, nblk * tm, d), F32),
        compiler_params=_cparams(2),
        name="merge_ffn",
    )(x_all, modsel, gates, y_ssm, y_swa, y_mla, wps, wpw, wpm, wo, norm2_g.reshape(1, d), wfi, wfo)


def _head_major(w, n_heads, parts):
    k = w.shape[0]
    w = w.reshape(k, n_heads, sum(parts))
    out, lo = [], 0
    for p in parts:
        out.append(w[:, :, lo:lo + p].reshape(k, n_heads * p))
        lo += p
    return jnp.concatenate(out, axis=1)


def kernel(x, c, ctx, c_ctx, w_mod, b_mod, norm1_g, norm2_g, w_in, ssm_conv_w, ssm_conv_b, ssm_dt_bias,
           ssm_a_log, ssm_d, ssm_norm_g, swa_q_norm_g, swa_k_norm_g, swa_sink, mla_q_lat_g, mla_kv_lat_g,
           w_mla_uq, w_mla_ukv, mla_q_norm_g, mla_k_norm_g, w_p_ssm, w_p_swa, w_p_mla, w_out, w_ffn_in,
           w_ffn_out):
    bsz, n_lat, d = x.shape
    n_ctx = ctx.shape[1]
    depth = w_mod.shape[0]
    assert n_ctx % TOKEN_BLOCK == 0 and n_lat % TOKEN_BLOCK == 0 and n_lat % GRID_W == 0
    assert bsz + 1 <= SUBLANE
    n_ctx_blocks = n_ctx // TOKEN_BLOCK

    cvec = jnp.concatenate([c, c_ctx[None], jnp.zeros((SUBLANE - bsz - 1, d), F32)], axis=0)
    mods = _modulation(cvec, w_mod, b_mod).reshape(depth, SUBLANE, 6, d)
    cos_swa, sin_swa = _rope_tables(n_lat, n_ctx, SWA_HEAD_DIM)
    cos_mla, sin_mla = _rope_tables(n_lat, n_ctx, MLA_ROPE)

    x_all = jnp.concatenate([ctx, x], axis=1)
    for i in range(depth):
        last = i == depth - 1
        ctx_mod = jnp.broadcast_to(mods[i, bsz][None], (bsz, 6, d))
        modsel = jnp.pad(jnp.stack([ctx_mod, mods[i, :bsz]], axis=1), ((0, 0), (0, 0), (0, SUBLANE - 6), (0, 0)))

        xbc, small, swa_kv, ckv, z, swa_q, cq, gates = _in_projection(
            x_all, modsel, norm1_g[i], _permute_in_weights(w_in[i], d), n_ctx_blocks)

        y_ssm = _ssd_branch(xbc, small, z, ssm_conv_w[i], ssm_conv_b[i], ssm_dt_bias[i], ssm_a_log[i],
                            ssm_d[i], ssm_norm_g[i], n_ctx)

        first_swa = n_ctx // SWA_BLOCK if last else 0
        qs, ks, vs = _swa_prep(swa_q, swa_kv, swa_q_norm_g[i], swa_k_norm_g[i], cos_swa, sin_swa)
        y_swa = _swa_attention(qs, ks, vs, swa_sink[i], n_ctx, first_swa)

        first_tok = n_ctx_blocks if last else 0
        qm, km, vm = _mla_prep(cq, ckv, small, w_mla_uq[i], w_mla_ukv[i], mla_q_lat_g[i], mla_kv_lat_g[i],
                               mla_q_norm_g[i], mla_k_norm_g[i], cos_mla, sin_mla)
        y_mla = _mla_attention(qm, km, vm, n_ctx, first_tok)

        x_all = _merge_ffn(x_all, modsel, gates, y_ssm, y_swa, y_mla,
                           w_p_ssm[i].astype(BF16), w_p_swa[i].astype(BF16), w_p_mla[i].astype(BF16),
                           w_out[i].astype(BF16), norm2_g[i], w_ffn_in[i].astype(BF16),
                           w_ffn_out[i].astype(BF16), n_ctx_blocks, first_tok)
    return x_all
```

```python
import functools
import math

import jax
import jax.numpy as jnp
from jax import lax
from jax.experimental import pallas as pl
from jax.experimental.pallas import tpu as pltpu

F32 = jnp.float32
BF16 = jnp.bfloat16
HIGHEST = lax.Precision.HIGHEST

EPS = 1e-6
ROPE_BASE = 10000.0
GRID_W = 64

SSM_HEADS = 16
SSM_HEAD_DIM = 64
SSM_INNER = SSM_HEADS * SSM_HEAD_DIM
SSM_GROUPS = 2
SSM_STATE = 128
SSM_CONV = 5
SSM_CHUNK = 128
SSM_BC = SSM_GROUPS * SSM_STATE
SSM_CONV_DIM = SSM_INNER + 2 * SSM_BC
SSM_HPG = SSM_HEADS // SSM_GROUPS

SWA_Q_HEADS = 8
SWA_KV_HEADS = 2
SWA_HEAD_DIM = 128
SWA_WINDOW = 128
SWA_BLOCK = 128
SWA_GRP = SWA_Q_HEADS // SWA_KV_HEADS

MLA_HEADS = 8
MLA_Q_RANK = 384
MLA_KV_RANK = 256
MLA_NOPE = 128
MLA_ROPE = 64
MLA_QK = MLA_NOPE + MLA_ROPE
MLA_V = 128
BF16_SUBLANES = 16
MLA_VT_ROWS = MLA_V + BF16_SUBLANES

N_BRANCH = 3
LANE = 128
SUBLANE = 8
HALO = SUBLANE
TOKEN_BLOCK = 256
VMEM_LIMIT = 56 * 1024 * 1024


def _cparams(n_axes):
    return pltpu.CompilerParams(
        dimension_semantics=("arbitrary",) * n_axes, vmem_limit_bytes=VMEM_LIMIT)


def _resident(shape):
    nd = len(shape)
    return pl.BlockSpec(shape, lambda *_: (0,) * nd, pipeline_mode=pl.Buffered(1))


def _rms(x, g):
    return x * lax.rsqrt(jnp.mean(x * x, axis=-1, keepdims=True) + EPS) * g


def _silu(x):
    return x * jax.nn.sigmoid(x)


def _xdot(a, b):
    return jnp.dot(a, b, precision=HIGHEST, preferred_element_type=F32)


def _mod_kernel(c_ref, w_ref, b_ref, o_ref):
    o_ref[0] = _xdot(_silu(c_ref[...]), w_ref[0]) + b_ref[0]


def _modulation(cvec, w_mod, b_mod):
    depth, d, d6 = w_mod.shape
    rows = cvec.shape[0]
    return pl.pallas_call(
        _mod_kernel,
        grid=(depth, d6 // d),
        in_specs=[pl.BlockSpec((rows, d), lambda i, j: (0, 0)),
                  pl.BlockSpec((1, d, d), lambda i, j: (i, 0, j)),
                  pl.BlockSpec((1, 1, d), lambda i, j: (i, 0, j))],
        out_specs=pl.BlockSpec((1, rows, d), lambda i, j: (i, 0, j)),
        out_shape=jax.ShapeDtypeStruct((depth, rows, d6), F32),
        compiler_params=_cparams(2),
        name="modulation",
    )(cvec, w_mod, b_mod.reshape(depth, 1, d6))


SWA_DQ = SWA_Q_HEADS * SWA_HEAD_DIM
SWA_DKV = SWA_KV_HEADS * SWA_HEAD_DIM
SWA_VT_ROWS = SWA_HEAD_DIM + BF16_SUBLANES
LOG2E = math.log2(math.e)


def _split_in_weights(w_in_l, d):
    o = [0]
    for w in (SSM_CONV_DIM, 2 * SSM_HEADS, SWA_DKV, SWA_DKV, MLA_KV_RANK, MLA_ROPE, SSM_INNER, SWA_DQ,
              MLA_Q_RANK, N_BRANCH * d):
        o.append(o[-1] + w)
    xbc, dt, k, v, ckv, kr, z, q, cq, gates = (w_in_l[:, o[i]:o[i + 1]] for i in range(10))
    pad = jnp.zeros((d, LANE - 2 * SSM_HEADS - MLA_ROPE), w_in_l.dtype)
    w_tok = jnp.concatenate([xbc, dt, kr, pad, k, ckv, z, gates], axis=1).astype(BF16)
    w_feat = jnp.concatenate([q, v, cq], axis=1).T.astype(BF16)
    return w_tok, w_feat


def _inproj_kernel(x_ref, mod_ref, g_ref, wtok_ref, wfeat_ref, wqt_ref, wk_ref, wvt_ref,
                   sqg_ref, skg_ref, qlg_ref, kvlg_ref, qgn_ref, qgr_ref, kgn_ref, kgr_ref,
                   scos_ref, ssin_ref, scost_ref, ssint_ref, mcos_ref, msin_ref, mcost_ref, msint_ref,
                   xbc_ref, small_ref, z_ref, gates_ref, sq_ref, sk_ref, sv_ref, mq_ref, mk_ref, mv_ref):
    d = x_ref.shape[-1]
    tm = x_ref.shape[1]
    mod = mod_ref[0, 0]
    hb = (_rms(x_ref[0], g_ref[...]) * (1.0 + mod[1:2]) + mod[0:1]).astype(BF16)

    def tok(lo, width):
        return jnp.dot(hb, wtok_ref[:, lo:lo + width], preferred_element_type=F32)

    o_small = SSM_CONV_DIM
    o_k = o_small + LANE
    o_ckv = o_k + SWA_DKV
    o_z = o_ckv + MLA_KV_RANK
    o_gates = o_z + SSM_INNER
    dh = SWA_HEAD_DIM
    nh, dn, dr, dv = MLA_HEADS, MLA_NOPE, MLA_ROPE, MLA_V
    qscale = MLA_QK ** -0.5 * LOG2E
    o_cq = SWA_DQ + SWA_DKV

    feat = lax.dot_general(wfeat_ref[...], hb, (((1,), (1,)), ((), ())), preferred_element_type=F32)
    small = tok(o_small, LANE)
    small_ref[0] = small
    k_swa = tok(o_k, SWA_DKV)
    ckvn = _rms(tok(o_ckv, MLA_KV_RANK), kvlg_ref[...]).astype(BF16)
    cqn_t = _rms_cols(feat[o_cq:o_cq + MLA_Q_RANK], qlg_ref[...]).astype(BF16)
    qf_t = jnp.dot(wqt_ref[...], cqn_t, preferred_element_type=F32)

    def swa_queries(heads):
        scos_t, ssin_t = scost_ref[...], ssint_ref[...]
        for h in heads:
            xq = _rms_cols(feat[h * dh:(h + 1) * dh], sqg_ref[...])
            xq = xq * scos_t + _rotate_half_rows(xq) * ssin_t
            sq_ref[0, h * dh:(h + 1) * dh, :] = (xq * (dh ** -0.5 * LOG2E)).astype(sq_ref.dtype)

    def swa_keys_values():
        scos, ssin = scos_ref[...], ssin_ref[...]
        for h in range(SWA_KV_HEADS):
            vt = feat[SWA_DQ + h * dh:SWA_DQ + (h + 1) * dh]
            for u in range(tm // SWA_BLOCK):
                sv_ref[0, h, u, 0:dh, :] = vt[:, u * SWA_BLOCK:(u + 1) * SWA_BLOCK].astype(sv_ref.dtype)
                sv_ref[0, h, u, dh:, :] = jnp.ones((SWA_VT_ROWS - dh, SWA_BLOCK), sv_ref.dtype)
            xk = _rms(k_swa[:, h * dh:(h + 1) * dh], skg_ref[...])
            xk = xk * scos + _rotate_half(xk) * ssin
            sk_ref[0, :, h * dh:(h + 1) * dh] = xk.astype(sk_ref.dtype)

    def mla_queries(heads):
        mcos_t, msin_t = mcost_ref[...], msint_ref[...]
        for h in heads:
            qn = _rms_cols(qf_t[h * dn:(h + 1) * dn], qgn_ref[...])
            qr = _rms_cols(qf_t[nh * dn + h * dr:nh * dn + (h + 1) * dr], qgr_ref[...])
            qr = qr * mcos_t + _rotate_half_rows(qr) * msin_t
            mq_ref[0, h, 0:dn, :] = (qn * qscale).astype(mq_ref.dtype)
            mq_ref[0, h, dn:dn + dr, :] = (qr * qscale).astype(mq_ref.dtype)

    def mla_keys():
        kf = jnp.dot(ckvn, wk_ref[...], preferred_element_type=F32)
        kr0 = 2 * SSM_HEADS
        kr = _rms(small[:, kr0:kr0 + dr], kgr_ref[...])
        kr = kr * mcos_ref[...] + _rotate_half(kr) * msin_ref[...]
        for h in range(nh):
            kn = _rms(kf[:, h * dn:(h + 1) * dn], kgn_ref[...])
            mk_ref[0, h] = jnp.concatenate([kn, kr], axis=1).astype(mk_ref.dtype)

    def mla_values():
        vf_t = lax.dot_general(wvt_ref[...], ckvn, (((1,), (1,)), ((), ())), preferred_element_type=F32)
        for h in range(nh):
            mv_ref[0, h, 0, 0:dv, :] = vf_t[h * dv:(h + 1) * dv].astype(mv_ref.dtype)
            mv_ref[0, h, 0, dv:, :] = jnp.ones((MLA_VT_ROWS - dv, tm), mv_ref.dtype)

    half_q = SWA_Q_HEADS // 2
    half_c = SSM_CONV_DIM // 2
    xbc_ref[0, :, 0:half_c] = tok(0, half_c)
    swa_queries(range(0, half_q))
    xbc_ref[0, :, half_c:] = tok(half_c, SSM_CONV_DIM - half_c)
    swa_queries(range(half_q, SWA_Q_HEADS))
    z_ref[0] = tok(o_z, SSM_INNER)
    swa_keys_values()
    gates_ref[0, :, 0:d] = tok(o_gates, d)
    mla_queries(range(0, nh // 2))
    gates_ref[0, :, d:2 * d] = tok(o_gates + d, d)
    mla_queries(range(nh // 2, nh))
    gates_ref[0, :, 2 * d:] = tok(o_gates + 2 * d, d)
    mla_keys()
    mla_values()


def _in_projection(x_all, modsel, norm_g, w_in_l, w_uq, w_ukv, swa_q_g, swa_k_g, q_lat_g, kv_lat_g,
                   mla_q_g, mla_k_g, rope_swa, rope_mla, n_ctx_blocks):
    bsz, t, d = x_all.shape
    tm = TOKEN_BLOCK
    nh = MLA_HEADS
    w_tok, w_feat = _split_in_weights(w_in_l, d)
    wq_t = _head_major(w_uq, nh, (MLA_NOPE, MLA_ROPE)).T.astype(BF16)
    wkv = _head_major(w_ukv, nh, (MLA_NOPE, MLA_V))
    wk = wkv[:, :nh * MLA_NOPE].astype(BF16)
    wv_t = wkv[:, nh * MLA_NOPE:].T.astype(BF16)
    scos, ssin = rope_swa
    mcos, msin = rope_mla

    def cols(g):
        return jnp.broadcast_to(g[:, None], (g.shape[0], tm))

    def row(g):
        return g.reshape(1, -1)

    def tok(width):
        return pl.BlockSpec((1, tm, width), lambda b, i: (b, i, 0))

    def tok_table(width):
        return pl.BlockSpec((tm, width), lambda b, i: (i, 0))

    def feat_table(height):
        return pl.BlockSpec((height, tm), lambda b, i: (0, i))

    consts = [row(norm_g), w_tok, w_feat, wq_t, wk, wv_t,
              cols(swa_q_g), row(swa_k_g), cols(q_lat_g), row(kv_lat_g),
              cols(mla_q_g[:MLA_NOPE]), cols(mla_q_g[MLA_NOPE:]), row(mla_k_g[:MLA_NOPE]), row(mla_k_g[MLA_NOPE:])]
    nkb = tm // SWA_BLOCK
    return pl.pallas_call(
        _inproj_kernel,
        grid=(bsz, t // tm),
        in_specs=[tok(d),
                  pl.BlockSpec((1, 1, SUBLANE, d), lambda b, i: (b, jnp.where(i < n_ctx_blocks, 0, 1), 0, 0))]
                 + [_resident(a.shape) for a in consts]
                 + [tok_table(SWA_HEAD_DIM), tok_table(SWA_HEAD_DIM), feat_table(SWA_HEAD_DIM),
                    feat_table(SWA_HEAD_DIM), tok_table(MLA_ROPE), tok_table(MLA_ROPE), feat_table(MLA_ROPE),
                    feat_table(MLA_ROPE)],
        out_specs=[tok(SSM_CONV_DIM), tok(LANE), tok(SSM_INNER), tok(N_BRANCH * d),
                   pl.BlockSpec((1, SWA_DQ, tm), lambda b, i: (b, 0, i)),
                   tok(SWA_DKV),
                   pl.BlockSpec((1, SWA_KV_HEADS, nkb, SWA_VT_ROWS, SWA_BLOCK), lambda b, i: (b, 0, i, 0, 0)),
                   pl.BlockSpec((1, nh, MLA_QK, tm), lambda b, i: (b, 0, 0, i)),
                   pl.BlockSpec((1, nh, tm, MLA_QK), lambda b, i: (b, 0, i, 0)),
                   pl.BlockSpec((1, nh, 1, MLA_VT_ROWS, tm), lambda b, i: (b, 0, i, 0, 0))],
        out_shape=[jax.ShapeDtypeStruct((bsz, t, SSM_CONV_DIM), F32),
                   jax.ShapeDtypeStruct((bsz, t, LANE), F32),
                   jax.ShapeDtypeStruct((bsz, t, SSM_INNER), F32),
                   jax.ShapeDtypeStruct((bsz, t, N_BRANCH * d), F32),
                   jax.ShapeDtypeStruct((bsz, SWA_DQ, t), BF16),
                   jax.ShapeDtypeStruct((bsz, t, SWA_DKV), BF16),
                   jax.ShapeDtypeStruct((bsz, SWA_KV_HEADS, t // SWA_BLOCK, SWA_VT_ROWS, SWA_BLOCK), BF16),
                   jax.ShapeDtypeStruct((bsz, nh, MLA_QK, t), BF16),
                   jax.ShapeDtypeStruct((bsz, nh, t, MLA_QK), BF16),
                   jax.ShapeDtypeStruct((bsz, nh, t // tm, MLA_VT_ROWS, tm), BF16)],
        compiler_params=_cparams(2),
        name="in_projection",
    )(x_all, modsel, *consts, scos, ssin, scos.T, ssin.T, mcos, msin, mcos.T, msin.T)


def _ssd_scalars(small, dtb_ref, alog_ref):
    q = small.shape[0]
    lane = lax.broadcasted_iota(jnp.int32, (1, LANE), 1)
    raw = small + dtb_ref[...]
    dts = jnp.maximum(raw, 0.0) + jnp.log1p(jnp.exp(-jnp.abs(raw)))
    a = jnp.where(lane < 2 * SSM_HEADS, -jnp.exp(alog_ref[...]), 0.0)
    dta = dts * a
    ri = lax.broadcasted_iota(jnp.int32, (q, q), 0)
    ci = lax.broadcasted_iota(jnp.int32, (q, q), 1)
    tri = (ci <= ri).astype(BF16)
    parts = jnp.dot(tri, _split3(dta), preferred_element_type=F32)
    acs = parts[:, :LANE] + parts[:, LANE:2 * LANE] + parts[:, 2 * LANE:]
    ecs = acs - dta
    return dts, acs, ecs, dts.T, acs.T, ecs.T


def _split3(x):
    hi = x.astype(BF16)
    r1 = x - hi.astype(F32)
    mid = r1.astype(BF16)
    lo = (r1 - mid.astype(F32)).astype(BF16)
    return jnp.concatenate([hi, mid, lo], axis=1)


def _expand_matrix(first_row):
    r = lax.broadcasted_iota(jnp.int32, (3 * LANE, SSM_INNER), 0)
    c = lax.broadcasted_iota(jnp.int32, (3 * LANE, SSM_INNER), 1)
    return (c // SSM_HEAD_DIM + first_row == (r & (LANE - 1))).astype(BF16)


def _ssd_direction(backward, xs, bm, cm, scal, state_ref):
    dts, acs, ecs, dts_t, acs_t, ecs_t = scal
    q = xs.shape[0]
    base = SSM_HEADS if backward else 0
    expand = _expand_matrix(base)
    tot = acs[q - 1:q, :]
    if backward:
        dec_in = jnp.exp(tot - ecs)
        w_out = jnp.exp(ecs) * dts
        pos, pos_t = ecs, ecs_t
    else:
        dec_in = jnp.exp(acs)
        w_out = jnp.exp(tot - acs) * dts
        pos, pos_t = acs, acs_t
    stacked = jnp.concatenate([dec_in, w_out, jnp.broadcast_to(jnp.exp(tot), (SUBLANE, LANE))], axis=0)
    expanded = jnp.dot(_split3(stacked), expand, preferred_element_type=F32)
    dec_e, w_e, tot_e = expanded[0:q], expanded[q:2 * q], expanded[2 * q:2 * q + 1]
    xw = (xs * w_e).astype(BF16)
    xb = xs.astype(BF16)
    ri = lax.broadcasted_iota(jnp.int32, (q, q), 0)
    ci = lax.broadcasted_iota(jnp.int32, (q, q), 1)
    keep = (ci >= ri) if backward else (ci <= ri)
    gw = SSM_HPG * SSM_HEAD_DIM
    ys = []
    for g in range(SSM_GROUPS):
        b_g = bm[:, g * SSM_STATE:(g + 1) * SSM_STATE]
        c_g = cm[:, g * SSM_STATE:(g + 1) * SSM_STATE].astype(BF16)
        b_t = b_g.T.astype(BF16)
        cb = jnp.dot(c_g, b_t, preferred_element_type=F32)
        st = state_ref[g]
        y_off = jnp.dot(c_g, st.astype(BF16), preferred_element_type=F32) * dec_e[:, g * gw:(g + 1) * gw]
        state_ref[g] = st * tot_e[:, g * gw:(g + 1) * gw] + jnp.dot(
            b_t, xw[:, g * gw:(g + 1) * gw], preferred_element_type=F32)
        heads = []
        for hh in range(SSM_HPG):
            h = g * SSM_HPG + hh
            col = pos[:, base + h:base + h + 1]
            row = pos_t[base + h:base + h + 1, :]
            diff = (row - col) if backward else (col - row)
            seg = jnp.exp(jnp.where(keep, diff, -jnp.inf))
            m = (cb * seg * dts_t[base + h:base + h + 1, :]).astype(BF16)
            heads.append(jnp.dot(m, xb[:, h * SSM_HEAD_DIM:(h + 1) * SSM_HEAD_DIM],
                                 preferred_element_type=F32))
        ys.append(jnp.concatenate(heads, axis=1) + y_off)
    return jnp.concatenate(ys, axis=1)


def _ssd_fwd_kernel(xc_ref, xp_ref, xn_ref, small_ref, cw_ref, cb_ref, dtb_ref, alog_ref, dskip_ref,
                    y_ref, u_ref, state_ref, *, n_ctx_chunks, n_chunks):
    c = pl.program_id(1)

    @pl.when(c == 0)
    def _():
        state_ref[...] = jnp.zeros_like(state_ref)

    prev_ok = jnp.logical_and(c != 0, c != n_ctx_chunks)
    next_ok = jnp.logical_and(c != n_ctx_chunks - 1, c != n_chunks - 1)
    xp = jnp.where(prev_ok, xp_ref[0], 0.0)
    xn = jnp.where(next_ok, xn_ref[0], 0.0)
    xc = xc_ref[0]
    q = xc.shape[0]
    ext = jnp.concatenate([xp, xc, xn], axis=0)
    half = SSM_CONV // 2
    acc = jnp.zeros_like(xc) + cb_ref[...]
    for k in range(SSM_CONV):
        lo = HALO - half + k
        acc = acc + ext[lo:lo + q, :] * cw_ref[k:k + 1, :]
    u = _silu(acc)
    u_ref[0] = u.astype(u_ref.dtype)
    xs = u[:, :SSM_INNER]
    bm = u[:, SSM_INNER:SSM_INNER + SSM_BC]
    cm = u[:, SSM_INNER + SSM_BC:]
    scal = _ssd_scalars(small_ref[0], dtb_ref, alog_ref)
    y = _ssd_direction(False, xs, bm, cm, scal, state_ref)
    y_ref[0] = y + dskip_ref[...] * xs


def _ssd_bwd_kernel(u_ref, small_ref, z_ref, yf_ref, dtb_ref, alog_ref, ng_ref, o_ref, state_ref):
    @pl.when(pl.program_id(1) == 0)
    def _():
        state_ref[...] = jnp.zeros_like(state_ref)

    u = u_ref[0].astype(F32)
    xs = u[:, :SSM_INNER]
    bm = u[:, SSM_INNER:SSM_INNER + SSM_BC]
    cm = u[:, SSM_INNER + SSM_BC:]
    scal = _ssd_scalars(small_ref[0], dtb_ref, alog_ref)
    y = yf_ref[0] + _ssd_direction(True, xs, bm, cm, scal, state_ref)
    o_ref[0] = _rms(y * _silu(z_ref[0]), ng_ref[...]).astype(o_ref.dtype)


def _ssd_branch(xbc, small, z, conv_w, conv_b, dt_bias, a_log, d_skip, norm_g, n_ctx):
    bsz, t, _ = xbc.shape
    q = SSM_CHUNK
    n_chunks = t // q
    n_ctx_chunks = n_ctx // q
    hb = q // HALO
    n_halo = t // HALO
    pad32 = LANE - 2 * SSM_HEADS
    dtb = jnp.pad(dt_bias.reshape(1, -1), ((0, 0), (0, pad32)))
    alog = jnp.pad(a_log.reshape(1, -1), ((0, 0), (0, pad32)))
    dskip = jnp.repeat(d_skip, SSM_HEAD_DIM).reshape(1, SSM_INNER)
    state = pltpu.VMEM((SSM_GROUPS, SSM_STATE, SSM_HPG * SSM_HEAD_DIM), F32)

    def chunk(width):
        return pl.BlockSpec((1, q, width), lambda b, c: (b, c, 0))

    y_f, u = pl.pallas_call(
        functools.partial(_ssd_fwd_kernel, n_ctx_chunks=n_ctx_chunks, n_chunks=n_chunks),
        grid=(bsz, n_chunks),
        in_specs=[chunk(SSM_CONV_DIM),
                  pl.BlockSpec((1, HALO, SSM_CONV_DIM), lambda b, c: (b, jnp.maximum(c * hb - 1, 0), 0)),
                  pl.BlockSpec((1, HALO, SSM_CONV_DIM),
                               lambda b, c: (b, jnp.minimum((c + 1) * hb, n_halo - 1), 0)),
                  chunk(LANE),
                  _resident((SSM_CONV, SSM_CONV_DIM)), _resident((1, SSM_CONV_DIM)),
                  _resident((1, LANE)), _resident((1, LANE)), _resident((1, SSM_INNER))],
        out_specs=[chunk(SSM_INNER), chunk(SSM_CONV_DIM)],
        out_shape=[jax.ShapeDtypeStruct((bsz, t, SSM_INNER), F32),
                   jax.ShapeDtypeStruct((bsz, t, SSM_CONV_DIM), BF16)],
        scratch_shapes=[state],
        compiler_params=_cparams(2),
        name="ssd_forward",
    )(xbc, xbc, xbc, small, conv_w, conv_b.reshape(1, -1), dtb, alog, dskip)

    def rchunk(width):
        return pl.BlockSpec(
            (1, q, width),
            lambda b, s: (b, jnp.where(s < n_ctx_chunks, n_ctx_chunks - 1 - s,
                                       n_chunks + n_ctx_chunks - 1 - s), 0))

    return pl.pallas_call(
        _ssd_bwd_kernel,
        grid=(bsz, n_chunks),
        in_specs=[rchunk(SSM_CONV_DIM), rchunk(LANE), rchunk(SSM_INNER), rchunk(SSM_INNER),
                  _resident((1, LANE)), _resident((1, LANE)), _resident((1, SSM_INNER))],
        out_specs=rchunk(SSM_INNER),
        out_shape=jax.ShapeDtypeStruct((bsz, t, SSM_INNER), BF16),
        scratch_shapes=[state],
        compiler_params=_cparams(2),
        name="ssd_backward",
    )(u, small, z, y_f, dtb, alog, norm_g.reshape(1, -1))


def _rope_tables(n_lat, n_ctx, rot_dim):
    n_freq = rot_dim // 4
    inv = jnp.power(ROPE_BASE, -jnp.arange(n_freq, dtype=F32) / n_freq)
    t = jnp.arange(n_lat)
    r = (t // GRID_W).astype(F32)[:, None] * inv
    col = (t % GRID_W).astype(F32)[:, None] * inv
    cos2 = jnp.concatenate([jnp.cos(r), jnp.cos(r), jnp.cos(col), jnp.cos(col)], axis=1)
    sin2 = jnp.concatenate([-jnp.sin(r), jnp.sin(r), -jnp.sin(col), jnp.sin(col)], axis=1)
    cos2 = jnp.concatenate([jnp.ones((n_ctx, rot_dim), F32), cos2], axis=0)
    sin2 = jnp.concatenate([jnp.zeros((n_ctx, rot_dim), F32), sin2], axis=0)
    return cos2, sin2


def _rotate_half(x):
    f = x.shape[-1] // 4
    return jnp.concatenate([x[:, f:2 * f], x[:, :f], x[:, 3 * f:], x[:, 2 * f:3 * f]], axis=1)


SWA_BLOCKS_PER_STEP = 2


def _swa_kernel(sink_ref, qt_ref, k_ref, vt_ref, o_ref, *, first_block, n_ctx, t):
    for j in range(SWA_BLOCKS_PER_STEP):
        _swa_block(sink_ref, qt_ref, k_ref, vt_ref, o_ref, j, first_block=first_block, n_ctx=n_ctx, t=t)


def _swa_block(sink_ref, qt_ref, k_ref, vt_ref, o_ref, j, *, first_block, n_ctx, t):
    hk = pl.program_id(1)
    blk = pl.program_id(2) * SWA_BLOCKS_PER_STEP + j + first_block
    bq, dh = SWA_BLOCK, SWA_HEAD_DIM
    win = 3 * bq
    ncb = n_ctx // bq
    nq = SWA_GRP * bq
    qt = qt_ref[0, :, j * bq:(j + 1) * bq]
    q4t =jnp.concatenate([qt[g * dh:(g + 1) * dh, :] for g in range(SWA_GRP)], axis=1)
    start = pl.multiple_of(jnp.clip((blk - 1) * bq, 0, t - win), bq)
    b0 = start // bq
    s_w = jnp.dot(k_ref[0, pl.ds(start, win), :], q4t, preferred_element_type=F32)
    s_c = jnp.dot(k_ref[0, 0:n_ctx, :], q4t, preferred_element_type=F32)
    rows = lax.broadcasted_iota(jnp.int32, (win, nq), 0)
    cols = lax.broadcasted_iota(jnp.int32, (win, nq), 1)
    qpos = (blk - ncb) * bq + (cols & (bq - 1))
    kpos = start - n_ctx + rows
    ok = (jnp.abs(kpos - qpos) <= SWA_WINDOW) & (kpos >= 0) & (blk >= ncb)
    s_w = jnp.where(ok, s_w, -jnp.inf)
    c1 = lax.broadcasted_iota(jnp.int32, (1, nq), 1)
    sink = jnp.zeros((1, nq), F32)
    for g in range(SWA_GRP):
        sink = jnp.where(c1 // bq == g, sink_ref[hk * SWA_GRP + g], sink)
    sink = sink * LOG2E
    m = jnp.maximum(jnp.maximum(jnp.max(s_w, axis=0, keepdims=True),
                                jnp.max(s_c, axis=0, keepdims=True)), sink)
    p_w = jnp.exp2(s_w - m).astype(BF16)
    p_c = jnp.exp2(s_c - m).astype(BF16)

    def weighted_values(first, nblk, p):
        acc = None
        u = 0
        while u < nblk:
            n = 2 if u + 1 < nblk else 1
            vt = vt_ref[0, 0, first + u]
            if n == 2:
                vt = jnp.concatenate([vt, vt_ref[0, 0, first + u + 1]], axis=1)
            term = jnp.dot(vt, p[u * bq:(u + n) * bq], preferred_element_type=F32)
            acc = term if acc is None else acc + term
            u += n
        return acc

    acc = weighted_values(b0, win // bq, p_w) + weighted_values(0, ncb, p_c)
    o = acc[0:dh] / (acc[dh:dh + 1] + jnp.exp2(sink - m))
    for g in range(SWA_GRP):
        o_ref[0, j * bq:(j + 1) * bq, g * dh:(g + 1) * dh] = o[:, g * bq:(g + 1) * bq].T.astype(o_ref.dtype)


def _swa_attention(qt, k, vt, sink, n_ctx, first_block):
    bsz, dq, t = qt.shape
    nb = SWA_BLOCKS_PER_STEP
    bq = SWA_BLOCK
    gw = SWA_GRP * SWA_HEAD_DIM
    assert first_block % nb == 0 and (t // bq) % nb == 0
    nblk = (t // bq - first_block) // nb
    first = first_block // nb
    return pl.pallas_call(
        functools.partial(_swa_kernel, first_block=first_block, n_ctx=n_ctx, t=t),
        grid=(bsz, SWA_KV_HEADS, nblk),
        in_specs=[pl.BlockSpec(memory_space=pltpu.SMEM),
                  pl.BlockSpec((1, gw, nb * bq), lambda b, h, n: (b, h, n + first)),
                  pl.BlockSpec((1, t, SWA_HEAD_DIM), lambda b, h, n: (b, 0, h)),
                  pl.BlockSpec((1, 1, t // bq, SWA_VT_ROWS, bq), lambda b, h, n: (b, h, 0, 0, 0))],
        out_specs=pl.BlockSpec((1, nb * bq, gw), lambda b, h, n: (b, n + first, h)),
        out_shape=jax.ShapeDtypeStruct((bsz, t, dq), BF16),
        compiler_params=_cparams(3),
        name="swa_attention",
    )(sink, qt, k, vt)


def _rms_cols(x, g):
    return x * lax.rsqrt(jnp.mean(x * x, axis=0, keepdims=True) + EPS) * g


def _rotate_half_rows(x):
    f = x.shape[0] // 4
    return jnp.concatenate([x[f:2 * f], x[:f], x[3 * f:], x[2 * f:3 * f]], axis=0)


MLA_HEADS_PER_STEP = 2
MLA_KEY_SUBBLOCKS = 1
MLA_UNROLL_CHUNKS = 40


def _mla_kernel(qt_ref, k_ref, vt_ref, o_ref, s_scr, p_scr, a_scr, mx_scr, acc_scr, m_scr, *,
                first_block, n_ctx_blocks, n_key_blocks, sub):
    blk = pl.program_id(2) + first_block
    heads = qt_ref.shape[1]
    kb = vt_ref.shape[4]
    tk = sub * kb
    n_chunks = n_key_blocks // sub

    def softmax_update(c, s, s_max):
        m_prev = m_scr[c]
        m_new = jnp.maximum(m_prev, s_max)
        m_scr[c] = m_new
        return jnp.exp2(m_prev - m_new), jnp.exp2(s - m_new).astype(BF16)

    def value_update(c, alpha, pb, blk0, nsub):
        pv = jnp.dot(vt_ref[0, c, blk0], pb[0:kb], preferred_element_type=F32)
        for u in range(1, nsub):
            pv = pv + jnp.dot(vt_ref[0, c, blk0 + u], pb[u * kb:(u + 1) * kb], preferred_element_type=F32)
        acc_scr[c] = alpha * acc_scr[c] + pv

    def stage_scores(c, t, slot):
        row0 = t * tk if isinstance(t, int) else pl.multiple_of(t * tk, tk)
        s = jnp.dot(k_ref[0, c, pl.ds(row0, tk), :], qt_ref[0, c], preferred_element_type=F32)
        s_scr[c, slot] = s
        mx_scr[c, slot] = jnp.max(s, axis=0, keepdims=True)

    def stage_softmax(c, slot):
        alpha, pb = softmax_update(c, s_scr[c, slot], mx_scr[c, slot])
        a_scr[c, slot] = alpha
        p_scr[c, slot] = pb

    def stage_values(c, t, slot):
        value_update(c, a_scr[c, slot], p_scr[c, slot], t * sub, sub)

    def tick(t, parity):
        static = isinstance(t, int)
        for c in range(heads):
            if not static or t < n_chunks:
                stage_scores(c, t, parity)
            if not static or 1 <= t <= n_chunks:
                stage_softmax(c, 1 - parity)
            if not static or 2 <= t <= n_chunks + 1:
                stage_values(c, t - 2, parity)

    def reset():
        m_scr[...] = jnp.full(m_scr.shape, -jnp.inf, F32)
        acc_scr[...] = jnp.zeros(acc_scr.shape, F32)

    def finish():
        for c in range(heads):
            acc = acc_scr[c]
            o = acc[0:MLA_V] / acc[MLA_V:MLA_V + 1]
            o_ref[0, :, c * MLA_V:(c + 1) * MLA_V] = o.T.astype(o_ref.dtype)

    @pl.when(blk < n_ctx_blocks)
    def _():
        reset()
        for c in range(heads):
            s = jnp.dot(k_ref[0, c, 0:n_ctx_blocks * kb, :], qt_ref[0, c], preferred_element_type=F32)
            alpha, pb = softmax_update(c, s, jnp.max(s, axis=0, keepdims=True))
            value_update(c, alpha, pb, 0, n_ctx_blocks)
        finish()

    @pl.when(blk >= n_ctx_blocks)
    def _():
        reset()
        n_pairs = 0 if n_chunks <= MLA_UNROLL_CHUNKS else (n_chunks - 2) // 2
        for t in range(0, min(2, n_chunks + 2)):
            tick(t, t % 2)

        def body(i, carry):
            t0 = 2 + 2 * i
            tick(t0, 0)
            tick(t0 + 1, 1)
            return carry

        lax.fori_loop(0, n_pairs, body, 0)
        for t in range(2 + 2 * n_pairs, n_chunks + 2):
            tick(t, t % 2)
        finish()


def _mla_attention(qt, k, vt, n_ctx, first_block):
    bsz, nh, dqk, t = qt.shape
    tq = TOKEN_BLOCK
    kb = vt.shape[-1]
    n_key_blocks = t // kb
    sub = MLA_KEY_SUBBLOCKS if n_key_blocks % MLA_KEY_SUBBLOCKS == 0 else 1
    g = MLA_HEADS_PER_STEP
    nblk = t // tq - first_block
    return pl.pallas_call(
        functools.partial(_mla_kernel, first_block=first_block, n_ctx_blocks=n_ctx // kb,
                          n_key_blocks=n_key_blocks, sub=sub),
        grid=(bsz, nh // g, nblk),
        in_specs=[pl.BlockSpec((1, g, dqk, tq), lambda b, h, i: (b, h, 0, i + first_block)),
                  pl.BlockSpec((1, g, t, dqk), lambda b, h, i: (b, h, 0, 0)),
                  pl.BlockSpec((1, g, n_key_blocks, MLA_VT_ROWS, kb), lambda b, h, i: (b, h, 0, 0, 0))],
        out_specs=pl.BlockSpec((1, tq, g * MLA_V), lambda b, h, i: (b, i + first_block, h)),
        out_shape=jax.ShapeDtypeStruct((bsz, t, nh * MLA_V), BF16),
        scratch_shapes=[pltpu.VMEM((g, 2, sub * kb, tq), F32),
                        pltpu.VMEM((g, 2, sub * kb, tq), BF16),
                        pltpu.VMEM((g, 2, 1, tq), F32),
                        pltpu.VMEM((g, 2, 1, tq), F32),
                        pltpu.VMEM((g, MLA_VT_ROWS, tq), F32),
                        pltpu.VMEM((g, 1, tq), F32)],
        compiler_params=_cparams(3),
        name="mla_attention",
    )(qt, k, vt)


def _ffn_chunks(hidden):
    step = 512
    return tuple((lo, min(lo + step, hidden)) for lo in range(0, hidden, step))


def _merge_ffn_kernel(x_ref, mod_ref, gates_ref, ys_ref, yw_ref, ym_ref, wps_ref, wpw_ref, wpm_ref,
                      wo_ref, g2_ref, wfi_ref, wfo_ref, o_ref):
    d = x_ref.shape[-1]
    hidden = wfo_ref.shape[0]
    mod = mod_ref[0, 0]
    gt1, sh2, sc2, gt2 = mod[2:3], mod[3:4], mod[4:5], mod[5:6]
    gates = jax.nn.sigmoid(gates_ref[0])
    merged = (gates[:, 0:d] * jnp.dot(ys_ref[0], wps_ref[...], preferred_element_type=F32)
              + gates[:, d:2 * d] * jnp.dot(yw_ref[0], wpw_ref[...], preferred_element_type=F32)
              + gates[:, 2 * d:3 * d] * jnp.dot(ym_ref[0], wpm_ref[...], preferred_element_type=F32))
    x1 = x_ref[0] + gt1 * jnp.dot(merged.astype(BF16), wo_ref[...], preferred_element_type=F32)
    hb = (_rms(x1, g2_ref[...]) * (1.0 + sc2) + sh2).astype(BF16)
    acc = jnp.zeros_like(x1)
    for lo, hi in _ffn_chunks(hidden):
        gate = jnp.dot(hb, wfi_ref[:, lo:hi], preferred_element_type=F32)
        up = jnp.dot(hb, wfi_ref[:, hidden + lo:hidden + hi], preferred_element_type=F32)
        acc = acc + jnp.dot((_silu(gate) * up).astype(BF16), wfo_ref[lo:hi, :], preferred_element_type=F32)
    o_ref[0] = x1 + gt2 * acc


def _merge_ffn(x_all, modsel, gates, y_ssm, y_swa, y_mla, wps, wpw, wpm, wo, norm2_g, wfi, wfo,
               n_ctx_blocks, first_block):
    bsz, t, d = x_all.shape
    tm = TOKEN_BLOCK
    nblk = t // tm - first_block

    def tok(width):
        return pl.BlockSpec((1, tm, width), lambda b, i: (b, i + first_block, 0))

    return pl.pallas_call(
        _merge_ffn_kernel,
        grid=(bsz, nblk),
        in_specs=[tok(d),
                  pl.BlockSpec((1, 1, SUBLANE, d),
                               lambda b, i: (b, jnp.where(i + first_block < n_ctx_blocks, 0, 1), 0, 0)),
                  tok(N_BRANCH * d), tok(y_ssm.shape[-1]), tok(y_swa.shape[-1]), tok(y_mla.shape[-1]),
                  _resident(wps.shape), _resident(wpw.shape), _resident(wpm.shape), _resident(wo.shape),
                  _resident((1, d)), _resident(wfi.shape), _resident(wfo.shape)],
        out_specs=pl.BlockSpec((1, tm, d), lambda b, i: (b, i, 0)),
        out_shape=jax.ShapeDtypeStruct((bsz, nblk * tm, d), F32),
        compiler_params=_cparams(2),
        name="merge_ffn",
    )(x_all, modsel, gates, y_ssm, y_swa, y_mla, wps, wpw, wpm, wo, norm2_g.reshape(1, d), wfi, wfo)


def _head_major(w, n_heads, parts):
    k = w.shape[0]
    w = w.reshape(k, n_heads, sum(parts))
    out, lo = [], 0
    for p in parts:
        out.append(w[:, :, lo:lo + p].reshape(k, n_heads * p))
        lo += p
    return jnp.concatenate(out, axis=1)


def kernel(x, c, ctx, c_ctx, w_mod, b_mod, norm1_g, norm2_g, w_in, ssm_conv_w, ssm_conv_b, ssm_dt_bias,
           ssm_a_log, ssm_d, ssm_norm_g, swa_q_norm_g, swa_k_norm_g, swa_sink, mla_q_lat_g, mla_kv_lat_g,
           w_mla_uq, w_mla_ukv, mla_q_norm_g, mla_k_norm_g, w_p_ssm, w_p_swa, w_p_mla, w_out, w_ffn_in,
           w_ffn_out):
    bsz, n_lat, d = x.shape
    n_ctx = ctx.shape[1]
    depth = w_mod.shape[0]
    assert n_ctx % TOKEN_BLOCK == 0 and n_lat % TOKEN_BLOCK == 0 and n_lat % GRID_W == 0
    assert bsz + 1 <= SUBLANE
    n_ctx_blocks = n_ctx // TOKEN_BLOCK

    cvec = jnp.concatenate([c, c_ctx[None], jnp.zeros((SUBLANE - bsz - 1, d), F32)], axis=0)
    mods = _modulation(cvec, w_mod, b_mod).reshape(depth, SUBLANE, 6, d)
    rope_swa = _rope_tables(n_lat, n_ctx, SWA_HEAD_DIM)
    rope_mla = _rope_tables(n_lat, n_ctx, MLA_ROPE)

    x_all = jnp.concatenate([ctx, x], axis=1)
    for i in range(depth):
        last = i == depth - 1
        ctx_mod = jnp.broadcast_to(mods[i, bsz][None], (bsz, 6, d))
        modsel = jnp.pad(jnp.stack([ctx_mod, mods[i, :bsz]], axis=1), ((0, 0), (0, 0), (0, SUBLANE - 6), (0, 0)))

        xbc, small, z, gates, qs, ks, vs, qm, km, vm = _in_projection(
            x_all, modsel, norm1_g[i], w_in[i], w_mla_uq[i], w_mla_ukv[i], swa_q_norm_g[i], swa_k_norm_g[i],
            mla_q_lat_g[i], mla_kv_lat_g[i], mla_q_norm_g[i], mla_k_norm_g[i], rope_swa, rope_mla, n_ctx_blocks)

        y_ssm = _ssd_branch(xbc, small, z, ssm_conv_w[i], ssm_conv_b[i], ssm_dt_bias[i], ssm_a_log[i],
                            ssm_d[i], ssm_norm_g[i], n_ctx)
        y_swa = _swa_attention(qs, ks, vs, swa_sink[i], n_ctx, n_ctx // SWA_BLOCK if last else 0)
        first_tok = n_ctx_blocks if last else 0
        y_mla = _mla_attention(qm, km, vm, n_ctx, first_tok)

        x_all = _merge_ffn(x_all, modsel, gates, y_ssm, y_swa, y_mla,
                           w_p_ssm[i].astype(BF16), w_p_swa[i].astype(BF16), w_p_mla[i].astype(BF16),
                           w_out[i].astype(BF16), norm2_g[i], w_ffn_in[i].astype(BF16),
                           w_ffn_out[i].astype(BF16), n_ctx_blocks, first_tok)
    return x_all
```

```python
import functools
import math

import jax
import jax.numpy as jnp
from jax import lax
from jax.experimental import pallas as pl
from jax.experimental.pallas import tpu as pltpu

F32 = jnp.float32
BF16 = jnp.bfloat16
HIGHEST = lax.Precision.HIGHEST

EPS = 1e-6
ROPE_BASE = 10000.0
GRID_W = 64

SSM_HEADS = 16
SSM_HEAD_DIM = 64
SSM_INNER = SSM_HEADS * SSM_HEAD_DIM
SSM_GROUPS = 2
SSM_STATE = 128
SSM_CONV = 5
SSM_CHUNK = 128
SSM_BC = SSM_GROUPS * SSM_STATE
SSM_CONV_DIM = SSM_INNER + 2 * SSM_BC
SSM_HPG = SSM_HEADS // SSM_GROUPS

SWA_Q_HEADS = 8
SWA_KV_HEADS = 2
SWA_HEAD_DIM = 128
SWA_WINDOW = 128
SWA_BLOCK = 128
SWA_GRP = SWA_Q_HEADS // SWA_KV_HEADS

MLA_HEADS = 8
MLA_Q_RANK = 384
MLA_KV_RANK = 256
MLA_NOPE = 128
MLA_ROPE = 64
MLA_QK = MLA_NOPE + MLA_ROPE
MLA_V = 128
BF16_SUBLANES = 16
MLA_VT_ROWS = MLA_V + BF16_SUBLANES

N_BRANCH = 3
LANE = 128
SUBLANE = 8
HALO = SUBLANE
TOKEN_BLOCK = 256
VMEM_LIMIT = 56 * 1024 * 1024


def _cparams(n_axes):
    return pltpu.CompilerParams(
        dimension_semantics=("arbitrary",) * n_axes, vmem_limit_bytes=VMEM_LIMIT)


def _resident(shape):
    nd = len(shape)
    return pl.BlockSpec(shape, lambda *_: (0,) * nd, pipeline_mode=pl.Buffered(1))


def _rms(x, g):
    return x * lax.rsqrt(jnp.mean(x * x, axis=-1, keepdims=True) + EPS) * g


def _silu(x):
    return x * jax.nn.sigmoid(x)


def _xdot(a, b):
    return jnp.dot(a, b, precision=HIGHEST, preferred_element_type=F32)


def _mod_kernel(c_ref, w_ref, b_ref, o_ref):
    o_ref[0] = _xdot(_silu(c_ref[...]), w_ref[0]) + b_ref[0]


def _modulation(cvec, w_mod, b_mod):
    depth, d, d6 = w_mod.shape
    rows = cvec.shape[0]
    return pl.pallas_call(
        _mod_kernel,
        grid=(depth, d6 // d),
        in_specs=[pl.BlockSpec((rows, d), lambda i, j: (0, 0)),
                  pl.BlockSpec((1, d, d), lambda i, j: (i, 0, j)),
                  pl.BlockSpec((1, 1, d), lambda i, j: (i, 0, j))],
        out_specs=pl.BlockSpec((1, rows, d), lambda i, j: (i, 0, j)),
        out_shape=jax.ShapeDtypeStruct((depth, rows, d6), F32),
        compiler_params=_cparams(2),
        name="modulation",
    )(cvec, w_mod, b_mod.reshape(depth, 1, d6))


SWA_DQ = SWA_Q_HEADS * SWA_HEAD_DIM
SWA_DKV = SWA_KV_HEADS * SWA_HEAD_DIM
SWA_VT_ROWS = SWA_HEAD_DIM + BF16_SUBLANES
LOG2E = math.log2(math.e)


def _split_in_weights(w_in_l, d):
    o = [0]
    for w in (SSM_CONV_DIM, 2 * SSM_HEADS, SWA_DKV, SWA_DKV, MLA_KV_RANK, MLA_ROPE, SSM_INNER, SWA_DQ,
              MLA_Q_RANK, N_BRANCH * d):
        o.append(o[-1] + w)
    xbc, dt, k, v, ckv, kr, z, q, cq, gates = (w_in_l[:, o[i]:o[i + 1]] for i in range(10))
    pad = jnp.zeros((d, LANE - 2 * SSM_HEADS - MLA_ROPE), w_in_l.dtype)
    w_tok = jnp.concatenate([xbc, dt, kr, pad, k, ckv, z, gates], axis=1).astype(BF16)
    w_feat = jnp.concatenate([q, v, cq], axis=1).T.astype(BF16)
    return w_tok, w_feat


def _latent_first(i, *, n_blocks, n_ctx_blocks):
    return jnp.where(i < n_ctx_blocks, n_blocks - n_ctx_blocks + i, i - n_ctx_blocks)


def _inproj_kernel(x_ref, mod_ref, g_ref, wtok_ref, wfeat_ref, wqt_ref, wk_ref, wvt_ref,
                   sqg_ref, skg_ref, qlg_ref, kvlg_ref, qgn_ref, qgr_ref, kgn_ref, kgr_ref,
                   scos_ref, ssin_ref, scost_ref, ssint_ref, mcos_ref, msin_ref, mcost_ref, msint_ref,
                   xbc_ref, small_ref, z_ref, gates_ref, sq_ref, sk_ref, sv_ref, mq_ref, mk_ref, mv_ref):
    d = x_ref.shape[-1]
    tm = x_ref.shape[1]
    mod = mod_ref[0, 0]
    hb = (_rms(x_ref[0], g_ref[...]) * (1.0 + mod[1:2]) + mod[0:1]).astype(BF16)

    def tok(lo, width):
        return jnp.dot(hb, wtok_ref[:, lo:lo + width], preferred_element_type=F32)

    o_small = SSM_CONV_DIM
    o_k = o_small + LANE
    o_ckv = o_k + SWA_DKV
    o_z = o_ckv + MLA_KV_RANK
    o_gates = o_z + SSM_INNER
    dh = SWA_HEAD_DIM
    nh, dn, dr, dv = MLA_HEADS, MLA_NOPE, MLA_ROPE, MLA_V
    qscale = MLA_QK ** -0.5 * LOG2E
    o_cq = SWA_DQ + SWA_DKV

    feat = lax.dot_general(wfeat_ref[...], hb, (((1,), (1,)), ((), ())), preferred_element_type=F32)
    small = tok(o_small, LANE)
    small_ref[0] = small
    k_swa = tok(o_k, SWA_DKV)
    ckvn = _rms(tok(o_ckv, MLA_KV_RANK), kvlg_ref[...]).astype(BF16)
    cqn_t = _rms_cols(feat[o_cq:o_cq + MLA_Q_RANK], qlg_ref[...]).astype(BF16)
    qf_t = jnp.dot(wqt_ref[...], cqn_t, preferred_element_type=F32)

    def swa_queries(heads):
        scos_t, ssin_t = scost_ref[...], ssint_ref[...]
        for h in heads:
            xq = _rms_cols(feat[h * dh:(h + 1) * dh], sqg_ref[...])
            xq = xq * scos_t + _rotate_half_rows(xq) * ssin_t
            sq_ref[0, h * dh:(h + 1) * dh, :] = (xq * (dh ** -0.5 * LOG2E)).astype(sq_ref.dtype)

    def swa_keys_values():
        scos, ssin = scos_ref[...], ssin_ref[...]
        for h in range(SWA_KV_HEADS):
            vt = feat[SWA_DQ + h * dh:SWA_DQ + (h + 1) * dh]
            for u in range(tm // SWA_BLOCK):
                sv_ref[0, h, u, 0:dh, :] = vt[:, u * SWA_BLOCK:(u + 1) * SWA_BLOCK].astype(sv_ref.dtype)
                sv_ref[0, h, u, dh:, :] = jnp.ones((SWA_VT_ROWS - dh, SWA_BLOCK), sv_ref.dtype)
            xk = _rms(k_swa[:, h * dh:(h + 1) * dh], skg_ref[...])
            xk = xk * scos + _rotate_half(xk) * ssin
            sk_ref[0, :, h * dh:(h + 1) * dh] = xk.astype(sk_ref.dtype)

    def mla_queries(heads):
        mcos_t, msin_t = mcost_ref[...], msint_ref[...]
        for h in heads:
            qn = _rms_cols(qf_t[h * dn:(h + 1) * dn], qgn_ref[...])
            qr = _rms_cols(qf_t[nh * dn + h * dr:nh * dn + (h + 1) * dr], qgr_ref[...])
            qr = qr * mcos_t + _rotate_half_rows(qr) * msin_t
            mq_ref[0, h, 0:dn, :] = (qn * qscale).astype(mq_ref.dtype)
            mq_ref[0, h, dn:dn + dr, :] = (qr * qscale).astype(mq_ref.dtype)

    def mla_keys():
        kf = jnp.dot(ckvn, wk_ref[...], preferred_element_type=F32)
        kr0 = 2 * SSM_HEADS
        kr = _rms(small[:, kr0:kr0 + dr], kgr_ref[...])
        kr = kr * mcos_ref[...] + _rotate_half(kr) * msin_ref[...]
        for h in range(nh):
            kn = _rms(kf[:, h * dn:(h + 1) * dn], kgn_ref[...])
            mk_ref[0, h] = jnp.concatenate([kn, kr], axis=1).astype(mk_ref.dtype)

    def mla_values():
        vf_t = lax.dot_general(wvt_ref[...], ckvn, (((1,), (1,)), ((), ())), preferred_element_type=F32)
        for h in range(nh):
            mv_ref[0, h, 0, 0:dv, :] = vf_t[h * dv:(h + 1) * dv].astype(mv_ref.dtype)
            mv_ref[0, h, 0, dv:, :] = jnp.ones((MLA_VT_ROWS - dv, tm), mv_ref.dtype)

    half_q = SWA_Q_HEADS // 2
    half_c = SSM_CONV_DIM // 2
    xbc_ref[0, :, 0:half_c] = tok(0, half_c)
    swa_queries(range(0, half_q))
    xbc_ref[0, :, half_c:] = tok(half_c, SSM_CONV_DIM - half_c)
    swa_queries(range(half_q, SWA_Q_HEADS))
    z_ref[0] = tok(o_z, SSM_INNER)
    swa_keys_values()
    gates_ref[0, :, 0:d] = tok(o_gates, d)
    mla_queries(range(0, nh // 2))
    gates_ref[0, :, d:2 * d] = tok(o_gates + d, d)
    mla_queries(range(nh // 2, nh))
    gates_ref[0, :, 2 * d:] = tok(o_gates + 2 * d, d)
    mla_keys()
    mla_values()


def _in_projection(x_all, modsel, norm_g, w_in_l, w_uq, w_ukv, swa_q_g, swa_k_g, q_lat_g, kv_lat_g,
                   mla_q_g, mla_k_g, rope_swa, rope_mla, n_ctx_blocks):
    bsz, t, d = x_all.shape
    tm = TOKEN_BLOCK
    nh = MLA_HEADS
    w_tok, w_feat = _split_in_weights(w_in_l, d)
    wq_t = _head_major(w_uq, nh, (MLA_NOPE, MLA_ROPE)).T.astype(BF16)
    wkv = _head_major(w_ukv, nh, (MLA_NOPE, MLA_V))
    wk = wkv[:, :nh * MLA_NOPE].astype(BF16)
    wv_t = wkv[:, nh * MLA_NOPE:].T.astype(BF16)
    scos, ssin = rope_swa
    mcos, msin = rope_mla

    def cols(g):
        return jnp.broadcast_to(g[:, None], (g.shape[0], tm))

    def row(g):
        return g.reshape(1, -1)

    def tok(width):
        return pl.BlockSpec((1, tm, width), lambda b, i: (b, i, 0))

    def tok_table(width):
        return pl.BlockSpec((tm, width), lambda b, i: (i, 0))

    def feat_table(height):
        return pl.BlockSpec((height, tm), lambda b, i: (0, i))

    consts = [row(norm_g), w_tok, w_feat, wq_t, wk, wv_t,
              cols(swa_q_g), row(swa_k_g), cols(q_lat_g), row(kv_lat_g),
              cols(mla_q_g[:MLA_NOPE]), cols(mla_q_g[MLA_NOPE:]), row(mla_k_g[:MLA_NOPE]), row(mla_k_g[MLA_NOPE:])]
    nkb = tm // SWA_BLOCK
    mla_pos = functools.partial(_latent_first, n_blocks=t // tm, n_ctx_blocks=n_ctx_blocks)
    return pl.pallas_call(
        _inproj_kernel,
        grid=(bsz, t // tm),
        in_specs=[tok(d),
                  pl.BlockSpec((1, 1, SUBLANE, d), lambda b, i: (b, jnp.where(i < n_ctx_blocks, 0, 1), 0, 0))]
                 + [_resident(a.shape) for a in consts]
                 + [tok_table(SWA_HEAD_DIM), tok_table(SWA_HEAD_DIM), feat_table(SWA_HEAD_DIM),
                    feat_table(SWA_HEAD_DIM), tok_table(MLA_ROPE), tok_table(MLA_ROPE), feat_table(MLA_ROPE),
                    feat_table(MLA_ROPE)],
        out_specs=[tok(SSM_CONV_DIM), tok(LANE), tok(SSM_INNER), tok(N_BRANCH * d),
                   pl.BlockSpec((1, SWA_DQ, tm), lambda b, i: (b, 0, i)),
                   tok(SWA_DKV),
                   pl.BlockSpec((1, SWA_KV_HEADS, nkb, SWA_VT_ROWS, SWA_BLOCK), lambda b, i: (b, 0, i, 0, 0)),
                   pl.BlockSpec((1, nh, MLA_QK, tm), lambda b, i: (b, 0, 0, mla_pos(i))),
                   pl.BlockSpec((1, nh, tm, MLA_QK), lambda b, i: (b, 0, mla_pos(i), 0)),
                   pl.BlockSpec((1, nh, 1, MLA_VT_ROWS, tm), lambda b, i: (b, 0, mla_pos(i), 0, 0))],
        out_shape=[jax.ShapeDtypeStruct((bsz, t, SSM_CONV_DIM), F32),
                   jax.ShapeDtypeStruct((bsz, t, LANE), F32),
                   jax.ShapeDtypeStruct((bsz, t, SSM_INNER), F32),
                   jax.ShapeDtypeStruct((bsz, t, N_BRANCH * d), F32),
                   jax.ShapeDtypeStruct((bsz, SWA_DQ, t), BF16),
                   jax.ShapeDtypeStruct((bsz, t, SWA_DKV), BF16),
                   jax.ShapeDtypeStruct((bsz, SWA_KV_HEADS, t // SWA_BLOCK, SWA_VT_ROWS, SWA_BLOCK), BF16),
                   jax.ShapeDtypeStruct((bsz, nh, MLA_QK, t), BF16),
                   jax.ShapeDtypeStruct((bsz, nh, t, MLA_QK), BF16),
                   jax.ShapeDtypeStruct((bsz, nh, t // tm, MLA_VT_ROWS, tm), BF16)],
        compiler_params=_cparams(2),
        name="in_projection",
    )(x_all, modsel, *consts, scos, ssin, scos.T, ssin.T, mcos, msin, mcos.T, msin.T)


def _ssd_scalars(small, dtb_ref, alog_ref):
    q = small.shape[0]
    lane = lax.broadcasted_iota(jnp.int32, (1, LANE), 1)
    raw = small + dtb_ref[...]
    dts = jnp.maximum(raw, 0.0) + jnp.log1p(jnp.exp(-jnp.abs(raw)))
    a = jnp.where(lane < 2 * SSM_HEADS, -jnp.exp(alog_ref[...]), 0.0)
    dta = dts * a
    ri = lax.broadcasted_iota(jnp.int32, (q, q), 0)
    ci = lax.broadcasted_iota(jnp.int32, (q, q), 1)
    tri = (ci <= ri).astype(BF16)
    parts = jnp.dot(tri, _split3(dta), preferred_element_type=F32)
    acs = parts[:, :LANE] + parts[:, LANE:2 * LANE] + parts[:, 2 * LANE:]
    ecs = acs - dta
    return dts, acs, ecs, dts.T, acs.T, ecs.T


def _split3(x):
    hi = x.astype(BF16)
    r1 = x - hi.astype(F32)
    mid = r1.astype(BF16)
    lo = (r1 - mid.astype(F32)).astype(BF16)
    return jnp.concatenate([hi, mid, lo], axis=1)


def _expand_matrix(first_row):
    r = lax.broadcasted_iota(jnp.int32, (3 * LANE, SSM_INNER), 0)
    c = lax.broadcasted_iota(jnp.int32, (3 * LANE, SSM_INNER), 1)
    return (c // SSM_HEAD_DIM + first_row == (r & (LANE - 1))).astype(BF16)


def _ssd_direction(backward, xs, bm, cm, scal, state_ref):
    dts, acs, ecs, dts_t, acs_t, ecs_t = scal
    q = xs.shape[0]
    base = SSM_HEADS if backward else 0
    expand = _expand_matrix(base)
    tot = acs[q - 1:q, :]
    if backward:
        dec_in = jnp.exp(tot - ecs)
        w_out = jnp.exp(ecs) * dts
        pos, pos_t = ecs, ecs_t
    else:
        dec_in = jnp.exp(acs)
        w_out = jnp.exp(tot - acs) * dts
        pos, pos_t = acs, acs_t
    stacked = jnp.concatenate([dec_in, w_out, jnp.broadcast_to(jnp.exp(tot), (SUBLANE, LANE))], axis=0)
    expanded = jnp.dot(_split3(stacked), expand, preferred_element_type=F32)
    dec_e, w_e, tot_e = expanded[0:q], expanded[q:2 * q], expanded[2 * q:2 * q + 1]
    xw = (xs * w_e).astype(BF16)
    xb = xs.astype(BF16)
    ri = lax.broadcasted_iota(jnp.int32, (q, q), 0)
    ci = lax.broadcasted_iota(jnp.int32, (q, q), 1)
    keep = (ci >= ri) if backward else (ci <= ri)
    gw = SSM_HPG * SSM_HEAD_DIM
    ys = []
    for g in range(SSM_GROUPS):
        b_g = bm[:, g * SSM_STATE:(g + 1) * SSM_STATE]
        c_g = cm[:, g * SSM_STATE:(g + 1) * SSM_STATE].astype(BF16)
        b_t = b_g.T.astype(BF16)
        cb = jnp.dot(c_g, b_t, preferred_element_type=F32)
        st = state_ref[g]
        y_off = jnp.dot(c_g, st.astype(BF16), preferred_element_type=F32) * dec_e[:, g * gw:(g + 1) * gw]
        state_ref[g] = st * tot_e[:, g * gw:(g + 1) * gw] + jnp.dot(
            b_t, xw[:, g * gw:(g + 1) * gw], preferred_element_type=F32)
        heads = []
        for hh in range(SSM_HPG):
            h = g * SSM_HPG + hh
            col = pos[:, base + h:base + h + 1]
            row = pos_t[base + h:base + h + 1, :]
            diff = (row - col) if backward else (col - row)
            seg = jnp.exp(jnp.where(keep, diff, -jnp.inf))
            m = (cb * seg * dts_t[base + h:base + h + 1, :]).astype(BF16)
            heads.append(jnp.dot(m, xb[:, h * SSM_HEAD_DIM:(h + 1) * SSM_HEAD_DIM],
                                 preferred_element_type=F32))
        ys.append(jnp.concatenate(heads, axis=1) + y_off)
    return jnp.concatenate(ys, axis=1)


SSD_CHUNKS_PER_STEP = 2


def _ssd_fwd_kernel(xc_ref, xp_ref, xn_ref, small_ref, cw_ref, cb_ref, dtb_ref, alog_ref, dskip_ref,
                    y_ref, u_ref, state_ref, *, n_ctx_chunks, n_chunks):
    step = pl.program_id(1)

    @pl.when(step == 0)
    def _():
        state_ref[...] = jnp.zeros_like(state_ref)

    q = SSM_CHUNK
    half = SSM_CONV // 2
    x_all = xc_ref[0]
    for j in range(SSD_CHUNKS_PER_STEP):
        c = step * SSD_CHUNKS_PER_STEP + j
        rows = slice(j * q, (j + 1) * q)
        prev_ok = jnp.logical_and(c != 0, c != n_ctx_chunks)
        next_ok = jnp.logical_and(c != n_ctx_chunks - 1, c != n_chunks - 1)
        xp = xp_ref[0] if j == 0 else x_all[j * q - HALO:j * q]
        xn = xn_ref[0] if j == SSD_CHUNKS_PER_STEP - 1 else x_all[(j + 1) * q:(j + 1) * q + HALO]
        xc = x_all[rows]
        ext = jnp.concatenate([jnp.where(prev_ok, xp, 0.0), xc, jnp.where(next_ok, xn, 0.0)], axis=0)
        acc = jnp.zeros_like(xc) + cb_ref[...]
        for k in range(SSM_CONV):
            lo = HALO - half + k
            acc = acc + ext[lo:lo + q, :] * cw_ref[k:k + 1, :]
        u = _silu(acc)
        u_ref[0, rows, :] = u.astype(u_ref.dtype)
        xs = u[:, :SSM_INNER]
        bm = u[:, SSM_INNER:SSM_INNER + SSM_BC]
        cm = u[:, SSM_INNER + SSM_BC:]
        scal = _ssd_scalars(small_ref[0, rows, :], dtb_ref, alog_ref)
        y = _ssd_direction(False, xs, bm, cm, scal, state_ref)
        y_ref[0, rows, :] = y + dskip_ref[...] * xs


def _ssd_bwd_kernel(u_ref, small_ref, z_ref, yf_ref, dtb_ref, alog_ref, ng_ref, o_ref, state_ref):
    @pl.when(pl.program_id(1) == 0)
    def _():
        state_ref[...] = jnp.zeros_like(state_ref)

    q = SSM_CHUNK
    for j in reversed(range(SSD_CHUNKS_PER_STEP)):
        rows = slice(j * q, (j + 1) * q)
        u = u_ref[0, rows, :].astype(F32)
        xs = u[:, :SSM_INNER]
        bm = u[:, SSM_INNER:SSM_INNER + SSM_BC]
        cm = u[:, SSM_INNER + SSM_BC:]
        scal = _ssd_scalars(small_ref[0, rows, :], dtb_ref, alog_ref)
        y = yf_ref[0, rows, :] + _ssd_direction(True, xs, bm, cm, scal, state_ref)
        o_ref[0, rows, :] = _rms(y * _silu(z_ref[0, rows, :]), ng_ref[...]).astype(o_ref.dtype)


def _ssd_branch(xbc, small, z, conv_w, conv_b, dt_bias, a_log, d_skip, norm_g, n_ctx):
    bsz, t, _ = xbc.shape
    q = SSM_CHUNK
    rows = SSD_CHUNKS_PER_STEP * q
    assert t % rows == 0 and n_ctx % rows == 0
    n_steps = t // rows
    n_ctx_steps = n_ctx // rows
    hb = rows // HALO
    n_halo = t // HALO
    pad32 = LANE - 2 * SSM_HEADS
    dtb = jnp.pad(dt_bias.reshape(1, -1), ((0, 0), (0, pad32)))
    alog = jnp.pad(a_log.reshape(1, -1), ((0, 0), (0, pad32)))
    dskip = jnp.repeat(d_skip, SSM_HEAD_DIM).reshape(1, SSM_INNER)
    state = pltpu.VMEM((SSM_GROUPS, SSM_STATE, SSM_HPG * SSM_HEAD_DIM), F32)

    def block(width):
        return pl.BlockSpec((1, rows, width), lambda b, s: (b, s, 0))

    y_f, u = pl.pallas_call(
        functools.partial(_ssd_fwd_kernel, n_ctx_chunks=n_ctx // q, n_chunks=t // q),
        grid=(bsz, n_steps),
        in_specs=[block(SSM_CONV_DIM),
                  pl.BlockSpec((1, HALO, SSM_CONV_DIM), lambda b, s: (b, jnp.maximum(s * hb - 1, 0), 0)),
                  pl.BlockSpec((1, HALO, SSM_CONV_DIM),
                               lambda b, s: (b, jnp.minimum((s + 1) * hb, n_halo - 1), 0)),
                  block(LANE),
                  _resident((SSM_CONV, SSM_CONV_DIM)), _resident((1, SSM_CONV_DIM)),
                  _resident((1, LANE)), _resident((1, LANE)), _resident((1, SSM_INNER))],
        out_specs=[block(SSM_INNER), block(SSM_CONV_DIM)],
        out_shape=[jax.ShapeDtypeStruct((bsz, t, SSM_INNER), F32),
                   jax.ShapeDtypeStruct((bsz, t, SSM_CONV_DIM), BF16)],
        scratch_shapes=[state],
        compiler_params=_cparams(2),
        name="ssd_forward",
    )(xbc, xbc, xbc, small, conv_w, conv_b.reshape(1, -1), dtb, alog, dskip)

    def rblock(width):
        return pl.BlockSpec(
            (1, rows, width),
            lambda b, s: (b, jnp.where(s < n_ctx_steps, n_ctx_steps - 1 - s, n_steps + n_ctx_steps - 1 - s), 0))

    return pl.pallas_call(
        _ssd_bwd_kernel,
        grid=(bsz, n_steps),
        in_specs=[rblock(SSM_CONV_DIM), rblock(LANE), rblock(SSM_INNER), rblock(SSM_INNER),
                  _resident((1, LANE)), _resident((1, LANE)), _resident((1, SSM_INNER))],
        out_specs=rblock(SSM_INNER),
        out_shape=jax.ShapeDtypeStruct((bsz, t, SSM_INNER), BF16),
        scratch_shapes=[state],
        compiler_params=_cparams(2),
        name="ssd_backward",
    )(u, small, z, y_f, dtb, alog, norm_g.reshape(1, -1))


def _rope_tables(n_lat, n_ctx, rot_dim):
    n_freq = rot_dim // 4
    inv = jnp.power(ROPE_BASE, -jnp.arange(n_freq, dtype=F32) / n_freq)
    t = jnp.arange(n_lat)
    r = (t // GRID_W).astype(F32)[:, None] * inv
    col = (t % GRID_W).astype(F32)[:, None] * inv
    cos2 = jnp.concatenate([jnp.cos(r), jnp.cos(r), jnp.cos(col), jnp.cos(col)], axis=1)
    sin2 = jnp.concatenate([-jnp.sin(r), jnp.sin(r), -jnp.sin(col), jnp.sin(col)], axis=1)
    cos2 = jnp.concatenate([jnp.ones((n_ctx, rot_dim), F32), cos2], axis=0)
    sin2 = jnp.concatenate([jnp.zeros((n_ctx, rot_dim), F32), sin2], axis=0)
    return cos2, sin2


def _rotate_half(x):
    f = x.shape[-1] // 4
    return jnp.concatenate([x[:, f:2 * f], x[:, :f], x[:, 3 * f:], x[:, 2 * f:3 * f]], axis=1)


SWA_BLOCKS_PER_STEP = 2


def _swa_kernel(sink_ref, qt_ref, k_ref, vt_ref, o_ref, *, first_block, n_ctx, t):
    for j in range(SWA_BLOCKS_PER_STEP):
        _swa_block(sink_ref, qt_ref, k_ref, vt_ref, o_ref, j, first_block=first_block, n_ctx=n_ctx, t=t)


def _swa_block(sink_ref, qt_ref, k_ref, vt_ref, o_ref, j, *, first_block, n_ctx, t):
    hk = pl.program_id(1)
    blk = pl.program_id(2) * SWA_BLOCKS_PER_STEP + j + first_block
    bq, dh = SWA_BLOCK, SWA_HEAD_DIM
    win = 3 * bq
    ncb = n_ctx // bq
    nq = SWA_GRP * bq
    qt = qt_ref[0, :, j * bq:(j + 1) * bq]
    q4t =jnp.concatenate([qt[g * dh:(g + 1) * dh, :] for g in range(SWA_GRP)], axis=1)
    start = pl.multiple_of(jnp.clip((blk - 1) * bq, 0, t - win), bq)
    b0 = start // bq
    s_w = jnp.dot(k_ref[0, pl.ds(start, win), :], q4t, preferred_element_type=F32)
    s_c = jnp.dot(k_ref[0, 0:n_ctx, :], q4t, preferred_element_type=F32)
    rows = lax.broadcasted_iota(jnp.int32, (win, nq), 0)
    cols = lax.broadcasted_iota(jnp.int32, (win, nq), 1)
    qpos = (blk - ncb) * bq + (cols & (bq - 1))
    kpos = start - n_ctx + rows
    ok = (jnp.abs(kpos - qpos) <= SWA_WINDOW) & (kpos >= 0) & (blk >= ncb)
    s_w = jnp.where(ok, s_w, -jnp.inf)
    c1 = lax.broadcasted_iota(jnp.int32, (1, nq), 1)
    sink = jnp.zeros((1, nq), F32)
    for g in range(SWA_GRP):
        sink = jnp.where(c1 // bq == g, sink_ref[hk * SWA_GRP + g], sink)
    sink = sink * LOG2E
    m = jnp.maximum(jnp.maximum(jnp.max(s_w, axis=0, keepdims=True),
                                jnp.max(s_c, axis=0, keepdims=True)), sink)
    p_w = jnp.exp2(s_w - m).astype(BF16)
    p_c = jnp.exp2(s_c - m).astype(BF16)

    def weighted_values(first, nblk, p):
        acc = None
        u = 0
        while u < nblk:
            n = 2 if u + 1 < nblk else 1
            vt = vt_ref[0, 0, first + u]
            if n == 2:
                vt = jnp.concatenate([vt, vt_ref[0, 0, first + u + 1]], axis=1)
            term = jnp.dot(vt, p[u * bq:(u + n) * bq], preferred_element_type=F32)
            acc = term if acc is None else acc + term
            u += n
        return acc

    acc = weighted_values(b0, win // bq, p_w) + weighted_values(0, ncb, p_c)
    o = acc[0:dh] / (acc[dh:dh + 1] + jnp.exp2(sink - m))
    for g in range(SWA_GRP):
        o_ref[0, j * bq:(j + 1) * bq, g * dh:(g + 1) * dh] = o[:, g * bq:(g + 1) * bq].T.astype(o_ref.dtype)


def _swa_attention(qt, k, vt, sink, n_ctx, first_block):
    bsz, dq, t = qt.shape
    nb = SWA_BLOCKS_PER_STEP
    bq = SWA_BLOCK
    gw = SWA_GRP * SWA_HEAD_DIM
    assert first_block % nb == 0 and (t // bq) % nb == 0
    nblk = (t // bq - first_block) // nb
    first = first_block // nb
    return pl.pallas_call(
        functools.partial(_swa_kernel, first_block=first_block, n_ctx=n_ctx, t=t),
        grid=(bsz, SWA_KV_HEADS, nblk),
        in_specs=[pl.BlockSpec(memory_space=pltpu.SMEM),
                  pl.BlockSpec((1, gw, nb * bq), lambda b, h, n: (b, h, n + first)),
                  pl.BlockSpec((1, t, SWA_HEAD_DIM), lambda b, h, n: (b, 0, h)),
                  pl.BlockSpec((1, 1, t // bq, SWA_VT_ROWS, bq), lambda b, h, n: (b, h, 0, 0, 0))],
        out_specs=pl.BlockSpec((1, nb * bq, gw), lambda b, h, n: (b, n + first, h)),
        out_shape=jax.ShapeDtypeStruct((bsz, t, dq), BF16),
        compiler_params=_cparams(3),
        name="swa_attention",
    )(sink, qt, k, vt)


def _rms_cols(x, g):
    return x * lax.rsqrt(jnp.mean(x * x, axis=0, keepdims=True) + EPS) * g


def _rotate_half_rows(x):
    f = x.shape[0] // 4
    return jnp.concatenate([x[f:2 * f], x[:f], x[3 * f:], x[2 * f:3 * f]], axis=0)


MLA_HEADS_PER_STEP = 2
MLA_QBLOCKS_PER_STEP = 2


def _mla_kernel(qt_ref, k_ref, vt_ref, o_ref, s_scr, p_scr, a_scr, mx_scr, acc_scr, m_scr):
    heads = qt_ref.shape[1]
    n_chunks, kb = vt_ref.shape[2], vt_ref.shape[4]
    tq = TOKEN_BLOCK
    n_qb = qt_ref.shape[3] // tq
    total = n_qb * n_chunks

    def stage_scores(c, item, slot):
        qb, j = divmod(item, n_chunks)
        s = jnp.dot(k_ref[0, c, j * kb:(j + 1) * kb, :], qt_ref[0, c, :, qb * tq:(qb + 1) * tq],
                    preferred_element_type=F32)
        s_scr[c, slot] = s
        mx_scr[c, slot] = jnp.max(s, axis=0, keepdims=True)

    def stage_softmax(c, item, slot):
        qb = item // n_chunks
        m_prev = m_scr[c, qb]
        m_new = jnp.maximum(m_prev, mx_scr[c, slot])
        m_scr[c, qb] = m_new
        a_scr[c, slot] = jnp.exp2(m_prev - m_new)
        p_scr[c, slot] = jnp.exp2(s_scr[c, slot] - m_new).astype(BF16)

    def stage_values(c, item, slot):
        qb, j = divmod(item, n_chunks)
        pv = jnp.dot(vt_ref[0, c, j], p_scr[c, slot], preferred_element_type=F32)
        acc_scr[c, qb] = a_scr[c, slot] * acc_scr[c, qb] + pv

    m_scr[...] = jnp.full(m_scr.shape, -jnp.inf, F32)
    acc_scr[...] = jnp.zeros(acc_scr.shape, F32)
    for t in range(total + 2):
        parity = t % 2
        for c in range(heads):
            if t < total:
                stage_scores(c, t, parity)
            if 1 <= t <= total:
                stage_softmax(c, t - 1, 1 - parity)
            if t >= 2:
                stage_values(c, t - 2, parity)
    for c in range(heads):
        for qb in range(n_qb):
            acc = acc_scr[c, qb]
            o = acc[0:MLA_V] / acc[MLA_V:MLA_V + 1]
            o_ref[0, qb * tq:(qb + 1) * tq, c * MLA_V:(c + 1) * MLA_V] = o.T.astype(o_ref.dtype)


def _mla_ctx_kernel(qt_ref, k_ref, vt_ref, y_ref, o_ref):
    del y_ref
    kb = vt_ref.shape[4]
    for c in range(qt_ref.shape[1]):
        s = jnp.dot(k_ref[0, c], qt_ref[0, c], preferred_element_type=F32)
        p = jnp.exp2(s - jnp.max(s, axis=0, keepdims=True)).astype(BF16)
        acc = jnp.dot(vt_ref[0, c, 0], p[0:kb], preferred_element_type=F32)
        for u in range(1, vt_ref.shape[2]):
            acc = acc + jnp.dot(vt_ref[0, c, u], p[u * kb:(u + 1) * kb], preferred_element_type=F32)
        o = acc[0:MLA_V] / acc[MLA_V:MLA_V + 1]
        o_ref[0, :, c * MLA_V:(c + 1) * MLA_V] = o.T.astype(o_ref.dtype)


def _mla_attention(qt, k, vt, n_ctx, with_ctx):
    bsz, nh, dqk, t = qt.shape
    tq = TOKEN_BLOCK
    kb = vt.shape[-1]
    n_key_blocks = t // kb
    g = MLA_HEADS_PER_STEP
    nq = MLA_QBLOCKS_PER_STEP * tq
    n_lat = t - n_ctx
    assert n_lat % nq == 0 and n_lat % n_ctx == 0 and n_ctx % kb == 0
    y = pl.pallas_call(
        _mla_kernel,
        grid=(bsz, nh // g, n_lat // nq),
        in_specs=[pl.BlockSpec((1, g, dqk, nq), lambda b, h, i: (b, h, 0, i)),
                  pl.BlockSpec((1, g, t, dqk), lambda b, h, i: (b, h, 0, 0)),
                  pl.BlockSpec((1, g, n_key_blocks, MLA_VT_ROWS, kb), lambda b, h, i: (b, h, 0, 0, 0))],
        out_specs=pl.BlockSpec((1, nq, g * MLA_V), lambda b, h, i: (b, i, h)),
        out_shape=jax.ShapeDtypeStruct((bsz, t, nh * MLA_V), BF16),
        scratch_shapes=[pltpu.VMEM((g, 2, kb, tq), F32),
                        pltpu.VMEM((g, 2, kb, tq), BF16),
                        pltpu.VMEM((g, 2, 1, tq), F32),
                        pltpu.VMEM((g, 2, 1, tq), F32),
                        pltpu.VMEM((g, MLA_QBLOCKS_PER_STEP, MLA_VT_ROWS, tq), F32),
                        pltpu.VMEM((g, MLA_QBLOCKS_PER_STEP, 1, tq), F32)],
        compiler_params=_cparams(3),
        name="mla_attention",
    )(qt, k, vt)
    if not with_ctx:
        return y
    ctx_blk = n_lat // n_ctx
    ncb = n_ctx // kb
    return pl.pallas_call(
        _mla_ctx_kernel,
        grid=(bsz, nh // g),
        in_specs=[pl.BlockSpec((1, g, dqk, n_ctx), lambda b, h: (b, h, 0, ctx_blk)),
                  pl.BlockSpec((1, g, n_ctx, dqk), lambda b, h: (b, h, ctx_blk, 0)),
                  pl.BlockSpec((1, g, ncb, MLA_VT_ROWS, kb), lambda b, h: (b, h, ctx_blk, 0, 0)),
                  pl.BlockSpec(memory_space=pl.ANY)],
        out_specs=pl.BlockSpec((1, n_ctx, g * MLA_V), lambda b, h: (b, ctx_blk, h)),
        out_shape=jax.ShapeDtypeStruct(y.shape, y.dtype),
        input_output_aliases={3: 0},
        compiler_params=_cparams(2),
        name="mla_attention_ctx",
    )(qt, k, vt, y)


def _ffn_chunks(hidden):
    step = 512
    return tuple((lo, min(lo + step, hidden)) for lo in range(0, hidden, step))


def _merge_ffn_kernel(x_ref, mod_ref, gates_ref, ys_ref, yw_ref, ym_ref, wps_ref, wpw_ref, wpm_ref,
                      wo_ref, g2_ref, wfi_ref, wfo_ref, o_ref):
    d = x_ref.shape[-1]
    hidden = wfo_ref.shape[0]
    mod = mod_ref[0, 0]
    gt1, sh2, sc2, gt2 = mod[2:3], mod[3:4], mod[4:5], mod[5:6]
    gates = jax.nn.sigmoid(gates_ref[0])
    merged = (gates[:, 0:d] * jnp.dot(ys_ref[0], wps_ref[...], preferred_element_type=F32)
              + gates[:, d:2 * d] * jnp.dot(yw_ref[0], wpw_ref[...], preferred_element_type=F32)
              + gates[:, 2 * d:3 * d] * jnp.dot(ym_ref[0], wpm_ref[...], preferred_element_type=F32))
    x1 = x_ref[0] + gt1 * jnp.dot(merged.astype(BF16), wo_ref[...], preferred_element_type=F32)
    hb = (_rms(x1, g2_ref[...]) * (1.0 + sc2) + sh2).astype(BF16)
    acc = jnp.zeros_like(x1)
    for lo, hi in _ffn_chunks(hidden):
        gate = jnp.dot(hb, wfi_ref[:, lo:hi], preferred_element_type=F32)
        up = jnp.dot(hb, wfi_ref[:, hidden + lo:hidden + hi], preferred_element_type=F32)
        acc = acc + jnp.dot((_silu(gate) * up).astype(BF16), wfo_ref[lo:hi, :], preferred_element_type=F32)
    o_ref[0] = x1 + gt2 * acc


def _merge_ffn(x_all, modsel, gates, y_ssm, y_swa, y_mla, wps, wpw, wpm, wo, norm2_g, wfi, wfo,
               n_ctx_blocks, first_block):
    bsz, t, d = x_all.shape
    tm = TOKEN_BLOCK
    nblk = t // tm - first_block

    def tok(width):
        return pl.BlockSpec((1, tm, width), lambda b, i: (b, i + first_block, 0))

    return pl.pallas_call(
        _merge_ffn_kernel,
        grid=(bsz, nblk),
        in_specs=[tok(d),
                  pl.BlockSpec((1, 1, SUBLANE, d),
                               lambda b, i: (b, jnp.where(i + first_block < n_ctx_blocks, 0, 1), 0, 0)),
                  tok(N_BRANCH * d), tok(y_ssm.shape[-1]), tok(y_swa.shape[-1]),
                  pl.BlockSpec((1, tm, y_mla.shape[-1]),
                               lambda b, i: (b, _latent_first(i + first_block, n_blocks=t // tm,
                                                              n_ctx_blocks=n_ctx_blocks), 0)),
                  _resident(wps.shape), _resident(wpw.shape), _resident(wpm.shape), _resident(wo.shape),
                  _resident((1, d)), _resident(wfi.shape), _resident(wfo.shape)],
        out_specs=pl.BlockSpec((1, tm, d), lambda b, i: (b, i, 0)),
        out_shape=jax.ShapeDtypeStruct((bsz, nblk * tm, d), F32),
        compiler_params=_cparams(2),
        name="merge_ffn",
    )(x_all, modsel, gates, y_ssm, y_swa, y_mla, wps, wpw, wpm, wo, norm2_g.reshape(1, d), wfi, wfo)


def _head_major(w, n_heads, parts):
    k = w.shape[0]
    w = w.reshape(k, n_heads, sum(parts))
    out, lo = [], 0
    for p in parts:
        out.append(w[:, :, lo:lo + p].reshape(k, n_heads * p))
        lo += p
    return jnp.concatenate(out, axis=1)


def kernel(x, c, ctx, c_ctx, w_mod, b_mod, norm1_g, norm2_g, w_in, ssm_conv_w, ssm_conv_b, ssm_dt_bias,
           ssm_a_log, ssm_d, ssm_norm_g, swa_q_norm_g, swa_k_norm_g, swa_sink, mla_q_lat_g, mla_kv_lat_g,
           w_mla_uq, w_mla_ukv, mla_q_norm_g, mla_k_norm_g, w_p_ssm, w_p_swa, w_p_mla, w_out, w_ffn_in,
           w_ffn_out):
    bsz, n_lat, d = x.shape
    n_ctx = ctx.shape[1]
    depth = w_mod.shape[0]
    assert n_ctx % TOKEN_BLOCK == 0 and n_lat % TOKEN_BLOCK == 0 and n_lat % GRID_W == 0
    assert bsz + 1 <= SUBLANE
    n_ctx_blocks = n_ctx // TOKEN_BLOCK

    cvec = jnp.concatenate([c, c_ctx[None], jnp.zeros((SUBLANE - bsz - 1, d), F32)], axis=0)
    mods = _modulation(cvec, w_mod, b_mod).reshape(depth, SUBLANE, 6, d)
    rope_swa = _rope_tables(n_lat, n_ctx, SWA_HEAD_DIM)
    rope_mla = _rope_tables(n_lat, n_ctx, MLA_ROPE)

    x_all = jnp.concatenate([ctx, x], axis=1)
    for i in range(depth):
        last = i == depth - 1
        ctx_mod = jnp.broadcast_to(mods[i, bsz][None], (bsz, 6, d))
        modsel = jnp.pad(jnp.stack([ctx_mod, mods[i, :bsz]], axis=1), ((0, 0), (0, 0), (0, SUBLANE - 6), (0, 0)))

        xbc, small, z, gates, qs, ks, vs, qm, km, vm = _in_projection(
            x_all, modsel, norm1_g[i], w_in[i], w_mla_uq[i], w_mla_ukv[i], swa_q_norm_g[i], swa_k_norm_g[i],
            mla_q_lat_g[i], mla_kv_lat_g[i], mla_q_norm_g[i], mla_k_norm_g[i], rope_swa, rope_mla, n_ctx_blocks)

        y_ssm = _ssd_branch(xbc, small, z, ssm_conv_w[i], ssm_conv_b[i], ssm_dt_bias[i], ssm_a_log[i],
                            ssm_d[i], ssm_norm_g[i], n_ctx)
        y_swa = _swa_attention(qs, ks, vs, swa_sink[i], n_ctx, n_ctx // SWA_BLOCK if last else 0)
        first_tok = n_ctx_blocks if last else 0
        y_mla = _mla_attention(qm, km, vm, n_ctx, with_ctx=not last)

        x_all = _merge_ffn(x_all, modsel, gates, y_ssm, y_swa, y_mla,
                           w_p_ssm[i].astype(BF16), w_p_swa[i].astype(BF16), w_p_mla[i].astype(BF16),
                           w_out[i].astype(BF16), norm2_g[i], w_ffn_in[i].astype(BF16),
                           w_ffn_out[i].astype(BF16), n_ctx_blocks, first_tok)
    return x_all
```

```python
import functools
import math

import jax
import jax.numpy as jnp
from jax import lax
from jax.experimental import pallas as pl
from jax.experimental.pallas import tpu as pltpu

F32 = jnp.float32
BF16 = jnp.bfloat16
HIGHEST = lax.Precision.HIGHEST

EPS = 1e-6
ROPE_BASE = 10000.0
GRID_W = 64

SSM_HEADS = 16
SSM_HEAD_DIM = 64
SSM_INNER = SSM_HEADS * SSM_HEAD_DIM
SSM_GROUPS = 2
SSM_STATE = 128
SSM_CONV = 5
SSM_CHUNK = 128
SSM_BC = SSM_GROUPS * SSM_STATE
SSM_CONV_DIM = SSM_INNER + 2 * SSM_BC
SSM_HPG = SSM_HEADS // SSM_GROUPS

SWA_Q_HEADS = 8
SWA_KV_HEADS = 2
SWA_HEAD_DIM = 128
SWA_WINDOW = 128
SWA_BLOCK = 128
SWA_GRP = SWA_Q_HEADS // SWA_KV_HEADS

MLA_HEADS = 8
MLA_Q_RANK = 384
MLA_KV_RANK = 256
MLA_NOPE = 128
MLA_ROPE = 64
MLA_QK = MLA_NOPE + MLA_ROPE
MLA_V = 128
BF16_SUBLANES = 16
MLA_VT_ROWS = MLA_V + BF16_SUBLANES

N_BRANCH = 3
LANE = 128
SUBLANE = 8
HALO = SUBLANE
TOKEN_BLOCK = 256
VMEM_LIMIT = 56 * 1024 * 1024


def _cparams(n_axes):
    return pltpu.CompilerParams(
        dimension_semantics=("arbitrary",) * n_axes, vmem_limit_bytes=VMEM_LIMIT)


def _resident(shape):
    nd = len(shape)
    return pl.BlockSpec(shape, lambda *_: (0,) * nd, pipeline_mode=pl.Buffered(1))


def _rms(x, g):
    return x * lax.rsqrt(jnp.mean(x * x, axis=-1, keepdims=True) + EPS) * g


def _silu(x):
    return x * jax.nn.sigmoid(x)


def _xdot(a, b):
    return jnp.dot(a, b, precision=HIGHEST, preferred_element_type=F32)


def _mod_kernel(c_ref, w_ref, b_ref, o_ref):
    o_ref[0] = _xdot(_silu(c_ref[...]), w_ref[0]) + b_ref[0]


def _modulation(cvec, w_mod, b_mod):
    depth, d, d6 = w_mod.shape
    rows = cvec.shape[0]
    return pl.pallas_call(
        _mod_kernel,
        grid=(depth, d6 // d),
        in_specs=[pl.BlockSpec((rows, d), lambda i, j: (0, 0)),
                  pl.BlockSpec((1, d, d), lambda i, j: (i, 0, j)),
                  pl.BlockSpec((1, 1, d), lambda i, j: (i, 0, j))],
        out_specs=pl.BlockSpec((1, rows, d), lambda i, j: (i, 0, j)),
        out_shape=jax.ShapeDtypeStruct((depth, rows, d6), F32),
        compiler_params=_cparams(2),
        name="modulation",
    )(cvec, w_mod, b_mod.reshape(depth, 1, d6))


SWA_DQ = SWA_Q_HEADS * SWA_HEAD_DIM
SWA_DKV = SWA_KV_HEADS * SWA_HEAD_DIM
SWA_VT_ROWS = SWA_HEAD_DIM + BF16_SUBLANES
LOG2E = math.log2(math.e)


def _split_in_weights(w_in_l, d):
    o = [0]
    for w in (SSM_CONV_DIM, 2 * SSM_HEADS, SWA_DKV, SWA_DKV, MLA_KV_RANK, MLA_ROPE, SSM_INNER, SWA_DQ,
              MLA_Q_RANK, N_BRANCH * d):
        o.append(o[-1] + w)
    xbc, dt, k, v, ckv, kr, z, q, cq, gates = (w_in_l[:, o[i]:o[i + 1]] for i in range(10))
    pad = jnp.zeros((d, LANE - 2 * SSM_HEADS - MLA_ROPE), w_in_l.dtype)
    w_tok = jnp.concatenate([xbc, dt, kr, pad, k, ckv, z, gates], axis=1).astype(BF16)
    w_feat = jnp.concatenate([q, v, cq], axis=1).T.astype(BF16)
    return w_tok, w_feat


def _latent_first(i, *, n_blocks, n_ctx_blocks):
    return jnp.where(i < n_ctx_blocks, n_blocks - n_ctx_blocks + i, i - n_ctx_blocks)


def _inproj_kernel(x_ref, mod_ref, g_ref, wtok_ref, wfeat_ref, wqt_ref, wk_ref, wvt_ref,
                   sqg_ref, skg_ref, qlg_ref, kvlg_ref, qgn_ref, qgr_ref, kgn_ref, kgr_ref,
                   scos_ref, ssin_ref, scost_ref, ssint_ref, mcos_ref, msin_ref, mcost_ref, msint_ref,
                   xbc_ref, small_ref, z_ref, gates_ref, sq_ref, sk_ref, sv_ref, mq_ref, mk_ref, mv_ref):
    d = x_ref.shape[-1]
    tm = x_ref.shape[1]
    mod = mod_ref[0, 0]
    hb = (_rms(x_ref[0], g_ref[...]) * (1.0 + mod[1:2]) + mod[0:1]).astype(BF16)

    def tok(lo, width):
        return jnp.dot(hb, wtok_ref[:, lo:lo + width], preferred_element_type=F32)

    o_small = SSM_CONV_DIM
    o_k = o_small + LANE
    o_ckv = o_k + SWA_DKV
    o_z = o_ckv + MLA_KV_RANK
    o_gates = o_z + SSM_INNER
    dh = SWA_HEAD_DIM
    nh, dn, dr, dv = MLA_HEADS, MLA_NOPE, MLA_ROPE, MLA_V
    qscale = MLA_QK ** -0.5 * LOG2E
    o_cq = SWA_DQ + SWA_DKV

    feat = lax.dot_general(wfeat_ref[...], hb, (((1,), (1,)), ((), ())), preferred_element_type=F32)
    small = tok(o_small, LANE)
    small_ref[0] = small
    k_swa = tok(o_k, SWA_DKV)
    ckvn = _rms(tok(o_ckv, MLA_KV_RANK), kvlg_ref[...]).astype(BF16)
    cqn_t = _rms_cols(feat[o_cq:o_cq + MLA_Q_RANK], qlg_ref[...]).astype(BF16)
    qf_t = jnp.dot(wqt_ref[...], cqn_t, preferred_element_type=F32)

    def swa_queries(heads):
        scos_t, ssin_t = scost_ref[...], ssint_ref[...]
        for h in heads:
            xq = _rms_cols(feat[h * dh:(h + 1) * dh], sqg_ref[...])
            xq = xq * scos_t + _rotate_half_rows(xq) * ssin_t
            sq_ref[0, h * dh:(h + 1) * dh, :] = (xq * (dh ** -0.5 * LOG2E)).astype(sq_ref.dtype)

    def swa_keys_values():
        scos, ssin = scos_ref[...], ssin_ref[...]
        for h in range(SWA_KV_HEADS):
            vt = feat[SWA_DQ + h * dh:SWA_DQ + (h + 1) * dh]
            for u in range(tm // SWA_BLOCK):
                sv_ref[0, h, u, 0:dh, :] = vt[:, u * SWA_BLOCK:(u + 1) * SWA_BLOCK].astype(sv_ref.dtype)
                sv_ref[0, h, u, dh:, :] = jnp.ones((SWA_VT_ROWS - dh, SWA_BLOCK), sv_ref.dtype)
            xk = _rms(k_swa[:, h * dh:(h + 1) * dh], skg_ref[...])
            xk = xk * scos + _rotate_half(xk) * ssin
            sk_ref[0, :, h * dh:(h + 1) * dh] = xk.astype(sk_ref.dtype)

    def mla_queries(heads):
        mcos_t, msin_t = mcost_ref[...], msint_ref[...]
        for h in heads:
            qn = _rms_cols(qf_t[h * dn:(h + 1) * dn], qgn_ref[...])
            qr = _rms_cols(qf_t[nh * dn + h * dr:nh * dn + (h + 1) * dr], qgr_ref[...])
            qr = qr * mcos_t + _rotate_half_rows(qr) * msin_t
            mq_ref[0, h, 0:dn, :] = (qn * qscale).astype(mq_ref.dtype)
            mq_ref[0, h, dn:dn + dr, :] = (qr * qscale).astype(mq_ref.dtype)

    def mla_keys():
        kf = jnp.dot(ckvn, wk_ref[...], preferred_element_type=F32)
        kr0 = 2 * SSM_HEADS
        kr = _rms(small[:, kr0:kr0 + dr], kgr_ref[...])
        kr = kr * mcos_ref[...] + _rotate_half(kr) * msin_ref[...]
        for h in range(nh):
            kn = _rms(kf[:, h * dn:(h + 1) * dn], kgn_ref[...])
            mk_ref[0, h] = jnp.concatenate([kn, kr], axis=1).astype(mk_ref.dtype)

    def mla_values():
        vf_t = lax.dot_general(wvt_ref[...], ckvn, (((1,), (1,)), ((), ())), preferred_element_type=F32)
        for h in range(nh):
            mv_ref[0, h, 0, 0:dv, :] = vf_t[h * dv:(h + 1) * dv].astype(mv_ref.dtype)
            mv_ref[0, h, 0, dv:, :] = jnp.ones((MLA_VT_ROWS - dv, tm), mv_ref.dtype)

    half_q = SWA_Q_HEADS // 2
    half_c = SSM_CONV_DIM // 2
    xbc_ref[0, :, 0:half_c] = tok(0, half_c)
    swa_queries(range(0, half_q))
    xbc_ref[0, :, half_c:] = tok(half_c, SSM_CONV_DIM - half_c)
    swa_queries(range(half_q, SWA_Q_HEADS))
    z_ref[0] = tok(o_z, SSM_INNER)
    swa_keys_values()
    gates_ref[0, :, 0:d] = tok(o_gates, d)
    mla_queries(range(0, nh // 2))
    gates_ref[0, :, d:2 * d] = tok(o_gates + d, d)
    mla_queries(range(nh // 2, nh))
    gates_ref[0, :, 2 * d:] = tok(o_gates + 2 * d, d)
    mla_keys()
    mla_values()


def _in_projection(x_all, modsel, norm_g, w_in_l, w_uq, w_ukv, swa_q_g, swa_k_g, q_lat_g, kv_lat_g,
                   mla_q_g, mla_k_g, rope_swa, rope_mla, n_ctx_blocks):
    bsz, t, d = x_all.shape
    tm = TOKEN_BLOCK
    nh = MLA_HEADS
    w_tok, w_feat = _split_in_weights(w_in_l, d)
    wq_t = _head_major(w_uq, nh, (MLA_NOPE, MLA_ROPE)).T.astype(BF16)
    wkv = _head_major(w_ukv, nh, (MLA_NOPE, MLA_V))
    wk = wkv[:, :nh * MLA_NOPE].astype(BF16)
    wv_t = wkv[:, nh * MLA_NOPE:].T.astype(BF16)
    scos, ssin = rope_swa
    mcos, msin = rope_mla

    def cols(g):
        return jnp.broadcast_to(g[:, None], (g.shape[0], tm))

    def row(g):
        return g.reshape(1, -1)

    def tok(width):
        return pl.BlockSpec((1, tm, width), lambda b, i: (b, i, 0))

    def tok_table(width):
        return pl.BlockSpec((tm, width), lambda b, i: (i, 0))

    def feat_table(height):
        return pl.BlockSpec((height, tm), lambda b, i: (0, i))

    consts = [row(norm_g), w_tok, w_feat, wq_t, wk, wv_t,
              cols(swa_q_g), row(swa_k_g), cols(q_lat_g), row(kv_lat_g),
              cols(mla_q_g[:MLA_NOPE]), cols(mla_q_g[MLA_NOPE:]), row(mla_k_g[:MLA_NOPE]), row(mla_k_g[MLA_NOPE:])]
    nkb = tm // SWA_BLOCK
    mla_pos = functools.partial(_latent_first, n_blocks=t // tm, n_ctx_blocks=n_ctx_blocks)
    return pl.pallas_call(
        _inproj_kernel,
        grid=(bsz, t // tm),
        in_specs=[tok(d),
                  pl.BlockSpec((1, 1, SUBLANE, d), lambda b, i: (b, jnp.where(i < n_ctx_blocks, 0, 1), 0, 0))]
                 + [_resident(a.shape) for a in consts]
                 + [tok_table(SWA_HEAD_DIM), tok_table(SWA_HEAD_DIM), feat_table(SWA_HEAD_DIM),
                    feat_table(SWA_HEAD_DIM), tok_table(MLA_ROPE), tok_table(MLA_ROPE), feat_table(MLA_ROPE),
                    feat_table(MLA_ROPE)],
        out_specs=[tok(SSM_CONV_DIM), tok(LANE), tok(SSM_INNER), tok(N_BRANCH * d),
                   pl.BlockSpec((1, SWA_DQ, tm), lambda b, i: (b, 0, i)),
                   tok(SWA_DKV),
                   pl.BlockSpec((1, SWA_KV_HEADS, nkb, SWA_VT_ROWS, SWA_BLOCK), lambda b, i: (b, 0, i, 0, 0)),
                   pl.BlockSpec((1, nh, MLA_QK, tm), lambda b, i: (b, 0, 0, mla_pos(i))),
                   pl.BlockSpec((1, nh, tm, MLA_QK), lambda b, i: (b, 0, mla_pos(i), 0)),
                   pl.BlockSpec((1, nh, 1, MLA_VT_ROWS, tm), lambda b, i: (b, 0, mla_pos(i), 0, 0))],
        out_shape=[jax.ShapeDtypeStruct((bsz, t, SSM_CONV_DIM), F32),
                   jax.ShapeDtypeStruct((bsz, t, LANE), F32),
                   jax.ShapeDtypeStruct((bsz, t, SSM_INNER), F32),
                   jax.ShapeDtypeStruct((bsz, t, N_BRANCH * d), F32),
                   jax.ShapeDtypeStruct((bsz, SWA_DQ, t), BF16),
                   jax.ShapeDtypeStruct((bsz, t, SWA_DKV), BF16),
                   jax.ShapeDtypeStruct((bsz, SWA_KV_HEADS, t // SWA_BLOCK, SWA_VT_ROWS, SWA_BLOCK), BF16),
                   jax.ShapeDtypeStruct((bsz, nh, MLA_QK, t), BF16),
                   jax.ShapeDtypeStruct((bsz, nh, t, MLA_QK), BF16),
                   jax.ShapeDtypeStruct((bsz, nh, t // tm, MLA_VT_ROWS, tm), BF16)],
        compiler_params=_cparams(2),
        name="in_projection",
    )(x_all, modsel, *consts, scos, ssin, scos.T, ssin.T, mcos, msin, mcos.T, msin.T)


def _ssd_scalars(small, dtb_ref, alog_ref):
    q = small.shape[0]
    lane = lax.broadcasted_iota(jnp.int32, (1, LANE), 1)
    raw = small + dtb_ref[...]
    dts = jnp.maximum(raw, 0.0) + jnp.log1p(jnp.exp(-jnp.abs(raw)))
    a = jnp.where(lane < 2 * SSM_HEADS, -jnp.exp(alog_ref[...]), 0.0)
    dta = dts * a
    ri = lax.broadcasted_iota(jnp.int32, (q, q), 0)
    ci = lax.broadcasted_iota(jnp.int32, (q, q), 1)
    tri = (ci <= ri).astype(BF16)
    parts = jnp.dot(tri, _split3(dta), preferred_element_type=F32)
    acs = parts[:, :LANE] + parts[:, LANE:2 * LANE] + parts[:, 2 * LANE:]
    ecs = acs - dta
    return dts, acs, ecs, dts.T, acs.T, ecs.T


def _split3(x):
    hi = x.astype(BF16)
    r1 = x - hi.astype(F32)
    mid = r1.astype(BF16)
    lo = (r1 - mid.astype(F32)).astype(BF16)
    return jnp.concatenate([hi, mid, lo], axis=1)


def _expand_matrix(first_row):
    r = lax.broadcasted_iota(jnp.int32, (3 * LANE, SSM_INNER), 0)
    c = lax.broadcasted_iota(jnp.int32, (3 * LANE, SSM_INNER), 1)
    return (c // SSM_HEAD_DIM + first_row == (r & (LANE - 1))).astype(BF16)


def _ssd_direction(backward, xs, bm, cm, scal, state_ref):
    dts, acs, ecs, dts_t, acs_t, ecs_t = scal
    q = xs.shape[0]
    base = SSM_HEADS if backward else 0
    expand = _expand_matrix(base)
    tot = acs[q - 1:q, :]
    if backward:
        dec_in = jnp.exp(tot - ecs)
        w_out = jnp.exp(ecs) * dts
        pos, pos_t = ecs, ecs_t
    else:
        dec_in = jnp.exp(acs)
        w_out = jnp.exp(tot - acs) * dts
        pos, pos_t = acs, acs_t
    stacked = jnp.concatenate([dec_in, w_out, jnp.broadcast_to(jnp.exp(tot), (SUBLANE, LANE))], axis=0)
    expanded = jnp.dot(_split3(stacked), expand, preferred_element_type=F32)
    dec_e, w_e, tot_e = expanded[0:q], expanded[q:2 * q], expanded[2 * q:2 * q + 1]
    xw = (xs * w_e).astype(BF16)
    xb = xs.astype(BF16)
    ri = lax.broadcasted_iota(jnp.int32, (q, q), 0)
    ci = lax.broadcasted_iota(jnp.int32, (q, q), 1)
    keep = (ci >= ri) if backward else (ci <= ri)
    gw = SSM_HPG * SSM_HEAD_DIM
    ys = []
    for g in range(SSM_GROUPS):
        b_g = bm[:, g * SSM_STATE:(g + 1) * SSM_STATE]
        c_g = cm[:, g * SSM_STATE:(g + 1) * SSM_STATE].astype(BF16)
        b_t = b_g.T.astype(BF16)
        cb = jnp.dot(c_g, b_t, preferred_element_type=F32)
        st = state_ref[g]
        y_off = jnp.dot(c_g, st.astype(BF16), preferred_element_type=F32) * dec_e[:, g * gw:(g + 1) * gw]
        state_ref[g] = st * tot_e[:, g * gw:(g + 1) * gw] + jnp.dot(
            b_t, xw[:, g * gw:(g + 1) * gw], preferred_element_type=F32)
        heads = []
        for hh in range(SSM_HPG):
            h = g * SSM_HPG + hh
            col = pos[:, base + h:base + h + 1]
            row = pos_t[base + h:base + h + 1, :]
            diff = (row - col) if backward else (col - row)
            seg = jnp.exp(jnp.where(keep, diff, -jnp.inf))
            m = (cb * seg * dts_t[base + h:base + h + 1, :]).astype(BF16)
            heads.append(jnp.dot(m, xb[:, h * SSM_HEAD_DIM:(h + 1) * SSM_HEAD_DIM],
                                 preferred_element_type=F32))
        ys.append(jnp.concatenate(heads, axis=1) + y_off)
    return jnp.concatenate(ys, axis=1)


SSD_CHUNKS_PER_STEP = 2


def _ssd_fwd_kernel(xc_ref, xp_ref, xn_ref, small_ref, cw_ref, cb_ref, dtb_ref, alog_ref, dskip_ref,
                    y_ref, u_ref, state_ref, *, n_ctx_chunks, n_chunks):
    step = pl.program_id(1)

    @pl.when(step == 0)
    def _():
        state_ref[...] = jnp.zeros_like(state_ref)

    q = SSM_CHUNK
    half = SSM_CONV // 2
    x_all = xc_ref[0]
    for j in range(SSD_CHUNKS_PER_STEP):
        c = step * SSD_CHUNKS_PER_STEP + j
        rows = slice(j * q, (j + 1) * q)
        prev_ok = jnp.logical_and(c != 0, c != n_ctx_chunks)
        next_ok = jnp.logical_and(c != n_ctx_chunks - 1, c != n_chunks - 1)
        xp = xp_ref[0] if j == 0 else x_all[j * q - HALO:j * q]
        xn = xn_ref[0] if j == SSD_CHUNKS_PER_STEP - 1 else x_all[(j + 1) * q:(j + 1) * q + HALO]
        xc = x_all[rows]
        ext = jnp.concatenate([jnp.where(prev_ok, xp, 0.0), xc, jnp.where(next_ok, xn, 0.0)], axis=0)
        acc = jnp.zeros_like(xc) + cb_ref[...]
        for k in range(SSM_CONV):
            lo = HALO - half + k
            acc = acc + ext[lo:lo + q, :] * cw_ref[k:k + 1, :]
        u = _silu(acc)
        u_ref[0, rows, :] = u.astype(u_ref.dtype)
        xs = u[:, :SSM_INNER]
        bm = u[:, SSM_INNER:SSM_INNER + SSM_BC]
        cm = u[:, SSM_INNER + SSM_BC:]
        scal = _ssd_scalars(small_ref[0, rows, :], dtb_ref, alog_ref)
        y = _ssd_direction(False, xs, bm, cm, scal, state_ref)
        y_ref[0, rows, :] = y + dskip_ref[...] * xs


def _ssd_bwd_kernel(u_ref, small_ref, z_ref, yf_ref, dtb_ref, alog_ref, ng_ref, o_ref, state_ref):
    @pl.when(pl.program_id(1) == 0)
    def _():
        state_ref[...] = jnp.zeros_like(state_ref)

    q = SSM_CHUNK
    for j in reversed(range(SSD_CHUNKS_PER_STEP)):
        rows = slice(j * q, (j + 1) * q)
        u = u_ref[0, rows, :].astype(F32)
        xs = u[:, :SSM_INNER]
        bm = u[:, SSM_INNER:SSM_INNER + SSM_BC]
        cm = u[:, SSM_INNER + SSM_BC:]
        scal = _ssd_scalars(small_ref[0, rows, :], dtb_ref, alog_ref)
        y = yf_ref[0, rows, :] + _ssd_direction(True, xs, bm, cm, scal, state_ref)
        o_ref[0, rows, :] = _rms(y * _silu(z_ref[0, rows, :]), ng_ref[...]).astype(o_ref.dtype)


def _ssd_branch(xbc, small, z, conv_w, conv_b, dt_bias, a_log, d_skip, norm_g, n_ctx):
    bsz, t, _ = xbc.shape
    q = SSM_CHUNK
    rows = SSD_CHUNKS_PER_STEP * q
    assert t % rows == 0 and n_ctx % rows == 0
    n_steps = t // rows
    n_ctx_steps = n_ctx // rows
    hb = rows // HALO
    n_halo = t // HALO
    pad32 = LANE - 2 * SSM_HEADS
    dtb = jnp.pad(dt_bias.reshape(1, -1), ((0, 0), (0, pad32)))
    alog = jnp.pad(a_log.reshape(1, -1), ((0, 0), (0, pad32)))
    dskip = jnp.repeat(d_skip, SSM_HEAD_DIM).reshape(1, SSM_INNER)
    state = pltpu.VMEM((SSM_GROUPS, SSM_STATE, SSM_HPG * SSM_HEAD_DIM), F32)

    def block(width):
        return pl.BlockSpec((1, rows, width), lambda b, s: (b, s, 0))

    y_f, u = pl.pallas_call(
        functools.partial(_ssd_fwd_kernel, n_ctx_chunks=n_ctx // q, n_chunks=t // q),
        grid=(bsz, n_steps),
        in_specs=[block(SSM_CONV_DIM),
                  pl.BlockSpec((1, HALO, SSM_CONV_DIM), lambda b, s: (b, jnp.maximum(s * hb - 1, 0), 0)),
                  pl.BlockSpec((1, HALO, SSM_CONV_DIM),
                               lambda b, s: (b, jnp.minimum((s + 1) * hb, n_halo - 1), 0)),
                  block(LANE),
                  _resident((SSM_CONV, SSM_CONV_DIM)), _resident((1, SSM_CONV_DIM)),
                  _resident((1, LANE)), _resident((1, LANE)), _resident((1, SSM_INNER))],
        out_specs=[block(SSM_INNER), block(SSM_CONV_DIM)],
        out_shape=[jax.ShapeDtypeStruct((bsz, t, SSM_INNER), F32),
                   jax.ShapeDtypeStruct((bsz, t, SSM_CONV_DIM), BF16)],
        scratch_shapes=[state],
        compiler_params=_cparams(2),
        name="ssd_forward",
    )(xbc, xbc, xbc, small, conv_w, conv_b.reshape(1, -1), dtb, alog, dskip)

    def rblock(width):
        return pl.BlockSpec(
            (1, rows, width),
            lambda b, s: (b, jnp.where(s < n_ctx_steps, n_ctx_steps - 1 - s, n_steps + n_ctx_steps - 1 - s), 0))

    return pl.pallas_call(
        _ssd_bwd_kernel,
        grid=(bsz, n_steps),
        in_specs=[rblock(SSM_CONV_DIM), rblock(LANE), rblock(SSM_INNER), rblock(SSM_INNER),
                  _resident((1, LANE)), _resident((1, LANE)), _resident((1, SSM_INNER))],
        out_specs=rblock(SSM_INNER),
        out_shape=jax.ShapeDtypeStruct((bsz, t, SSM_INNER), BF16),
        scratch_shapes=[state],
        compiler_params=_cparams(2),
        name="ssd_backward",
    )(u, small, z, y_f, dtb, alog, norm_g.reshape(1, -1))


def _rope_tables(n_lat, n_ctx, rot_dim):
    n_freq = rot_dim // 4
    inv = jnp.power(ROPE_BASE, -jnp.arange(n_freq, dtype=F32) / n_freq)
    t = jnp.arange(n_lat)
    r = (t // GRID_W).astype(F32)[:, None] * inv
    col = (t % GRID_W).astype(F32)[:, None] * inv
    cos2 = jnp.concatenate([jnp.cos(r), jnp.cos(r), jnp.cos(col), jnp.cos(col)], axis=1)
    sin2 = jnp.concatenate([-jnp.sin(r), jnp.sin(r), -jnp.sin(col), jnp.sin(col)], axis=1)
    cos2 = jnp.concatenate([jnp.ones((n_ctx, rot_dim), F32), cos2], axis=0)
    sin2 = jnp.concatenate([jnp.zeros((n_ctx, rot_dim), F32), sin2], axis=0)
    return cos2, sin2


def _rotate_half(x):
    f = x.shape[-1] // 4
    return jnp.concatenate([x[:, f:2 * f], x[:, :f], x[:, 3 * f:], x[:, 2 * f:3 * f]], axis=1)


SWA_BLOCKS_PER_STEP = 6


def _swa_kernel(sink_ref, qt_ref, k_ref, vt_ref, o_ref, *, n_ctx, t):
    for j in range(SWA_BLOCKS_PER_STEP):
        _swa_block(sink_ref, qt_ref, k_ref, vt_ref, o_ref, j, n_ctx=n_ctx, t=t)


def _swa_block(sink_ref, qt_ref, k_ref, vt_ref, o_ref, j, *, n_ctx, t):
    hk = pl.program_id(1)
    blk = pl.program_id(2) * SWA_BLOCKS_PER_STEP + j
    bq, dh = SWA_BLOCK, SWA_HEAD_DIM
    win = 3 * bq
    ncb = n_ctx // bq
    nq = SWA_GRP * bq
    qt = qt_ref[0, :, j * bq:(j + 1) * bq]
    q4t =jnp.concatenate([qt[g * dh:(g + 1) * dh, :] for g in range(SWA_GRP)], axis=1)
    start = pl.multiple_of(jnp.clip((blk - 1) * bq, 0, t - win), bq)
    b0 = start // bq
    s_w = jnp.dot(k_ref[0, pl.ds(start, win), :], q4t, preferred_element_type=F32)
    s_c = jnp.dot(k_ref[0, 0:n_ctx, :], q4t, preferred_element_type=F32)
    rows = lax.broadcasted_iota(jnp.int32, (win, nq), 0)
    cols = lax.broadcasted_iota(jnp.int32, (win, nq), 1)
    qpos = (blk - ncb) * bq + (cols & (bq - 1))
    kpos = start - n_ctx + rows
    ok = (jnp.abs(kpos - qpos) <= SWA_WINDOW) & (kpos >= 0) & (blk >= ncb)
    s_w = jnp.where(ok, s_w, -jnp.inf)
    c1 = lax.broadcasted_iota(jnp.int32, (1, nq), 1)
    sink = jnp.zeros((1, nq), F32)
    for g in range(SWA_GRP):
        sink = jnp.where(c1 // bq == g, sink_ref[hk * SWA_GRP + g], sink)
    sink = sink * LOG2E
    m = jnp.maximum(jnp.maximum(jnp.max(s_w, axis=0, keepdims=True),
                                jnp.max(s_c, axis=0, keepdims=True)), sink)
    p_w = jnp.exp2(s_w - m).astype(BF16)
    p_c = jnp.exp2(s_c - m).astype(BF16)

    def weighted_values(first, nblk, p):
        acc = None
        u = 0
        while u < nblk:
            n = 2 if u + 1 < nblk else 1
            vt = vt_ref[0, 0, first + u]
            if n == 2:
                vt = jnp.concatenate([vt, vt_ref[0, 0, first + u + 1]], axis=1)
            term = jnp.dot(vt, p[u * bq:(u + n) * bq], preferred_element_type=F32)
            acc = term if acc is None else acc + term
            u += n
        return acc

    acc = weighted_values(b0, win // bq, p_w) + weighted_values(0, ncb, p_c)
    o = acc[0:dh] / (acc[dh:dh + 1] + jnp.exp2(sink - m))
    for g in range(SWA_GRP):
        o_ref[0, j * bq:(j + 1) * bq, g * dh:(g + 1) * dh] = o[:, g * bq:(g + 1) * bq].T.astype(o_ref.dtype)


def _swa_attention(qt, k, vt, sink, n_ctx):
    bsz, dq, t = qt.shape
    nb = SWA_BLOCKS_PER_STEP
    bq = SWA_BLOCK
    gw = SWA_GRP * SWA_HEAD_DIM
    assert (t // bq) % nb == 0
    return pl.pallas_call(
        functools.partial(_swa_kernel, n_ctx=n_ctx, t=t),
        grid=(bsz, SWA_KV_HEADS, t // (nb * bq)),
        in_specs=[pl.BlockSpec(memory_space=pltpu.SMEM),
                  pl.BlockSpec((1, gw, nb * bq), lambda b, h, n: (b, h, n)),
                  pl.BlockSpec((1, t, SWA_HEAD_DIM), lambda b, h, n: (b, 0, h)),
                  pl.BlockSpec((1, 1, t // bq, SWA_VT_ROWS, bq), lambda b, h, n: (b, h, 0, 0, 0))],
        out_specs=pl.BlockSpec((1, nb * bq, gw), lambda b, h, n: (b, n, h)),
        out_shape=jax.ShapeDtypeStruct((bsz, t, dq), BF16),
        compiler_params=_cparams(3),
        name="swa_attention",
    )(sink, qt, k, vt)


def _rms_cols(x, g):
    return x * lax.rsqrt(jnp.mean(x * x, axis=0, keepdims=True) + EPS) * g


def _rotate_half_rows(x):
    f = x.shape[0] // 4
    return jnp.concatenate([x[f:2 * f], x[:f], x[3 * f:], x[2 * f:3 * f]], axis=0)


MLA_HEADS_PER_STEP = 2
MLA_QBLOCKS_PER_STEP = 2


def _mla_kernel(qt_ref, k_ref, vt_ref, o_ref, s_scr, p_scr, a_scr, mx_scr, acc_scr, m_scr):
    heads = qt_ref.shape[1]
    n_chunks, kb = vt_ref.shape[2], vt_ref.shape[4]
    tq = TOKEN_BLOCK
    n_qb = qt_ref.shape[3] // tq
    total = n_qb * n_chunks

    def stage_scores(c, item, slot):
        qb, j = divmod(item, n_chunks)
        s = jnp.dot(k_ref[0, c, j * kb:(j + 1) * kb, :], qt_ref[0, c, :, qb * tq:(qb + 1) * tq],
                    preferred_element_type=F32)
        s_scr[c, slot] = s
        mx_scr[c, slot] = jnp.max(s, axis=0, keepdims=True)

    def stage_softmax(c, item, slot):
        qb = item // n_chunks
        m_prev = m_scr[c, qb]
        m_new = jnp.maximum(m_prev, mx_scr[c, slot])
        m_scr[c, qb] = m_new
        a_scr[c, slot] = jnp.exp2(m_prev - m_new)
        p_scr[c, slot] = jnp.exp2(s_scr[c, slot] - m_new).astype(BF16)

    def stage_values(c, item, slot):
        qb, j = divmod(item, n_chunks)
        pv = jnp.dot(vt_ref[0, c, j], p_scr[c, slot], preferred_element_type=F32)
        acc_scr[c, qb] = a_scr[c, slot] * acc_scr[c, qb] + pv

    m_scr[...] = jnp.full(m_scr.shape, -jnp.inf, F32)
    acc_scr[...] = jnp.zeros(acc_scr.shape, F32)
    for t in range(total + 2):
        parity = t % 2
        for c in range(heads):
            if t < total:
                stage_scores(c, t, parity)
            if 1 <= t <= total:
                stage_softmax(c, t - 1, 1 - parity)
            if t >= 2:
                stage_values(c, t - 2, parity)
    for c in range(heads):
        for qb in range(n_qb):
            acc = acc_scr[c, qb]
            o = acc[0:MLA_V] / acc[MLA_V:MLA_V + 1]
            o_ref[0, qb * tq:(qb + 1) * tq, c * MLA_V:(c + 1) * MLA_V] = o.T.astype(o_ref.dtype)


def _mla_ctx_kernel(qt_ref, k_ref, vt_ref, y_ref, o_ref):
    del y_ref
    kb = vt_ref.shape[4]
    for c in range(qt_ref.shape[1]):
        s = jnp.dot(k_ref[0, c], qt_ref[0, c], preferred_element_type=F32)
        p = jnp.exp2(s - jnp.max(s, axis=0, keepdims=True)).astype(BF16)
        acc = jnp.dot(vt_ref[0, c, 0], p[0:kb], preferred_element_type=F32)
        for u in range(1, vt_ref.shape[2]):
            acc = acc + jnp.dot(vt_ref[0, c, u], p[u * kb:(u + 1) * kb], preferred_element_type=F32)
        o = acc[0:MLA_V] / acc[MLA_V:MLA_V + 1]
        o_ref[0, :, c * MLA_V:(c + 1) * MLA_V] = o.T.astype(o_ref.dtype)


def _mla_attention(qt, k, vt, n_ctx, with_ctx):
    bsz, nh, dqk, t = qt.shape
    tq = TOKEN_BLOCK
    kb = vt.shape[-1]
    n_key_blocks = t // kb
    g = MLA_HEADS_PER_STEP
    nq = MLA_QBLOCKS_PER_STEP * tq
    n_lat = t - n_ctx
    assert n_lat % nq == 0 and n_lat % n_ctx == 0 and n_ctx % kb == 0
    y = pl.pallas_call(
        _mla_kernel,
        grid=(bsz, nh // g, n_lat // nq),
        in_specs=[pl.BlockSpec((1, g, dqk, nq), lambda b, h, i: (b, h, 0, i)),
                  pl.BlockSpec((1, g, t, dqk), lambda b, h, i: (b, h, 0, 0)),
                  pl.BlockSpec((1, g, n_key_blocks, MLA_VT_ROWS, kb), lambda b, h, i: (b, h, 0, 0, 0))],
        out_specs=pl.BlockSpec((1, nq, g * MLA_V), lambda b, h, i: (b, i, h)),
        out_shape=jax.ShapeDtypeStruct((bsz, t, nh * MLA_V), BF16),
        scratch_shapes=[pltpu.VMEM((g, 2, kb, tq), F32),
                        pltpu.VMEM((g, 2, kb, tq), BF16),
                        pltpu.VMEM((g, 2, 1, tq), F32),
                        pltpu.VMEM((g, 2, 1, tq), F32),
                        pltpu.VMEM((g, MLA_QBLOCKS_PER_STEP, MLA_VT_ROWS, tq), F32),
                        pltpu.VMEM((g, MLA_QBLOCKS_PER_STEP, 1, tq), F32)],
        compiler_params=_cparams(3),
        name="mla_attention",
    )(qt, k, vt)
    if not with_ctx:
        return y
    ctx_blk = n_lat // n_ctx
    ncb = n_ctx // kb
    return pl.pallas_call(
        _mla_ctx_kernel,
        grid=(bsz, nh // g),
        in_specs=[pl.BlockSpec((1, g, dqk, n_ctx), lambda b, h: (b, h, 0, ctx_blk)),
                  pl.BlockSpec((1, g, n_ctx, dqk), lambda b, h: (b, h, ctx_blk, 0)),
                  pl.BlockSpec((1, g, ncb, MLA_VT_ROWS, kb), lambda b, h: (b, h, ctx_blk, 0, 0)),
                  pl.BlockSpec(memory_space=pl.ANY)],
        out_specs=pl.BlockSpec((1, n_ctx, g * MLA_V), lambda b, h: (b, ctx_blk, h)),
        out_shape=jax.ShapeDtypeStruct(y.shape, y.dtype),
        input_output_aliases={3: 0},
        compiler_params=_cparams(2),
        name="mla_attention_ctx",
    )(qt, k, vt, y)


def _ffn_chunks(hidden):
    step = 512
    return tuple((lo, min(lo + step, hidden)) for lo in range(0, hidden, step))


def _merge_ffn_kernel(x_ref, mod_ref, gates_ref, ys_ref, yw_ref, ym_ref, wps_ref, wpw_ref, wpm_ref,
                      wo_ref, g2_ref, wfi_ref, wfo_ref, o_ref):
    d = x_ref.shape[-1]
    hidden = wfo_ref.shape[0]
    mod = mod_ref[0, 0]
    gt1, sh2, sc2, gt2 = mod[2:3], mod[3:4], mod[4:5], mod[5:6]
    gates = jax.nn.sigmoid(gates_ref[0])
    merged = (gates[:, 0:d] * jnp.dot(ys_ref[0], wps_ref[...], preferred_element_type=F32)
              + gates[:, d:2 * d] * jnp.dot(yw_ref[0], wpw_ref[...], preferred_element_type=F32)
              + gates[:, 2 * d:3 * d] * jnp.dot(ym_ref[0], wpm_ref[...], preferred_element_type=F32))
    x1 = x_ref[0] + gt1 * jnp.dot(merged.astype(BF16), wo_ref[...], preferred_element_type=F32)
    hb = (_rms(x1, g2_ref[...]) * (1.0 + sc2) + sh2).astype(BF16)
    acc = jnp.zeros_like(x1)
    for lo, hi in _ffn_chunks(hidden):
        gate = jnp.dot(hb, wfi_ref[:, lo:hi], preferred_element_type=F32)
        up = jnp.dot(hb, wfi_ref[:, hidden + lo:hidden + hi], preferred_element_type=F32)
        acc = acc + jnp.dot((_silu(gate) * up).astype(BF16), wfo_ref[lo:hi, :], preferred_element_type=F32)
    o_ref[0] = x1 + gt2 * acc


def _merge_ffn(x_all, modsel, gates, y_ssm, y_swa, y_mla, wps, wpw, wpm, wo, norm2_g, wfi, wfo,
               n_ctx_blocks, first_block):
    bsz, t, d = x_all.shape
    tm = TOKEN_BLOCK
    nblk = t // tm - first_block

    def tok(width):
        return pl.BlockSpec((1, tm, width), lambda b, i: (b, i + first_block, 0))

    return pl.pallas_call(
        _merge_ffn_kernel,
        grid=(bsz, nblk),
        in_specs=[tok(d),
                  pl.BlockSpec((1, 1, SUBLANE, d),
                               lambda b, i: (b, jnp.where(i + first_block < n_ctx_blocks, 0, 1), 0, 0)),
                  tok(N_BRANCH * d), tok(y_ssm.shape[-1]), tok(y_swa.shape[-1]),
                  pl.BlockSpec((1, tm, y_mla.shape[-1]),
                               lambda b, i: (b, _latent_first(i + first_block, n_blocks=t // tm,
                                                              n_ctx_blocks=n_ctx_blocks), 0)),
                  _resident(wps.shape), _resident(wpw.shape), _resident(wpm.shape), _resident(wo.shape),
                  _resident((1, d)), _resident(wfi.shape), _resident(wfo.shape)],
        out_specs=pl.BlockSpec((1, tm, d), lambda b, i: (b, i, 0)),
        out_shape=jax.ShapeDtypeStruct((bsz, nblk * tm, d), F32),
        compiler_params=_cparams(2),
        name="merge_ffn",
    )(x_all, modsel, gates, y_ssm, y_swa, y_mla, wps, wpw, wpm, wo, norm2_g.reshape(1, d), wfi, wfo)


def _head_major(w, n_heads, parts):
    k = w.shape[0]
    w = w.reshape(k, n_heads, sum(parts))
    out, lo = [], 0
    for p in parts:
        out.append(w[:, :, lo:lo + p].reshape(k, n_heads * p))
        lo += p
    return jnp.concatenate(out, axis=1)


def kernel(x, c, ctx, c_ctx, w_mod, b_mod, norm1_g, norm2_g, w_in, ssm_conv_w, ssm_conv_b, ssm_dt_bias,
           ssm_a_log, ssm_d, ssm_norm_g, swa_q_norm_g, swa_k_norm_g, swa_sink, mla_q_lat_g, mla_kv_lat_g,
           w_mla_uq, w_mla_ukv, mla_q_norm_g, mla_k_norm_g, w_p_ssm, w_p_swa, w_p_mla, w_out, w_ffn_in,
           w_ffn_out):
    bsz, n_lat, d = x.shape
    n_ctx = ctx.shape[1]
    depth = w_mod.shape[0]
    assert n_ctx % TOKEN_BLOCK == 0 and n_lat % TOKEN_BLOCK == 0 and n_lat % GRID_W == 0
    assert bsz + 1 <= SUBLANE
    n_ctx_blocks = n_ctx // TOKEN_BLOCK

    cvec = jnp.concatenate([c, c_ctx[None], jnp.zeros((SUBLANE - bsz - 1, d), F32)], axis=0)
    mods = _modulation(cvec, w_mod, b_mod).reshape(depth, SUBLANE, 6, d)
    rope_swa = _rope_tables(n_lat, n_ctx, SWA_HEAD_DIM)
    rope_mla = _rope_tables(n_lat, n_ctx, MLA_ROPE)

    x_all = jnp.concatenate([ctx, x], axis=1)
    for i in range(depth):
        last = i == depth - 1
        ctx_mod = jnp.broadcast_to(mods[i, bsz][None], (bsz, 6, d))
        modsel = jnp.pad(jnp.stack([ctx_mod, mods[i, :bsz]], axis=1), ((0, 0), (0, 0), (0, SUBLANE - 6), (0, 0)))

        xbc, small, z, gates, qs, ks, vs, qm, km, vm = _in_projection(
            x_all, modsel, norm1_g[i], w_in[i], w_mla_uq[i], w_mla_ukv[i], swa_q_norm_g[i], swa_k_norm_g[i],
            mla_q_lat_g[i], mla_kv_lat_g[i], mla_q_norm_g[i], mla_k_norm_g[i], rope_swa, rope_mla, n_ctx_blocks)

        y_ssm = _ssd_branch(xbc, small, z, ssm_conv_w[i], ssm_conv_b[i], ssm_dt_bias[i], ssm_a_log[i],
                            ssm_d[i], ssm_norm_g[i], n_ctx)
        y_swa = _swa_attention(qs, ks, vs, swa_sink[i], n_ctx)
        first_tok = n_ctx_blocks if last else 0
        y_mla = _mla_attention(qm, km, vm, n_ctx, with_ctx=not last)

        x_all = _merge_ffn(x_all, modsel, gates, y_ssm, y_swa, y_mla,
                           w_p_ssm[i].astype(BF16), w_p_swa[i].astype(BF16), w_p_mla[i].astype(BF16),
                           w_out[i].astype(BF16), norm2_g[i], w_ffn_in[i].astype(BF16),
                           w_ffn_out[i].astype(BF16), n_ctx_blocks, first_tok)
    return x_all
```

```python
import functools
import math

import jax
import jax.numpy as jnp
from jax import lax
from jax.experimental import pallas as pl
from jax.experimental.pallas import tpu as pltpu

F32 = jnp.float32
BF16 = jnp.bfloat16
HIGHEST = lax.Precision.HIGHEST

EPS = 1e-6
ROPE_BASE = 10000.0
GRID_W = 64

SSM_HEADS = 16
SSM_HEAD_DIM = 64
SSM_INNER = SSM_HEADS * SSM_HEAD_DIM
SSM_GROUPS = 2
SSM_STATE = 128
SSM_CONV = 5
SSM_CHUNK = 128
SSM_BC = SSM_GROUPS * SSM_STATE
SSM_CONV_DIM = SSM_INNER + 2 * SSM_BC
SSM_HPG = SSM_HEADS // SSM_GROUPS

SWA_Q_HEADS = 8
SWA_KV_HEADS = 2
SWA_HEAD_DIM = 128
SWA_WINDOW = 128
SWA_BLOCK = 128
SWA_GRP = SWA_Q_HEADS // SWA_KV_HEADS

MLA_HEADS = 8
MLA_Q_RANK = 384
MLA_KV_RANK = 256
MLA_NOPE = 128
MLA_ROPE = 64
MLA_QK = MLA_NOPE + MLA_ROPE
MLA_V = 128
BF16_SUBLANES = 16
MLA_VT_ROWS = MLA_V + BF16_SUBLANES

N_BRANCH = 3
LANE = 128
SUBLANE = 8
HALO = SUBLANE
TOKEN_BLOCK = 256
VMEM_LIMIT = 56 * 1024 * 1024


def _cparams(n_axes):
    return pltpu.CompilerParams(
        dimension_semantics=("arbitrary",) * n_axes, vmem_limit_bytes=VMEM_LIMIT)


def _resident(shape):
    nd = len(shape)
    return pl.BlockSpec(shape, lambda *_: (0,) * nd, pipeline_mode=pl.Buffered(1))


def _rms(x, g):
    return x * lax.rsqrt(jnp.mean(x * x, axis=-1, keepdims=True) + EPS) * g


def _silu(x):
    return x * jax.nn.sigmoid(x)


def _xdot(a, b):
    return jnp.dot(a, b, precision=HIGHEST, preferred_element_type=F32)


def _mod_kernel(c_ref, w_ref, b_ref, o_ref):
    o_ref[0] = _xdot(_silu(c_ref[...]), w_ref[0]) + b_ref[0]


def _modulation(cvec, w_mod, b_mod):
    depth, d, d6 = w_mod.shape
    rows = cvec.shape[0]
    return pl.pallas_call(
        _mod_kernel,
        grid=(depth, d6 // d),
        in_specs=[pl.BlockSpec((rows, d), lambda i, j: (0, 0)),
                  pl.BlockSpec((1, d, d), lambda i, j: (i, 0, j)),
                  pl.BlockSpec((1, 1, d), lambda i, j: (i, 0, j))],
        out_specs=pl.BlockSpec((1, rows, d), lambda i, j: (i, 0, j)),
        out_shape=jax.ShapeDtypeStruct((depth, rows, d6), F32),
        compiler_params=_cparams(2),
        name="modulation",
    )(cvec, w_mod, b_mod.reshape(depth, 1, d6))


SWA_DQ = SWA_Q_HEADS * SWA_HEAD_DIM
SWA_DKV = SWA_KV_HEADS * SWA_HEAD_DIM
SWA_VT_ROWS = SWA_HEAD_DIM + BF16_SUBLANES
LOG2E = math.log2(math.e)


def _split_in_weights(w_in_l, d):
    o = [0]
    for w in (SSM_CONV_DIM, 2 * SSM_HEADS, SWA_DKV, SWA_DKV, MLA_KV_RANK, MLA_ROPE, SSM_INNER, SWA_DQ,
              MLA_Q_RANK, N_BRANCH * d):
        o.append(o[-1] + w)
    xbc, dt, k, v, ckv, kr, z, q, cq, gates = (w_in_l[:, o[i]:o[i + 1]] for i in range(10))
    pad = jnp.zeros((d, LANE - 2 * SSM_HEADS - MLA_ROPE), w_in_l.dtype)
    w_tok = jnp.concatenate([xbc, dt, kr, pad, k, ckv, z, gates], axis=1).astype(BF16)
    w_feat = jnp.concatenate([q, v, cq], axis=1).T.astype(BF16)
    return w_tok, w_feat


def _latent_first(i, *, n_blocks, n_ctx_blocks):
    return jnp.where(i < n_ctx_blocks, n_blocks - n_ctx_blocks + i, i - n_ctx_blocks)


def _inproj_kernel(x_ref, mod_ref, g_ref, wtok_ref, wfeat_ref, wqt_ref, wk_ref, wvt_ref,
                   sqg_ref, skg_ref, qlg_ref, kvlg_ref, qgn_ref, qgr_ref, kgn_ref, kgr_ref,
                   scos_ref, ssin_ref, scost_ref, ssint_ref, mcos_ref, msin_ref, mcost_ref, msint_ref,
                   xbc_ref, small_ref, z_ref, gates_ref, sq_ref, sk_ref, sv_ref, mq_ref, mk_ref, mv_ref):
    d = x_ref.shape[-1]
    tm = x_ref.shape[1]
    mod = mod_ref[0, 0]
    hb = (_rms(x_ref[0], g_ref[...]) * (1.0 + mod[1:2]) + mod[0:1]).astype(BF16)

    def tok(lo, width):
        return jnp.dot(hb, wtok_ref[:, lo:lo + width], preferred_element_type=F32)

    o_small = SSM_CONV_DIM
    o_k = o_small + LANE
    o_ckv = o_k + SWA_DKV
    o_z = o_ckv + MLA_KV_RANK
    o_gates = o_z + SSM_INNER
    dh = SWA_HEAD_DIM
    nh, dn, dr, dv = MLA_HEADS, MLA_NOPE, MLA_ROPE, MLA_V
    qscale = MLA_QK ** -0.5 * LOG2E
    o_cq = SWA_DQ + SWA_DKV

    feat = lax.dot_general(wfeat_ref[...], hb, (((1,), (1,)), ((), ())), preferred_element_type=F32)
    small = tok(o_small, LANE)
    small_ref[0] = small
    k_swa = tok(o_k, SWA_DKV)
    ckvn = _rms(tok(o_ckv, MLA_KV_RANK), kvlg_ref[...]).astype(BF16)
    cqn_t = _rms_cols(feat[o_cq:o_cq + MLA_Q_RANK], qlg_ref[...]).astype(BF16)
    qf_t = jnp.dot(wqt_ref[...], cqn_t, preferred_element_type=F32)

    def swa_queries(heads):
        scos_t, ssin_t = scost_ref[...], ssint_ref[...]
        for h in heads:
            xq = _rms_cols(feat[h * dh:(h + 1) * dh], sqg_ref[...])
            xq = xq * scos_t + _rotate_half_rows(xq) * ssin_t
            sq_ref[0, h * dh:(h + 1) * dh, :] = (xq * (dh ** -0.5 * LOG2E)).astype(sq_ref.dtype)

    def swa_keys_values():
        scos, ssin = scos_ref[...], ssin_ref[...]
        for h in range(SWA_KV_HEADS):
            vt = feat[SWA_DQ + h * dh:SWA_DQ + (h + 1) * dh]
            for u in range(tm // SWA_BLOCK):
                sv_ref[0, h, u, 0:dh, :] = vt[:, u * SWA_BLOCK:(u + 1) * SWA_BLOCK].astype(sv_ref.dtype)
                sv_ref[0, h, u, dh:, :] = jnp.ones((SWA_VT_ROWS - dh, SWA_BLOCK), sv_ref.dtype)
            xk = _rms(k_swa[:, h * dh:(h + 1) * dh], skg_ref[...])
            xk = xk * scos + _rotate_half(xk) * ssin
            sk_ref[0, :, h * dh:(h + 1) * dh] = xk.astype(sk_ref.dtype)

    def mla_queries(heads):
        mcos_t, msin_t = mcost_ref[...], msint_ref[...]
        for h in heads:
            qn = _rms_cols(qf_t[h * dn:(h + 1) * dn], qgn_ref[...])
            qr = _rms_cols(qf_t[nh * dn + h * dr:nh * dn + (h + 1) * dr], qgr_ref[...])
            qr = qr * mcos_t + _rotate_half_rows(qr) * msin_t
            mq_ref[0, h, 0:dn, :] = (qn * qscale).astype(mq_ref.dtype)
            mq_ref[0, h, dn:dn + dr, :] = (qr * qscale).astype(mq_ref.dtype)

    def mla_keys():
        kf = jnp.dot(ckvn, wk_ref[...], preferred_element_type=F32)
        kr0 = 2 * SSM_HEADS
        kr = _rms(small[:, kr0:kr0 + dr], kgr_ref[...])
        kr = kr * mcos_ref[...] + _rotate_half(kr) * msin_ref[...]
        for h in range(nh):
            kn = _rms(kf[:, h * dn:(h + 1) * dn], kgn_ref[...])
            mk_ref[0, h] = jnp.concatenate([kn, kr], axis=1).astype(mk_ref.dtype)

    def mla_values():
        vf_t = lax.dot_general(wvt_ref[...], ckvn, (((1,), (1,)), ((), ())), preferred_element_type=F32)
        for h in range(nh):
            mv_ref[0, h, 0, 0:dv, :] = vf_t[h * dv:(h + 1) * dv].astype(mv_ref.dtype)
            mv_ref[0, h, 0, dv:, :] = jnp.ones((MLA_VT_ROWS - dv, tm), mv_ref.dtype)

    half_q = SWA_Q_HEADS // 2
    half_c = SSM_CONV_DIM // 2
    xbc_ref[0, :, 0:half_c] = tok(0, half_c)
    swa_queries(range(0, half_q))
    xbc_ref[0, :, half_c:] = tok(half_c, SSM_CONV_DIM - half_c)
    swa_queries(range(half_q, SWA_Q_HEADS))
    z_ref[0] = tok(o_z, SSM_INNER)
    swa_keys_values()
    gates_ref[0, :, 0:d] = tok(o_gates, d)
    mla_queries(range(0, nh // 2))
    gates_ref[0, :, d:2 * d] = tok(o_gates + d, d)
    mla_queries(range(nh // 2, nh))
    gates_ref[0, :, 2 * d:] = tok(o_gates + 2 * d, d)
    mla_keys()
    mla_values()


def _in_projection(x_all, modsel, norm_g, w_in_l, w_uq, w_ukv, swa_q_g, swa_k_g, q_lat_g, kv_lat_g,
                   mla_q_g, mla_k_g, rope_swa, rope_mla, n_ctx_blocks):
    bsz, t, d = x_all.shape
    tm = TOKEN_BLOCK
    nh = MLA_HEADS
    w_tok, w_feat = _split_in_weights(w_in_l, d)
    wq_t = _head_major(w_uq, nh, (MLA_NOPE, MLA_ROPE)).T.astype(BF16)
    wkv = _head_major(w_ukv, nh, (MLA_NOPE, MLA_V))
    wk = wkv[:, :nh * MLA_NOPE].astype(BF16)
    wv_t = wkv[:, nh * MLA_NOPE:].T.astype(BF16)
    scos, ssin = rope_swa
    mcos, msin = rope_mla

    def cols(g):
        return jnp.broadcast_to(g[:, None], (g.shape[0], tm))

    def row(g):
        return g.reshape(1, -1)

    def tok(width):
        return pl.BlockSpec((1, tm, width), lambda b, i: (b, i, 0))

    def tok_table(width):
        return pl.BlockSpec((tm, width), lambda b, i: (i, 0))

    def feat_table(height):
        return pl.BlockSpec((height, tm), lambda b, i: (0, i))

    consts = [row(norm_g), w_tok, w_feat, wq_t, wk, wv_t,
              cols(swa_q_g), row(swa_k_g), cols(q_lat_g), row(kv_lat_g),
              cols(mla_q_g[:MLA_NOPE]), cols(mla_q_g[MLA_NOPE:]), row(mla_k_g[:MLA_NOPE]), row(mla_k_g[MLA_NOPE:])]
    nkb = tm // SWA_BLOCK
    mla_pos = functools.partial(_latent_first, n_blocks=t // tm, n_ctx_blocks=n_ctx_blocks)
    return pl.pallas_call(
        _inproj_kernel,
        grid=(bsz, t // tm),
        in_specs=[tok(d),
                  pl.BlockSpec((1, 1, SUBLANE, d), lambda b, i: (b, jnp.where(i < n_ctx_blocks, 0, 1), 0, 0))]
                 + [_resident(a.shape) for a in consts]
                 + [tok_table(SWA_HEAD_DIM), tok_table(SWA_HEAD_DIM), feat_table(SWA_HEAD_DIM),
                    feat_table(SWA_HEAD_DIM), tok_table(MLA_ROPE), tok_table(MLA_ROPE), feat_table(MLA_ROPE),
                    feat_table(MLA_ROPE)],
        out_specs=[tok(SSM_CONV_DIM), tok(LANE), tok(SSM_INNER), tok(N_BRANCH * d),
                   pl.BlockSpec((1, SWA_DQ, tm), lambda b, i: (b, 0, i)),
                   tok(SWA_DKV),
                   pl.BlockSpec((1, SWA_KV_HEADS, nkb, SWA_VT_ROWS, SWA_BLOCK), lambda b, i: (b, 0, i, 0, 0)),
                   pl.BlockSpec((1, nh, MLA_QK, tm), lambda b, i: (b, 0, 0, mla_pos(i))),
                   pl.BlockSpec((1, nh, tm, MLA_QK), lambda b, i: (b, 0, mla_pos(i), 0)),
                   pl.BlockSpec((1, nh, 1, MLA_VT_ROWS, tm), lambda b, i: (b, 0, mla_pos(i), 0, 0))],
        out_shape=[jax.ShapeDtypeStruct((bsz, t, SSM_CONV_DIM), F32),
                   jax.ShapeDtypeStruct((bsz, t, LANE), F32),
                   jax.ShapeDtypeStruct((bsz, t, SSM_INNER), F32),
                   jax.ShapeDtypeStruct((bsz, t, N_BRANCH * d), F32),
                   jax.ShapeDtypeStruct((bsz, SWA_DQ, t), BF16),
                   jax.ShapeDtypeStruct((bsz, t, SWA_DKV), BF16),
                   jax.ShapeDtypeStruct((bsz, SWA_KV_HEADS, t // SWA_BLOCK, SWA_VT_ROWS, SWA_BLOCK), BF16),
                   jax.ShapeDtypeStruct((bsz, nh, MLA_QK, t), BF16),
                   jax.ShapeDtypeStruct((bsz, nh, t, MLA_QK), BF16),
                   jax.ShapeDtypeStruct((bsz, nh, t // tm, MLA_VT_ROWS, tm), BF16)],
        compiler_params=_cparams(2),
        name="in_projection",
    )(x_all, modsel, *consts, scos, ssin, scos.T, ssin.T, mcos, msin, mcos.T, msin.T)


def _ssd_scalars(small, dtb_ref, alog_ref):
    q = small.shape[0]
    lane = lax.broadcasted_iota(jnp.int32, (1, LANE), 1)
    raw = small + dtb_ref[...]
    dts = jnp.maximum(raw, 0.0) + jnp.log1p(jnp.exp(-jnp.abs(raw)))
    a = jnp.where(lane < 2 * SSM_HEADS, -jnp.exp(alog_ref[...]), 0.0)
    dta = dts * a
    ri = lax.broadcasted_iota(jnp.int32, (q, q), 0)
    ci = lax.broadcasted_iota(jnp.int32, (q, q), 1)
    tri = (ci <= ri).astype(BF16)
    parts = jnp.dot(tri, _split3(dta), preferred_element_type=F32)
    acs = parts[:, :LANE] + parts[:, LANE:2 * LANE] + parts[:, 2 * LANE:]
    ecs = acs - dta
    return dts, acs, ecs, dts.T, acs.T, ecs.T


def _split3(x):
    hi = x.astype(BF16)
    r1 = x - hi.astype(F32)
    mid = r1.astype(BF16)
    lo = (r1 - mid.astype(F32)).astype(BF16)
    return jnp.concatenate([hi, mid, lo], axis=1)


def _expand_matrix(first_row):
    r = lax.broadcasted_iota(jnp.int32, (3 * LANE, SSM_INNER), 0)
    c = lax.broadcasted_iota(jnp.int32, (3 * LANE, SSM_INNER), 1)
    return (c // SSM_HEAD_DIM + first_row == (r & (LANE - 1))).astype(BF16)


def _ssd_direction(backward, xs, bm, cm, scal, state_ref):
    dts, acs, ecs, dts_t, acs_t, ecs_t = scal
    q = xs.shape[0]
    base = SSM_HEADS if backward else 0
    expand = _expand_matrix(base)
    tot = acs[q - 1:q, :]
    if backward:
        dec_in = jnp.exp(tot - ecs)
        w_out = jnp.exp(ecs) * dts
        pos, pos_t = ecs, ecs_t
    else:
        dec_in = jnp.exp(acs)
        w_out = jnp.exp(tot - acs) * dts
        pos, pos_t = acs, acs_t
    stacked = jnp.concatenate([dec_in, w_out, jnp.broadcast_to(jnp.exp(tot), (SUBLANE, LANE))], axis=0)
    expanded = jnp.dot(_split3(stacked), expand, preferred_element_type=F32)
    dec_e, w_e, tot_e = expanded[0:q], expanded[q:2 * q], expanded[2 * q:2 * q + 1]
    xw = (xs * w_e).astype(BF16)
    xb = xs.astype(BF16)
    ri = lax.broadcasted_iota(jnp.int32, (q, q), 0)
    ci = lax.broadcasted_iota(jnp.int32, (q, q), 1)
    keep = (ci >= ri) if backward else (ci <= ri)
    gw = SSM_HPG * SSM_HEAD_DIM
    ys = []
    for g in range(SSM_GROUPS):
        b_g = bm[:, g * SSM_STATE:(g + 1) * SSM_STATE]
        c_g = cm[:, g * SSM_STATE:(g + 1) * SSM_STATE].astype(BF16)
        b_t = b_g.T.astype(BF16)
        cb = jnp.dot(c_g, b_t, preferred_element_type=F32)
        st = state_ref[g]
        y_off = jnp.dot(c_g, st.astype(BF16), preferred_element_type=F32) * dec_e[:, g * gw:(g + 1) * gw]
        state_ref[g] = st * tot_e[:, g * gw:(g + 1) * gw] + jnp.dot(
            b_t, xw[:, g * gw:(g + 1) * gw], preferred_element_type=F32)
        heads = []
        for hh in range(SSM_HPG):
            h = g * SSM_HPG + hh
            col = pos[:, base + h:base + h + 1]
            row = pos_t[base + h:base + h + 1, :]
            diff = (row - col) if backward else (col - row)
            seg = jnp.exp(jnp.where(keep, diff, -jnp.inf))
            m = (cb * seg * dts_t[base + h:base + h + 1, :]).astype(BF16)
            heads.append(jnp.dot(m, xb[:, h * SSM_HEAD_DIM:(h + 1) * SSM_HEAD_DIM],
                                 preferred_element_type=F32))
        ys.append(jnp.concatenate(heads, axis=1) + y_off)
    return jnp.concatenate(ys, axis=1)


SSD_FWD_CHUNKS_PER_STEP = 6
SSD_BWD_CHUNKS_PER_STEP = 2


def _ssd_fwd_kernel(xc_ref, xp_ref, xn_ref, small_ref, cw_ref, cb_ref, dtb_ref, alog_ref, dskip_ref,
                    y_ref, u_ref, state_ref, *, n_ctx_chunks, n_chunks):
    step = pl.program_id(1)

    @pl.when(step == 0)
    def _():
        state_ref[...] = jnp.zeros_like(state_ref)

    q = SSM_CHUNK
    half = SSM_CONV // 2
    x_all = xc_ref[0]
    n_local = x_all.shape[0] // q
    for j in range(n_local):
        c = step * n_local + j
        rows = slice(j * q, (j + 1) * q)
        prev_ok = jnp.logical_and(c != 0, c != n_ctx_chunks)
        next_ok = jnp.logical_and(c != n_ctx_chunks - 1, c != n_chunks - 1)
        xp = xp_ref[0] if j == 0 else x_all[j * q - HALO:j * q]
        xn = xn_ref[0] if j == n_local - 1 else x_all[(j + 1) * q:(j + 1) * q + HALO]
        xc = x_all[rows]
        ext = jnp.concatenate([jnp.where(prev_ok, xp, 0.0), xc, jnp.where(next_ok, xn, 0.0)], axis=0)
        acc = jnp.zeros_like(xc) + cb_ref[...]
        for k in range(SSM_CONV):
            lo = HALO - half + k
            acc = acc + ext[lo:lo + q, :] * cw_ref[k:k + 1, :]
        u = _silu(acc)
        u_ref[0, rows, :] = u.astype(u_ref.dtype)
        xs = u[:, :SSM_INNER]
        bm = u[:, SSM_INNER:SSM_INNER + SSM_BC]
        cm = u[:, SSM_INNER + SSM_BC:]
        scal = _ssd_scalars(small_ref[0, rows, :], dtb_ref, alog_ref)
        y = _ssd_direction(False, xs, bm, cm, scal, state_ref)
        y_ref[0, rows, :] = y + dskip_ref[...] * xs


def _ssd_bwd_kernel(u_ref, small_ref, z_ref, yf_ref, dtb_ref, alog_ref, ng_ref, o_ref, state_ref):
    @pl.when(pl.program_id(1) == 0)
    def _():
        state_ref[...] = jnp.zeros_like(state_ref)

    q = SSM_CHUNK
    for j in reversed(range(u_ref.shape[1] // q)):
        rows = slice(j * q, (j + 1) * q)
        u = u_ref[0, rows, :].astype(F32)
        xs = u[:, :SSM_INNER]
        bm = u[:, SSM_INNER:SSM_INNER + SSM_BC]
        cm = u[:, SSM_INNER + SSM_BC:]
        scal = _ssd_scalars(small_ref[0, rows, :], dtb_ref, alog_ref)
        y = yf_ref[0, rows, :] + _ssd_direction(True, xs, bm, cm, scal, state_ref)
        o_ref[0, rows, :] = _rms(y * _silu(z_ref[0, rows, :]), ng_ref[...]).astype(o_ref.dtype)


def _ssd_branch(xbc, small, z, conv_w, conv_b, dt_bias, a_log, d_skip, norm_g, n_ctx):
    bsz, t, _ = xbc.shape
    q = SSM_CHUNK
    rows_f = SSD_FWD_CHUNKS_PER_STEP * q
    rows = SSD_BWD_CHUNKS_PER_STEP * q
    assert t % rows_f == 0 and t % rows == 0 and n_ctx % rows == 0
    n_steps = t // rows
    n_ctx_steps = n_ctx // rows
    hb = rows_f // HALO
    n_halo = t // HALO
    pad32 = LANE - 2 * SSM_HEADS
    dtb = jnp.pad(dt_bias.reshape(1, -1), ((0, 0), (0, pad32)))
    alog = jnp.pad(a_log.reshape(1, -1), ((0, 0), (0, pad32)))
    dskip = jnp.repeat(d_skip, SSM_HEAD_DIM).reshape(1, SSM_INNER)
    state = pltpu.VMEM((SSM_GROUPS, SSM_STATE, SSM_HPG * SSM_HEAD_DIM), F32)

    def block(width):
        return pl.BlockSpec((1, rows_f, width), lambda b, s: (b, s, 0))

    y_f, u = pl.pallas_call(
        functools.partial(_ssd_fwd_kernel, n_ctx_chunks=n_ctx // q, n_chunks=t // q),
        grid=(bsz, t // rows_f),
        in_specs=[block(SSM_CONV_DIM),
                  pl.BlockSpec((1, HALO, SSM_CONV_DIM), lambda b, s: (b, jnp.maximum(s * hb - 1, 0), 0)),
                  pl.BlockSpec((1, HALO, SSM_CONV_DIM),
                               lambda b, s: (b, jnp.minimum((s + 1) * hb, n_halo - 1), 0)),
                  block(LANE),
                  _resident((SSM_CONV, SSM_CONV_DIM)), _resident((1, SSM_CONV_DIM)),
                  _resident((1, LANE)), _resident((1, LANE)), _resident((1, SSM_INNER))],
        out_specs=[block(SSM_INNER), block(SSM_CONV_DIM)],
        out_shape=[jax.ShapeDtypeStruct((bsz, t, SSM_INNER), F32),
                   jax.ShapeDtypeStruct((bsz, t, SSM_CONV_DIM), BF16)],
        scratch_shapes=[state],
        compiler_params=_cparams(2),
        name="ssd_forward",
    )(xbc, xbc, xbc, small, conv_w, conv_b.reshape(1, -1), dtb, alog, dskip)

    def rblock(width):
        return pl.BlockSpec(
            (1, rows, width),
            lambda b, s: (b, jnp.where(s < n_ctx_steps, n_ctx_steps - 1 - s, n_steps + n_ctx_steps - 1 - s), 0))

    return pl.pallas_call(
        _ssd_bwd_kernel,
        grid=(bsz, n_steps),
        in_specs=[rblock(SSM_CONV_DIM), rblock(LANE), rblock(SSM_INNER), rblock(SSM_INNER),
                  _resident((1, LANE)), _resident((1, LANE)), _resident((1, SSM_INNER))],
        out_specs=rblock(SSM_INNER),
        out_shape=jax.ShapeDtypeStruct((bsz, t, SSM_INNER), BF16),
        scratch_shapes=[state],
        compiler_params=_cparams(2),
        name="ssd_backward",
    )(u, small, z, y_f, dtb, alog, norm_g.reshape(1, -1))


def _rope_tables(n_lat, n_ctx, rot_dim):
    n_freq = rot_dim // 4
    inv = jnp.power(ROPE_BASE, -jnp.arange(n_freq, dtype=F32) / n_freq)
    t = jnp.arange(n_lat)
    r = (t // GRID_W).astype(F32)[:, None] * inv
    col = (t % GRID_W).astype(F32)[:, None] * inv
    cos2 = jnp.concatenate([jnp.cos(r), jnp.cos(r), jnp.cos(col), jnp.cos(col)], axis=1)
    sin2 = jnp.concatenate([-jnp.sin(r), jnp.sin(r), -jnp.sin(col), jnp.sin(col)], axis=1)
    cos2 = jnp.concatenate([jnp.ones((n_ctx, rot_dim), F32), cos2], axis=0)
    sin2 = jnp.concatenate([jnp.zeros((n_ctx, rot_dim), F32), sin2], axis=0)
    return cos2, sin2


def _rotate_half(x):
    f = x.shape[-1] // 4
    return jnp.concatenate([x[:, f:2 * f], x[:, :f], x[:, 3 * f:], x[:, 2 * f:3 * f]], axis=1)


SWA_BLOCKS_PER_STEP = 6


def _swa_kernel(sink_ref, qt_ref, k_ref, vt_ref, o_ref, *, n_ctx, t):
    for j in range(SWA_BLOCKS_PER_STEP):
        _swa_block(sink_ref, qt_ref, k_ref, vt_ref, o_ref, j, n_ctx=n_ctx, t=t)


def _swa_block(sink_ref, qt_ref, k_ref, vt_ref, o_ref, j, *, n_ctx, t):
    hk = pl.program_id(1)
    blk = pl.program_id(2) * SWA_BLOCKS_PER_STEP + j
    bq, dh = SWA_BLOCK, SWA_HEAD_DIM
    win = 3 * bq
    ncb = n_ctx // bq
    nq = SWA_GRP * bq
    qt = qt_ref[0, :, j * bq:(j + 1) * bq]
    q4t =jnp.concatenate([qt[g * dh:(g + 1) * dh, :] for g in range(SWA_GRP)], axis=1)
    start = pl.multiple_of(jnp.clip((blk - 1) * bq, 0, t - win), bq)
    b0 = start // bq
    s_w = jnp.dot(k_ref[0, pl.ds(start, win), :], q4t, preferred_element_type=F32)
    s_c = jnp.dot(k_ref[0, 0:n_ctx, :], q4t, preferred_element_type=F32)
    rows = lax.broadcasted_iota(jnp.int32, (win, nq), 0)
    cols = lax.broadcasted_iota(jnp.int32, (win, nq), 1)
    qpos = (blk - ncb) * bq + (cols & (bq - 1))
    kpos = start - n_ctx + rows
    ok = (jnp.abs(kpos - qpos) <= SWA_WINDOW) & (kpos >= 0) & (blk >= ncb)
    s_w = jnp.where(ok, s_w, -jnp.inf)
    c1 = lax.broadcasted_iota(jnp.int32, (1, nq), 1)
    sink = jnp.zeros((1, nq), F32)
    for g in range(SWA_GRP):
        sink = jnp.where(c1 // bq == g, sink_ref[hk * SWA_GRP + g], sink)
    sink = sink * LOG2E
    m = jnp.maximum(jnp.maximum(jnp.max(s_w, axis=0, keepdims=True),
                                jnp.max(s_c, axis=0, keepdims=True)), sink)
    p_w = jnp.exp2(s_w - m).astype(BF16)
    p_c = jnp.exp2(s_c - m).astype(BF16)

    def weighted_values(first, nblk, p):
        acc = None
        u = 0
        while u < nblk:
            n = 2 if u + 1 < nblk else 1
            vt = vt_ref[0, 0, first + u]
            if n == 2:
                vt = jnp.concatenate([vt, vt_ref[0, 0, first + u + 1]], axis=1)
            term = jnp.dot(vt, p[u * bq:(u + n) * bq], preferred_element_type=F32)
            acc = term if acc is None else acc + term
            u += n
        return acc

    acc = weighted_values(b0, win // bq, p_w) + weighted_values(0, ncb, p_c)
    o = acc[0:dh] / (acc[dh:dh + 1] + jnp.exp2(sink - m))
    for g in range(SWA_GRP):
        o_ref[0, j * bq:(j + 1) * bq, g * dh:(g + 1) * dh] = o[:, g * bq:(g + 1) * bq].T.astype(o_ref.dtype)


def _swa_attention(qt, k, vt, sink, n_ctx):
    bsz, dq, t = qt.shape
    nb = SWA_BLOCKS_PER_STEP
    bq = SWA_BLOCK
    gw = SWA_GRP * SWA_HEAD_DIM
    assert (t // bq) % nb == 0
    return pl.pallas_call(
        functools.partial(_swa_kernel, n_ctx=n_ctx, t=t),
        grid=(bsz, SWA_KV_HEADS, t // (nb * bq)),
        in_specs=[pl.BlockSpec(memory_space=pltpu.SMEM),
                  pl.BlockSpec((1, gw, nb * bq), lambda b, h, n: (b, h, n)),
                  pl.BlockSpec((1, t, SWA_HEAD_DIM), lambda b, h, n: (b, 0, h)),
                  pl.BlockSpec((1, 1, t // bq, SWA_VT_ROWS, bq), lambda b, h, n: (b, h, 0, 0, 0))],
        out_specs=pl.BlockSpec((1, nb * bq, gw), lambda b, h, n: (b, n, h)),
        out_shape=jax.ShapeDtypeStruct((bsz, t, dq), BF16),
        compiler_params=_cparams(3),
        name="swa_attention",
    )(sink, qt, k, vt)


def _rms_cols(x, g):
    return x * lax.rsqrt(jnp.mean(x * x, axis=0, keepdims=True) + EPS) * g


def _rotate_half_rows(x):
    f = x.shape[0] // 4
    return jnp.concatenate([x[f:2 * f], x[:f], x[3 * f:], x[2 * f:3 * f]], axis=0)


MLA_HEADS_PER_STEP = 2
MLA_QBLOCKS_PER_STEP = 2


def _mla_kernel(qt_ref, k_ref, vt_ref, o_ref, s_scr, p_scr, a_scr, mx_scr, acc_scr, m_scr):
    heads = qt_ref.shape[1]
    n_chunks, kb = vt_ref.shape[2], vt_ref.shape[4]
    tq = TOKEN_BLOCK
    n_qb = qt_ref.shape[3] // tq
    total = n_qb * n_chunks

    def stage_scores(c, item, slot):
        qb, j = divmod(item, n_chunks)
        s = jnp.dot(k_ref[0, c, j * kb:(j + 1) * kb, :], qt_ref[0, c, :, qb * tq:(qb + 1) * tq],
                    preferred_element_type=F32)
        s_scr[c, slot] = s
        mx_scr[c, slot] = jnp.max(s, axis=0, keepdims=True)

    def stage_softmax(c, item, slot):
        qb = item // n_chunks
        m_prev = m_scr[c, qb]
        m_new = jnp.maximum(m_prev, mx_scr[c, slot])
        m_scr[c, qb] = m_new
        a_scr[c, slot] = jnp.exp2(m_prev - m_new)
        p_scr[c, slot] = jnp.exp2(s_scr[c, slot] - m_new).astype(BF16)

    def stage_values(c, item, slot):
        qb, j = divmod(item, n_chunks)
        pv = jnp.dot(vt_ref[0, c, j], p_scr[c, slot], preferred_element_type=F32)
        acc_scr[c, qb] = a_scr[c, slot] * acc_scr[c, qb] + pv

    m_scr[...] = jnp.full(m_scr.shape, -jnp.inf, F32)
    acc_scr[...] = jnp.zeros(acc_scr.shape, F32)
    for t in range(total + 2):
        parity = t % 2
        for c in range(heads):
            if t < total:
                stage_scores(c, t, parity)
            if 1 <= t <= total:
                stage_softmax(c, t - 1, 1 - parity)
            if t >= 2:
                stage_values(c, t - 2, parity)
    for c in range(heads):
        for qb in range(n_qb):
            acc = acc_scr[c, qb]
            o = acc[0:MLA_V] / acc[MLA_V:MLA_V + 1]
            o_ref[0, qb * tq:(qb + 1) * tq, c * MLA_V:(c + 1) * MLA_V] = o.T.astype(o_ref.dtype)


def _mla_ctx_kernel(qt_ref, k_ref, vt_ref, y_ref, o_ref):
    del y_ref
    kb = vt_ref.shape[4]
    for c in range(qt_ref.shape[1]):
        s = jnp.dot(k_ref[0, c], qt_ref[0, c], preferred_element_type=F32)
        p = jnp.exp2(s - jnp.max(s, axis=0, keepdims=True)).astype(BF16)
        acc = jnp.dot(vt_ref[0, c, 0], p[0:kb], preferred_element_type=F32)
        for u in range(1, vt_ref.shape[2]):
            acc = acc + jnp.dot(vt_ref[0, c, u], p[u * kb:(u + 1) * kb], preferred_element_type=F32)
        o = acc[0:MLA_V] / acc[MLA_V:MLA_V + 1]
        o_ref[0, :, c * MLA_V:(c + 1) * MLA_V] = o.T.astype(o_ref.dtype)


def _mla_attention(qt, k, vt, n_ctx, with_ctx):
    bsz, nh, dqk, t = qt.shape
    tq = TOKEN_BLOCK
    kb = vt.shape[-1]
    n_key_blocks = t // kb
    g = MLA_HEADS_PER_STEP
    nq = MLA_QBLOCKS_PER_STEP * tq
    n_lat = t - n_ctx
    assert n_lat % nq == 0 and n_lat % n_ctx == 0 and n_ctx % kb == 0
    y = pl.pallas_call(
        _mla_kernel,
        grid=(bsz, nh // g, n_lat // nq),
        in_specs=[pl.BlockSpec((1, g, dqk, nq), lambda b, h, i: (b, h, 0, i)),
                  pl.BlockSpec((1, g, t, dqk), lambda b, h, i: (b, h, 0, 0)),
                  pl.BlockSpec((1, g, n_key_blocks, MLA_VT_ROWS, kb), lambda b, h, i: (b, h, 0, 0, 0))],
        out_specs=pl.BlockSpec((1, nq, g * MLA_V), lambda b, h, i: (b, i, h)),
        out_shape=jax.ShapeDtypeStruct((bsz, t, nh * MLA_V), BF16),
        scratch_shapes=[pltpu.VMEM((g, 2, kb, tq), F32),
                        pltpu.VMEM((g, 2, kb, tq), BF16),
                        pltpu.VMEM((g, 2, 1, tq), F32),
                        pltpu.VMEM((g, 2, 1, tq), F32),
                        pltpu.VMEM((g, MLA_QBLOCKS_PER_STEP, MLA_VT_ROWS, tq), F32),
                        pltpu.VMEM((g, MLA_QBLOCKS_PER_STEP, 1, tq), F32)],
        compiler_params=_cparams(3),
        name="mla_attention",
    )(qt, k, vt)
    if not with_ctx:
        return y
    ctx_blk = n_lat // n_ctx
    ncb = n_ctx // kb
    return pl.pallas_call(
        _mla_ctx_kernel,
        grid=(bsz, nh // g),
        in_specs=[pl.BlockSpec((1, g, dqk, n_ctx), lambda b, h: (b, h, 0, ctx_blk)),
                  pl.BlockSpec((1, g, n_ctx, dqk), lambda b, h: (b, h, ctx_blk, 0)),
                  pl.BlockSpec((1, g, ncb, MLA_VT_ROWS, kb), lambda b, h: (b, h, ctx_blk, 0, 0)),
                  pl.BlockSpec(memory_space=pl.ANY)],
        out_specs=pl.BlockSpec((1, n_ctx, g * MLA_V), lambda b, h: (b, ctx_blk, h)),
        out_shape=jax.ShapeDtypeStruct(y.shape, y.dtype),
        input_output_aliases={3: 0},
        compiler_params=_cparams(2),
        name="mla_attention_ctx",
    )(qt, k, vt, y)


def _ffn_chunks(hidden):
    step = 512
    return tuple((lo, min(lo + step, hidden)) for lo in range(0, hidden, step))


def _merge_ffn_kernel(x_ref, mod_ref, gates_ref, ys_ref, yw_ref, ym_ref, wps_ref, wpw_ref, wpm_ref,
                      wo_ref, g2_ref, wfi_ref, wfo_ref, o_ref):
    d = x_ref.shape[-1]
    hidden = wfo_ref.shape[0]
    mod = mod_ref[0, 0]
    gt1, sh2, sc2, gt2 = mod[2:3], mod[3:4], mod[4:5], mod[5:6]
    gates = jax.nn.sigmoid(gates_ref[0])
    merged = (gates[:, 0:d] * jnp.dot(ys_ref[0], wps_ref[...], preferred_element_type=F32)
              + gates[:, d:2 * d] * jnp.dot(yw_ref[0], wpw_ref[...], preferred_element_type=F32)
              + gates[:, 2 * d:3 * d] * jnp.dot(ym_ref[0], wpm_ref[...], preferred_element_type=F32))
    x1 = x_ref[0] + gt1 * jnp.dot(merged.astype(BF16), wo_ref[...], preferred_element_type=F32)
    hb = (_rms(x1, g2_ref[...]) * (1.0 + sc2) + sh2).astype(BF16)
    acc = jnp.zeros_like(x1)
    for lo, hi in _ffn_chunks(hidden):
        gate = jnp.dot(hb, wfi_ref[:, lo:hi], preferred_element_type=F32)
        up = jnp.dot(hb, wfi_ref[:, hidden + lo:hidden + hi], preferred_element_type=F32)
        acc = acc + jnp.dot((_silu(gate) * up).astype(BF16), wfo_ref[lo:hi, :], preferred_element_type=F32)
    o_ref[0] = x1 + gt2 * acc


def _merge_ffn(x_all, modsel, gates, y_ssm, y_swa, y_mla, wps, wpw, wpm, wo, norm2_g, wfi, wfo,
               n_ctx_blocks, first_block):
    bsz, t, d = x_all.shape
    tm = TOKEN_BLOCK
    nblk = t // tm - first_block

    def tok(width):
        return pl.BlockSpec((1, tm, width), lambda b, i: (b, i + first_block, 0))

    return pl.pallas_call(
        _merge_ffn_kernel,
        grid=(bsz, nblk),
        in_specs=[tok(d),
                  pl.BlockSpec((1, 1, SUBLANE, d),
                               lambda b, i: (b, jnp.where(i + first_block < n_ctx_blocks, 0, 1), 0, 0)),
                  tok(N_BRANCH * d), tok(y_ssm.shape[-1]), tok(y_swa.shape[-1]),
                  pl.BlockSpec((1, tm, y_mla.shape[-1]),
                               lambda b, i: (b, _latent_first(i + first_block, n_blocks=t // tm,
                                                              n_ctx_blocks=n_ctx_blocks), 0)),
                  _resident(wps.shape), _resident(wpw.shape), _resident(wpm.shape), _resident(wo.shape),
                  _resident((1, d)), _resident(wfi.shape), _resident(wfo.shape)],
        out_specs=pl.BlockSpec((1, tm, d), lambda b, i: (b, i, 0)),
        out_shape=jax.ShapeDtypeStruct((bsz, nblk * tm, d), F32),
        compiler_params=_cparams(2),
        name="merge_ffn",
    )(x_all, modsel, gates, y_ssm, y_swa, y_mla, wps, wpw, wpm, wo, norm2_g.reshape(1, d), wfi, wfo)


def _head_major(w, n_heads, parts):
    k = w.shape[0]
    w = w.reshape(k, n_heads, sum(parts))
    out, lo = [], 0
    for p in parts:
        out.append(w[:, :, lo:lo + p].reshape(k, n_heads * p))
        lo += p
    return jnp.concatenate(out, axis=1)


def kernel(x, c, ctx, c_ctx, w_mod, b_mod, norm1_g, norm2_g, w_in, ssm_conv_w, ssm_conv_b, ssm_dt_bias,
           ssm_a_log, ssm_d, ssm_norm_g, swa_q_norm_g, swa_k_norm_g, swa_sink, mla_q_lat_g, mla_kv_lat_g,
           w_mla_uq, w_mla_ukv, mla_q_norm_g, mla_k_norm_g, w_p_ssm, w_p_swa, w_p_mla, w_out, w_ffn_in,
           w_ffn_out):
    bsz, n_lat, d = x.shape
    n_ctx = ctx.shape[1]
    depth = w_mod.shape[0]
    assert n_ctx % TOKEN_BLOCK == 0 and n_lat % TOKEN_BLOCK == 0 and n_lat % GRID_W == 0
    assert bsz + 1 <= SUBLANE
    n_ctx_blocks = n_ctx // TOKEN_BLOCK

    cvec = jnp.concatenate([c, c_ctx[None], jnp.zeros((SUBLANE - bsz - 1, d), F32)], axis=0)
    mods = _modulation(cvec, w_mod, b_mod).reshape(depth, SUBLANE, 6, d)
    rope_swa = _rope_tables(n_lat, n_ctx, SWA_HEAD_DIM)
    rope_mla = _rope_tables(n_lat, n_ctx, MLA_ROPE)

    x_all = jnp.concatenate([ctx, x], axis=1)
    for i in range(depth):
        last = i == depth - 1
        ctx_mod = jnp.broadcast_to(mods[i, bsz][None], (bsz, 6, d))
        modsel = jnp.pad(jnp.stack([ctx_mod, mods[i, :bsz]], axis=1), ((0, 0), (0, 0), (0, SUBLANE - 6), (0, 0)))

        xbc, small, z, gates, qs, ks, vs, qm, km, vm = _in_projection(
            x_all, modsel, norm1_g[i], w_in[i], w_mla_uq[i], w_mla_ukv[i], swa_q_norm_g[i], swa_k_norm_g[i],
            mla_q_lat_g[i], mla_kv_lat_g[i], mla_q_norm_g[i], mla_k_norm_g[i], rope_swa, rope_mla, n_ctx_blocks)

        y_ssm = _ssd_branch(xbc, small, z, ssm_conv_w[i], ssm_conv_b[i], ssm_dt_bias[i], ssm_a_log[i],
                            ssm_d[i], ssm_norm_g[i], n_ctx)
        y_swa = _swa_attention(qs, ks, vs, swa_sink[i], n_ctx)
        first_tok = n_ctx_blocks if last else 0
        y_mla = _mla_attention(qm, km, vm, n_ctx, with_ctx=not last)

        x_all = _merge_ffn(x_all, modsel, gates, y_ssm, y_swa, y_mla,
                           w_p_ssm[i].astype(BF16), w_p_swa[i].astype(BF16), w_p_mla[i].astype(BF16),
                           w_out[i].astype(BF16), norm2_g[i], w_ffn_in[i].astype(BF16),
                           w_ffn_out[i].astype(BF16), n_ctx_blocks, first_tok)
    return x_all
```

```python
import functools
import math

import jax
import jax.numpy as jnp
from jax import lax
from jax.experimental import pallas as pl
from jax.experimental.pallas import tpu as pltpu

F32 = jnp.float32
BF16 = jnp.bfloat16
HIGHEST = lax.Precision.HIGHEST

EPS = 1e-6
ROPE_BASE = 10000.0
GRID_W = 64

SSM_HEADS = 16
SSM_HEAD_DIM = 64
SSM_INNER = SSM_HEADS * SSM_HEAD_DIM
SSM_GROUPS = 2
SSM_STATE = 128
SSM_CONV = 5
SSM_CHUNK = 128
SSM_BC = SSM_GROUPS * SSM_STATE
SSM_CONV_DIM = SSM_INNER + 2 * SSM_BC
SSM_HPG = SSM_HEADS // SSM_GROUPS

SWA_Q_HEADS = 8
SWA_KV_HEADS = 2
SWA_HEAD_DIM = 128
SWA_WINDOW = 128
SWA_BLOCK = 128
SWA_GRP = SWA_Q_HEADS // SWA_KV_HEADS

MLA_HEADS = 8
MLA_Q_RANK = 384
MLA_KV_RANK = 256
MLA_NOPE = 128
MLA_ROPE = 64
MLA_QK = MLA_NOPE + MLA_ROPE
MLA_V = 128
BF16_SUBLANES = 16
MLA_VT_ROWS = MLA_V + BF16_SUBLANES

N_BRANCH = 3
LANE = 128
SUBLANE = 8
HALO = SUBLANE
TOKEN_BLOCK = 256
VMEM_LIMIT = 56 * 1024 * 1024


def _cparams(n_axes):
    return pltpu.CompilerParams(
        dimension_semantics=("arbitrary",) * n_axes, vmem_limit_bytes=VMEM_LIMIT)


def _resident(shape):
    nd = len(shape)
    return pl.BlockSpec(shape, lambda *_: (0,) * nd, pipeline_mode=pl.Buffered(1))


def _rms(x, g):
    return x * lax.rsqrt(jnp.mean(x * x, axis=-1, keepdims=True) + EPS) * g


def _silu(x):
    return x * jax.nn.sigmoid(x)


def _xdot(a, b):
    return jnp.dot(a, b, precision=HIGHEST, preferred_element_type=F32)


def _mod_kernel(c_ref, w_ref, b_ref, o_ref):
    o_ref[0] = _xdot(_silu(c_ref[...]), w_ref[0]) + b_ref[0]


def _modulation(cvec, w_mod, b_mod):
    depth, d, d6 = w_mod.shape
    rows = cvec.shape[0]
    return pl.pallas_call(
        _mod_kernel,
        grid=(depth, d6 // d),
        in_specs=[pl.BlockSpec((rows, d), lambda i, j: (0, 0)),
                  pl.BlockSpec((1, d, d), lambda i, j: (i, 0, j)),
                  pl.BlockSpec((1, 1, d), lambda i, j: (i, 0, j))],
        out_specs=pl.BlockSpec((1, rows, d), lambda i, j: (i, 0, j)),
        out_shape=jax.ShapeDtypeStruct((depth, rows, d6), F32),
        compiler_params=_cparams(2),
        name="modulation",
    )(cvec, w_mod, b_mod.reshape(depth, 1, d6))


SWA_DQ = SWA_Q_HEADS * SWA_HEAD_DIM
SWA_DKV = SWA_KV_HEADS * SWA_HEAD_DIM
SWA_VT_ROWS = SWA_HEAD_DIM + BF16_SUBLANES
LOG2E = math.log2(math.e)


def _split_in_weights(w_in_l, d):
    o = [0]
    for w in (SSM_CONV_DIM, 2 * SSM_HEADS, SWA_DKV, SWA_DKV, MLA_KV_RANK, MLA_ROPE, SSM_INNER, SWA_DQ,
              MLA_Q_RANK, N_BRANCH * d):
        o.append(o[-1] + w)
    xbc, dt, k, v, ckv, kr, z, q, cq, gates = (w_in_l[:, o[i]:o[i + 1]] for i in range(10))
    pad = jnp.zeros((d, LANE - 2 * SSM_HEADS - MLA_ROPE), w_in_l.dtype)
    w_tok = jnp.concatenate([xbc, dt, kr, pad, k, ckv, z, gates], axis=1).astype(BF16)
    w_feat = jnp.concatenate([q, v, cq], axis=1).T.astype(BF16)
    return w_tok, w_feat


def _latent_first(i, *, n_blocks, n_ctx_blocks):
    return jnp.where(i < n_ctx_blocks, n_blocks - n_ctx_blocks + i, i - n_ctx_blocks)


def _inproj_kernel(x_ref, mod_ref, g_ref, wtok_ref, wfeat_ref, wqt_ref, wk_ref, wvt_ref,
                   sqg_ref, skg_ref, qlg_ref, kvlg_ref, qgn_ref, qgr_ref, kgn_ref, kgr_ref,
                   scos_ref, ssin_ref, scost_ref, ssint_ref, mcos_ref, msin_ref, mcost_ref, msint_ref,
                   xbc_ref, small_ref, z_ref, gates_ref, sq_ref, sk_ref, sv_ref, mq_ref, mk_ref, mv_ref):
    d = x_ref.shape[-1]
    tm = x_ref.shape[1]
    mod = mod_ref[0, 0]
    hb = (_rms(x_ref[0], g_ref[...]) * (1.0 + mod[1:2]) + mod[0:1]).astype(BF16)

    def tok(lo, width):
        return jnp.dot(hb, wtok_ref[:, lo:lo + width], preferred_element_type=F32)

    o_small = SSM_CONV_DIM
    o_k = o_small + LANE
    o_ckv = o_k + SWA_DKV
    o_z = o_ckv + MLA_KV_RANK
    o_gates = o_z + SSM_INNER
    dh = SWA_HEAD_DIM
    nh, dn, dr, dv = MLA_HEADS, MLA_NOPE, MLA_ROPE, MLA_V
    qscale = MLA_QK ** -0.5 * LOG2E
    o_cq = SWA_DQ + SWA_DKV

    nt = (((1,), (1,)), ((), ()))
    feat_cq = lax.dot_general(wfeat_ref[o_cq:o_cq + MLA_Q_RANK, :], hb, nt, preferred_element_type=F32)
    feat = lax.dot_general(wfeat_ref[0:o_cq, :], hb, nt, preferred_element_type=F32)
    small = tok(o_small, LANE)
    small_ref[0] = small
    k_swa = tok(o_k, SWA_DKV)
    ckvn = _rms(tok(o_ckv, MLA_KV_RANK), kvlg_ref[...]).astype(BF16)
    cqn_t = _rms_cols(feat_cq, qlg_ref[...]).astype(BF16)
    qf_t = jnp.dot(wqt_ref[...], cqn_t, preferred_element_type=F32)

    def swa_queries(heads):
        scos_t, ssin_t = scost_ref[...], ssint_ref[...]
        for h in heads:
            xq = _rms_cols(feat[h * dh:(h + 1) * dh], sqg_ref[...])
            xq = xq * scos_t + _rotate_half_rows(xq) * ssin_t
            sq_ref[0, h * dh:(h + 1) * dh, :] = (xq * (dh ** -0.5 * LOG2E)).astype(sq_ref.dtype)

    def swa_keys_values():
        scos, ssin = scos_ref[...], ssin_ref[...]
        for h in range(SWA_KV_HEADS):
            vt = feat[SWA_DQ + h * dh:SWA_DQ + (h + 1) * dh]
            for u in range(tm // SWA_BLOCK):
                sv_ref[0, h, u, 0:dh, :] = vt[:, u * SWA_BLOCK:(u + 1) * SWA_BLOCK].astype(sv_ref.dtype)
                sv_ref[0, h, u, dh:, :] = jnp.ones((SWA_VT_ROWS - dh, SWA_BLOCK), sv_ref.dtype)
            xk = _rms(k_swa[:, h * dh:(h + 1) * dh], skg_ref[...])
            xk = xk * scos + _rotate_half(xk) * ssin
            sk_ref[0, :, h * dh:(h + 1) * dh] = xk.astype(sk_ref.dtype)

    def mla_queries(heads):
        mcos_t, msin_t = mcost_ref[...], msint_ref[...]
        for h in heads:
            qn = _rms_cols(qf_t[h * dn:(h + 1) * dn], qgn_ref[...])
            qr = _rms_cols(qf_t[nh * dn + h * dr:nh * dn + (h + 1) * dr], qgr_ref[...])
            qr = qr * mcos_t + _rotate_half_rows(qr) * msin_t
            mq_ref[0, h, 0:dn, :] = (qn * qscale).astype(mq_ref.dtype)
            mq_ref[0, h, dn:dn + dr, :] = (qr * qscale).astype(mq_ref.dtype)

    def mla_keys():
        kf = jnp.dot(ckvn, wk_ref[...], preferred_element_type=F32)
        kr0 = 2 * SSM_HEADS
        kr = _rms(small[:, kr0:kr0 + dr], kgr_ref[...])
        kr = kr * mcos_ref[...] + _rotate_half(kr) * msin_ref[...]
        for h in range(nh):
            kn = _rms(kf[:, h * dn:(h + 1) * dn], kgn_ref[...])
            mk_ref[0, h] = jnp.concatenate([kn, kr], axis=1).astype(mk_ref.dtype)

    def mla_values():
        vf_t = lax.dot_general(wvt_ref[...], ckvn, (((1,), (1,)), ((), ())), preferred_element_type=F32)
        for h in range(nh):
            mv_ref[0, h, 0, 0:dv, :] = vf_t[h * dv:(h + 1) * dv].astype(mv_ref.dtype)
            mv_ref[0, h, 0, dv:, :] = jnp.ones((MLA_VT_ROWS - dv, tm), mv_ref.dtype)

    half_q = SWA_Q_HEADS // 2
    half_c = SSM_CONV_DIM // 2
    xbc_ref[0, :, 0:half_c] = tok(0, half_c)
    swa_queries(range(0, half_q))
    xbc_ref[0, :, half_c:] = tok(half_c, SSM_CONV_DIM - half_c)
    swa_queries(range(half_q, SWA_Q_HEADS))
    z_ref[0] = tok(o_z, SSM_INNER)
    swa_keys_values()
    gates_ref[0, :, 0:d] = tok(o_gates, d)
    mla_queries(range(0, nh // 2))
    gates_ref[0, :, d:2 * d] = tok(o_gates + d, d)
    mla_queries(range(nh // 2, nh))
    gates_ref[0, :, 2 * d:] = tok(o_gates + 2 * d, d)
    mla_keys()
    mla_values()


def _in_projection(x_all, modsel, norm_g, w_in_l, w_uq, w_ukv, swa_q_g, swa_k_g, q_lat_g, kv_lat_g,
                   mla_q_g, mla_k_g, rope_swa, rope_mla, n_ctx_blocks):
    bsz, t, d = x_all.shape
    tm = TOKEN_BLOCK
    nh = MLA_HEADS
    w_tok, w_feat = _split_in_weights(w_in_l, d)
    wq_t = _head_major(w_uq, nh, (MLA_NOPE, MLA_ROPE)).T.astype(BF16)
    wkv = _head_major(w_ukv, nh, (MLA_NOPE, MLA_V))
    wk = wkv[:, :nh * MLA_NOPE].astype(BF16)
    wv_t = wkv[:, nh * MLA_NOPE:].T.astype(BF16)
    scos, ssin = rope_swa
    mcos, msin = rope_mla

    def cols(g):
        return jnp.broadcast_to(g[:, None], (g.shape[0], tm))

    def row(g):
        return g.reshape(1, -1)

    def tok(width):
        return pl.BlockSpec((1, tm, width), lambda b, i: (b, i, 0))

    def tok_table(width):
        return pl.BlockSpec((tm, width), lambda b, i: (i, 0))

    def feat_table(height):
        return pl.BlockSpec((height, tm), lambda b, i: (0, i))

    consts = [row(norm_g), w_tok, w_feat, wq_t, wk, wv_t,
              cols(swa_q_g), row(swa_k_g), cols(q_lat_g), row(kv_lat_g),
              cols(mla_q_g[:MLA_NOPE]), cols(mla_q_g[MLA_NOPE:]), row(mla_k_g[:MLA_NOPE]), row(mla_k_g[MLA_NOPE:])]
    nkb = tm // SWA_BLOCK
    mla_pos = functools.partial(_latent_first, n_blocks=t // tm, n_ctx_blocks=n_ctx_blocks)
    return pl.pallas_call(
        _inproj_kernel,
        grid=(bsz, t // tm),
        in_specs=[tok(d),
                  pl.BlockSpec((1, 1, SUBLANE, d), lambda b, i: (b, jnp.where(i < n_ctx_blocks, 0, 1), 0, 0))]
                 + [_resident(a.shape) for a in consts]
                 + [tok_table(SWA_HEAD_DIM), tok_table(SWA_HEAD_DIM), feat_table(SWA_HEAD_DIM),
                    feat_table(SWA_HEAD_DIM), tok_table(MLA_ROPE), tok_table(MLA_ROPE), feat_table(MLA_ROPE),
                    feat_table(MLA_ROPE)],
        out_specs=[tok(SSM_CONV_DIM), tok(LANE), tok(SSM_INNER), tok(N_BRANCH * d),
                   pl.BlockSpec((1, SWA_DQ, tm), lambda b, i: (b, 0, i)),
                   tok(SWA_DKV),
                   pl.BlockSpec((1, SWA_KV_HEADS, nkb, SWA_VT_ROWS, SWA_BLOCK), lambda b, i: (b, 0, i, 0, 0)),
                   pl.BlockSpec((1, nh, MLA_QK, tm), lambda b, i: (b, 0, 0, mla_pos(i))),
                   pl.BlockSpec((1, nh, tm, MLA_QK), lambda b, i: (b, 0, mla_pos(i), 0)),
                   pl.BlockSpec((1, nh, 1, MLA_VT_ROWS, tm), lambda b, i: (b, 0, mla_pos(i), 0, 0))],
        out_shape=[jax.ShapeDtypeStruct((bsz, t, SSM_CONV_DIM), F32),
                   jax.ShapeDtypeStruct((bsz, t, LANE), F32),
                   jax.ShapeDtypeStruct((bsz, t, SSM_INNER), F32),
                   jax.ShapeDtypeStruct((bsz, t, N_BRANCH * d), F32),
                   jax.ShapeDtypeStruct((bsz, SWA_DQ, t), BF16),
                   jax.ShapeDtypeStruct((bsz, t, SWA_DKV), BF16),
                   jax.ShapeDtypeStruct((bsz, SWA_KV_HEADS, t // SWA_BLOCK, SWA_VT_ROWS, SWA_BLOCK), BF16),
                   jax.ShapeDtypeStruct((bsz, nh, MLA_QK, t), BF16),
                   jax.ShapeDtypeStruct((bsz, nh, t, MLA_QK), BF16),
                   jax.ShapeDtypeStruct((bsz, nh, t // tm, MLA_VT_ROWS, tm), BF16)],
        compiler_params=_cparams(2),
        name="in_projection",
    )(x_all, modsel, *consts, scos, ssin, scos.T, ssin.T, mcos, msin, mcos.T, msin.T)


def _ssd_scalars(small, dtb_ref, alog_ref):
    q = small.shape[0]
    lane = lax.broadcasted_iota(jnp.int32, (1, LANE), 1)
    raw = small + dtb_ref[...]
    dts = jnp.maximum(raw, 0.0) + jnp.log1p(jnp.exp(-jnp.abs(raw)))
    a = jnp.where(lane < 2 * SSM_HEADS, -jnp.exp(alog_ref[...]), 0.0)
    dta = dts * a
    ri = lax.broadcasted_iota(jnp.int32, (q, q), 0)
    ci = lax.broadcasted_iota(jnp.int32, (q, q), 1)
    tri = (ci <= ri).astype(BF16)
    parts = jnp.dot(tri, _split3(dta), preferred_element_type=F32)
    acs = parts[:, :LANE] + parts[:, LANE:2 * LANE] + parts[:, 2 * LANE:]
    ecs = acs - dta
    return dts, acs, ecs, dts.T, acs.T, ecs.T


def _split3(x):
    hi = x.astype(BF16)
    r1 = x - hi.astype(F32)
    mid = r1.astype(BF16)
    lo = (r1 - mid.astype(F32)).astype(BF16)
    return jnp.concatenate([hi, mid, lo], axis=1)


def _expand_matrix(first_row):
    r = lax.broadcasted_iota(jnp.int32, (3 * LANE, SSM_INNER), 0)
    c = lax.broadcasted_iota(jnp.int32, (3 * LANE, SSM_INNER), 1)
    return (c // SSM_HEAD_DIM + first_row == (r & (LANE - 1))).astype(BF16)


def _ssd_direction(backward, xs, bm, cm, scal, state_ref):
    dts, acs, ecs, dts_t, acs_t, ecs_t = scal
    q = xs.shape[0]
    base = SSM_HEADS if backward else 0
    expand = _expand_matrix(base)
    tot = acs[q - 1:q, :]
    if backward:
        dec_in = jnp.exp(tot - ecs)
        w_out = jnp.exp(ecs) * dts
        pos, pos_t = ecs, ecs_t
    else:
        dec_in = jnp.exp(acs)
        w_out = jnp.exp(tot - acs) * dts
        pos, pos_t = acs, acs_t
    stacked = jnp.concatenate([dec_in, w_out, jnp.broadcast_to(jnp.exp(tot), (SUBLANE, LANE))], axis=0)
    expanded = jnp.dot(_split3(stacked), expand, preferred_element_type=F32)
    dec_e, w_e, tot_e = expanded[0:q], expanded[q:2 * q], expanded[2 * q:2 * q + 1]
    xw = (xs * w_e).astype(BF16)
    xb = xs.astype(BF16)
    ri = lax.broadcasted_iota(jnp.int32, (q, q), 0)
    ci = lax.broadcasted_iota(jnp.int32, (q, q), 1)
    keep = (ci >= ri) if backward else (ci <= ri)
    gw = SSM_HPG * SSM_HEAD_DIM
    ys = []
    for g in range(SSM_GROUPS):
        b_g = bm[:, g * SSM_STATE:(g + 1) * SSM_STATE]
        c_g = cm[:, g * SSM_STATE:(g + 1) * SSM_STATE].astype(BF16)
        b_t = b_g.T.astype(BF16)
        cb = jnp.dot(c_g, b_t, preferred_element_type=F32)
        st = state_ref[g]
        y_off = jnp.dot(c_g, st.astype(BF16), preferred_element_type=F32) * dec_e[:, g * gw:(g + 1) * gw]
        state_ref[g] = st * tot_e[:, g * gw:(g + 1) * gw] + jnp.dot(
            b_t, xw[:, g * gw:(g + 1) * gw], preferred_element_type=F32)
        heads = []
        for hh in range(SSM_HPG):
            h = g * SSM_HPG + hh
            col = pos[:, base + h:base + h + 1]
            row = pos_t[base + h:base + h + 1, :]
            diff = (row - col) if backward else (col - row)
            seg = jnp.exp(jnp.where(keep, diff, -jnp.inf))
            m = (cb * seg * dts_t[base + h:base + h + 1, :]).astype(BF16)
            heads.append(jnp.dot(m, xb[:, h * SSM_HEAD_DIM:(h + 1) * SSM_HEAD_DIM],
                                 preferred_element_type=F32))
        ys.append(jnp.concatenate(heads, axis=1) + y_off)
    return jnp.concatenate(ys, axis=1)


SSD_FWD_CHUNKS_PER_STEP = 6
SSD_BWD_CHUNKS_PER_STEP = 2


def _ssd_fwd_kernel(xc_ref, xp_ref, xn_ref, small_ref, cw_ref, cb_ref, dtb_ref, alog_ref, dskip_ref,
                    y_ref, u_ref, state_ref, *, n_ctx_chunks, n_chunks):
    step = pl.program_id(1)

    @pl.when(step == 0)
    def _():
        state_ref[...] = jnp.zeros_like(state_ref)

    q = SSM_CHUNK
    half = SSM_CONV // 2
    x_all = xc_ref[0]
    n_local = x_all.shape[0] // q
    for j in range(n_local):
        c = step * n_local + j
        rows = slice(j * q, (j + 1) * q)
        prev_ok = jnp.logical_and(c != 0, c != n_ctx_chunks)
        next_ok = jnp.logical_and(c != n_ctx_chunks - 1, c != n_chunks - 1)
        xp = xp_ref[0] if j == 0 else x_all[j * q - HALO:j * q]
        xn = xn_ref[0] if j == n_local - 1 else x_all[(j + 1) * q:(j + 1) * q + HALO]
        xc = x_all[rows]
        ext = jnp.concatenate([jnp.where(prev_ok, xp, 0.0), xc, jnp.where(next_ok, xn, 0.0)], axis=0)
        acc = jnp.zeros_like(xc) + cb_ref[...]
        for k in range(SSM_CONV):
            lo = HALO - half + k
            acc = acc + ext[lo:lo + q, :] * cw_ref[k:k + 1, :]
        u = _silu(acc)
        u_ref[0, rows, :] = u.astype(u_ref.dtype)
        xs = u[:, :SSM_INNER]
        bm = u[:, SSM_INNER:SSM_INNER + SSM_BC]
        cm = u[:, SSM_INNER + SSM_BC:]
        scal = _ssd_scalars(small_ref[0, rows, :], dtb_ref, alog_ref)
        y = _ssd_direction(False, xs, bm, cm, scal, state_ref)
        y_ref[0, rows, :] = y + dskip_ref[...] * xs


def _ssd_bwd_kernel(u_ref, small_ref, z_ref, yf_ref, dtb_ref, alog_ref, ng_ref, o_ref, state_ref):
    @pl.when(pl.program_id(1) == 0)
    def _():
        state_ref[...] = jnp.zeros_like(state_ref)

    q = SSM_CHUNK
    for j in reversed(range(u_ref.shape[1] // q)):
        rows = slice(j * q, (j + 1) * q)
        u = u_ref[0, rows, :].astype(F32)
        xs = u[:, :SSM_INNER]
        bm = u[:, SSM_INNER:SSM_INNER + SSM_BC]
        cm = u[:, SSM_INNER + SSM_BC:]
        scal = _ssd_scalars(small_ref[0, rows, :], dtb_ref, alog_ref)
        y = yf_ref[0, rows, :] + _ssd_direction(True, xs, bm, cm, scal, state_ref)
        o_ref[0, rows, :] = _rms(y * _silu(z_ref[0, rows, :]), ng_ref[...]).astype(o_ref.dtype)


def _ssd_branch(xbc, small, z, conv_w, conv_b, dt_bias, a_log, d_skip, norm_g, n_ctx):
    bsz, t, _ = xbc.shape
    q = SSM_CHUNK
    rows_f = SSD_FWD_CHUNKS_PER_STEP * q
    rows = SSD_BWD_CHUNKS_PER_STEP * q
    assert t % rows_f == 0 and t % rows == 0 and n_ctx % rows == 0
    n_steps = t // rows
    n_ctx_steps = n_ctx // rows
    hb = rows_f // HALO
    n_halo = t // HALO
    pad32 = LANE - 2 * SSM_HEADS
    dtb = jnp.pad(dt_bias.reshape(1, -1), ((0, 0), (0, pad32)))
    alog = jnp.pad(a_log.reshape(1, -1), ((0, 0), (0, pad32)))
    dskip = jnp.repeat(d_skip, SSM_HEAD_DIM).reshape(1, SSM_INNER)
    state = pltpu.VMEM((SSM_GROUPS, SSM_STATE, SSM_HPG * SSM_HEAD_DIM), F32)

    def block(width):
        return pl.BlockSpec((1, rows_f, width), lambda b, s: (b, s, 0))

    y_f, u = pl.pallas_call(
        functools.partial(_ssd_fwd_kernel, n_ctx_chunks=n_ctx // q, n_chunks=t // q),
        grid=(bsz, t // rows_f),
        in_specs=[block(SSM_CONV_DIM),
                  pl.BlockSpec((1, HALO, SSM_CONV_DIM), lambda b, s: (b, jnp.maximum(s * hb - 1, 0), 0)),
                  pl.BlockSpec((1, HALO, SSM_CONV_DIM),
                               lambda b, s: (b, jnp.minimum((s + 1) * hb, n_halo - 1), 0)),
                  block(LANE),
                  _resident((SSM_CONV, SSM_CONV_DIM)), _resident((1, SSM_CONV_DIM)),
                  _resident((1, LANE)), _resident((1, LANE)), _resident((1, SSM_INNER))],
        out_specs=[block(SSM_INNER), block(SSM_CONV_DIM)],
        out_shape=[jax.ShapeDtypeStruct((bsz, t, SSM_INNER), F32),
                   jax.ShapeDtypeStruct((bsz, t, SSM_CONV_DIM), BF16)],
        scratch_shapes=[state],
        compiler_params=_cparams(2),
        name="ssd_forward",
    )(xbc, xbc, xbc, small, conv_w, conv_b.reshape(1, -1), dtb, alog, dskip)

    def rblock(width):
        return pl.BlockSpec(
            (1, rows, width),
            lambda b, s: (b, jnp.where(s < n_ctx_steps, n_ctx_steps - 1 - s, n_steps + n_ctx_steps - 1 - s), 0))

    return pl.pallas_call(
        _ssd_bwd_kernel,
        grid=(bsz, n_steps),
        in_specs=[rblock(SSM_CONV_DIM), rblock(LANE), rblock(SSM_INNER), rblock(SSM_INNER),
                  _resident((1, LANE)), _resident((1, LANE)), _resident((1, SSM_INNER))],
        out_specs=rblock(SSM_INNER),
        out_shape=jax.ShapeDtypeStruct((bsz, t, SSM_INNER), BF16),
        scratch_shapes=[state],
        compiler_params=_cparams(2),
        name="ssd_backward",
    )(u, small, z, y_f, dtb, alog, norm_g.reshape(1, -1))


def _rope_tables(n_lat, n_ctx, rot_dim):
    n_freq = rot_dim // 4
    inv = jnp.power(ROPE_BASE, -jnp.arange(n_freq, dtype=F32) / n_freq)
    t = jnp.arange(n_lat)
    r = (t // GRID_W).astype(F32)[:, None] * inv
    col = (t % GRID_W).astype(F32)[:, None] * inv
    cos2 = jnp.concatenate([jnp.cos(r), jnp.cos(r), jnp.cos(col), jnp.cos(col)], axis=1)
    sin2 = jnp.concatenate([-jnp.sin(r), jnp.sin(r), -jnp.sin(col), jnp.sin(col)], axis=1)
    cos2 = jnp.concatenate([jnp.ones((n_ctx, rot_dim), F32), cos2], axis=0)
    sin2 = jnp.concatenate([jnp.zeros((n_ctx, rot_dim), F32), sin2], axis=0)
    return cos2, sin2


def _rotate_half(x):
    f = x.shape[-1] // 4
    return jnp.concatenate([x[:, f:2 * f], x[:, :f], x[:, 3 * f:], x[:, 2 * f:3 * f]], axis=1)


SWA_BLOCKS_PER_STEP = 6


def _swa_kernel(sink_ref, qt_ref, k_ref, vt_ref, o_ref, *, n_ctx, t):
    for j in range(SWA_BLOCKS_PER_STEP):
        _swa_block(sink_ref, qt_ref, k_ref, vt_ref, o_ref, j, n_ctx=n_ctx, t=t)


def _swa_block(sink_ref, qt_ref, k_ref, vt_ref, o_ref, j, *, n_ctx, t):
    hk = pl.program_id(1)
    blk = pl.program_id(2) * SWA_BLOCKS_PER_STEP + j
    bq, dh = SWA_BLOCK, SWA_HEAD_DIM
    win = 3 * bq
    ncb = n_ctx // bq
    nq = SWA_GRP * bq
    qt = qt_ref[0, :, j * bq:(j + 1) * bq]
    q4t =jnp.concatenate([qt[g * dh:(g + 1) * dh, :] for g in range(SWA_GRP)], axis=1)
    start = pl.multiple_of(jnp.clip((blk - 1) * bq, 0, t - win), bq)
    b0 = start // bq
    s_w = jnp.dot(k_ref[0, pl.ds(start, win), :], q4t, preferred_element_type=F32)
    s_c = jnp.dot(k_ref[0, 0:n_ctx, :], q4t, preferred_element_type=F32)
    rows = lax.broadcasted_iota(jnp.int32, (win, nq), 0)
    cols = lax.broadcasted_iota(jnp.int32, (win, nq), 1)
    qpos = (blk - ncb) * bq + (cols & (bq - 1))
    kpos = start - n_ctx + rows
    ok = (jnp.abs(kpos - qpos) <= SWA_WINDOW) & (kpos >= 0) & (blk >= ncb)
    s_w = jnp.where(ok, s_w, -jnp.inf)
    c1 = lax.broadcasted_iota(jnp.int32, (1, nq), 1)
    sink = jnp.zeros((1, nq), F32)
    for g in range(SWA_GRP):
        sink = jnp.where(c1 // bq == g, sink_ref[hk * SWA_GRP + g], sink)
    sink = sink * LOG2E
    m = jnp.maximum(jnp.maximum(jnp.max(s_w, axis=0, keepdims=True),
                                jnp.max(s_c, axis=0, keepdims=True)), sink)
    p_w = jnp.exp2(s_w - m).astype(BF16)
    p_c = jnp.exp2(s_c - m).astype(BF16)

    def weighted_values(first, nblk, p):
        acc = None
        u = 0
        while u < nblk:
            n = 2 if u + 1 < nblk else 1
            vt = vt_ref[0, 0, first + u]
            if n == 2:
                vt = jnp.concatenate([vt, vt_ref[0, 0, first + u + 1]], axis=1)
            term = jnp.dot(vt, p[u * bq:(u + n) * bq], preferred_element_type=F32)
            acc = term if acc is None else acc + term
            u += n
        return acc

    acc = weighted_values(b0, win // bq, p_w) + weighted_values(0, ncb, p_c)
    o = acc[0:dh] / (acc[dh:dh + 1] + jnp.exp2(sink - m))
    for g in range(SWA_GRP):
        o_ref[0, j * bq:(j + 1) * bq, g * dh:(g + 1) * dh] = o[:, g * bq:(g + 1) * bq].T.astype(o_ref.dtype)


def _swa_attention(qt, k, vt, sink, n_ctx):
    bsz, dq, t = qt.shape
    nb = SWA_BLOCKS_PER_STEP
    bq = SWA_BLOCK
    gw = SWA_GRP * SWA_HEAD_DIM
    assert (t // bq) % nb == 0
    return pl.pallas_call(
        functools.partial(_swa_kernel, n_ctx=n_ctx, t=t),
        grid=(bsz, SWA_KV_HEADS, t // (nb * bq)),
        in_specs=[pl.BlockSpec(memory_space=pltpu.SMEM),
                  pl.BlockSpec((1, gw, nb * bq), lambda b, h, n: (b, h, n)),
                  pl.BlockSpec((1, t, SWA_HEAD_DIM), lambda b, h, n: (b, 0, h)),
                  pl.BlockSpec((1, 1, t // bq, SWA_VT_ROWS, bq), lambda b, h, n: (b, h, 0, 0, 0))],
        out_specs=pl.BlockSpec((1, nb * bq, gw), lambda b, h, n: (b, n, h)),
        out_shape=jax.ShapeDtypeStruct((bsz, t, dq), BF16),
        compiler_params=_cparams(3),
        name="swa_attention",
    )(sink, qt, k, vt)


def _rms_cols(x, g):
    return x * lax.rsqrt(jnp.mean(x * x, axis=0, keepdims=True) + EPS) * g


def _rotate_half_rows(x):
    f = x.shape[0] // 4
    return jnp.concatenate([x[f:2 * f], x[:f], x[3 * f:], x[2 * f:3 * f]], axis=0)


MLA_HEADS_PER_STEP = 2
MLA_QBLOCKS_PER_STEP = 2


def _mla_kernel(qt_ref, k_ref, vt_ref, o_ref, s_scr, p_scr, a_scr, mx_scr, acc_scr, m_scr):
    heads = qt_ref.shape[1]
    n_chunks, kb = vt_ref.shape[2], vt_ref.shape[4]
    tq = TOKEN_BLOCK
    n_qb = qt_ref.shape[3] // tq
    total = n_qb * n_chunks

    def stage_scores(c, item, slot):
        qb, j = divmod(item, n_chunks)
        s = jnp.dot(k_ref[0, c, j * kb:(j + 1) * kb, :], qt_ref[0, c, :, qb * tq:(qb + 1) * tq],
                    preferred_element_type=F32)
        s_scr[c, slot] = s
        mx_scr[c, slot] = jnp.max(s, axis=0, keepdims=True)

    def stage_softmax(c, item, slot):
        qb = item // n_chunks
        m_prev = m_scr[c, qb]
        m_new = jnp.maximum(m_prev, mx_scr[c, slot])
        m_scr[c, qb] = m_new
        a_scr[c, slot] = jnp.exp2(m_prev - m_new)
        p_scr[c, slot] = jnp.exp2(s_scr[c, slot] - m_new).astype(BF16)

    def stage_values(c, item, slot):
        qb, j = divmod(item, n_chunks)
        pv = jnp.dot(vt_ref[0, c, j], p_scr[c, slot], preferred_element_type=F32)
        acc_scr[c, qb] = a_scr[c, slot] * acc_scr[c, qb] + pv

    m_scr[...] = jnp.full(m_scr.shape, -jnp.inf, F32)
    acc_scr[...] = jnp.zeros(acc_scr.shape, F32)
    for t in range(total + 2):
        parity = t % 2
        for c in range(heads):
            if t < total:
                stage_scores(c, t, parity)
            if 1 <= t <= total:
                stage_softmax(c, t - 1, 1 - parity)
            if t >= 2:
                stage_values(c, t - 2, parity)
    for c in range(heads):
        for qb in range(n_qb):
            acc = acc_scr[c, qb]
            o = acc[0:MLA_V] / acc[MLA_V:MLA_V + 1]
            o_ref[0, qb * tq:(qb + 1) * tq, c * MLA_V:(c + 1) * MLA_V] = o.T.astype(o_ref.dtype)


def _mla_ctx_kernel(qt_ref, k_ref, vt_ref, y_ref, o_ref):
    del y_ref
    kb = vt_ref.shape[4]
    for c in range(qt_ref.shape[1]):
        s = jnp.dot(k_ref[0, c], qt_ref[0, c], preferred_element_type=F32)
        p = jnp.exp2(s - jnp.max(s, axis=0, keepdims=True)).astype(BF16)
        acc = jnp.dot(vt_ref[0, c, 0], p[0:kb], preferred_element_type=F32)
        for u in range(1, vt_ref.shape[2]):
            acc = acc + jnp.dot(vt_ref[0, c, u], p[u * kb:(u + 1) * kb], preferred_element_type=F32)
        o = acc[0:MLA_V] / acc[MLA_V:MLA_V + 1]
        o_ref[0, :, c * MLA_V:(c + 1) * MLA_V] = o.T.astype(o_ref.dtype)


def _mla_attention(qt, k, vt, n_ctx, with_ctx):
    bsz, nh, dqk, t = qt.shape
    tq = TOKEN_BLOCK
    kb = vt.shape[-1]
    n_key_blocks = t // kb
    g = MLA_HEADS_PER_STEP
    nq = MLA_QBLOCKS_PER_STEP * tq
    n_lat = t - n_ctx
    assert n_lat % nq == 0 and n_lat % n_ctx == 0 and n_ctx % kb == 0
    y = pl.pallas_call(
        _mla_kernel,
        grid=(bsz, nh // g, n_lat // nq),
        in_specs=[pl.BlockSpec((1, g, dqk, nq), lambda b, h, i: (b, h, 0, i)),
                  pl.BlockSpec((1, g, t, dqk), lambda b, h, i: (b, h, 0, 0)),
                  pl.BlockSpec((1, g, n_key_blocks, MLA_VT_ROWS, kb), lambda b, h, i: (b, h, 0, 0, 0))],
        out_specs=pl.BlockSpec((1, nq, g * MLA_V), lambda b, h, i: (b, i, h)),
        out_shape=jax.ShapeDtypeStruct((bsz, t, nh * MLA_V), BF16),
        scratch_shapes=[pltpu.VMEM((g, 2, kb, tq), F32),
                        pltpu.VMEM((g, 2, kb, tq), BF16),
                        pltpu.VMEM((g, 2, 1, tq), F32),
                        pltpu.VMEM((g, 2, 1, tq), F32),
                        pltpu.VMEM((g, MLA_QBLOCKS_PER_STEP, MLA_VT_ROWS, tq), F32),
                        pltpu.VMEM((g, MLA_QBLOCKS_PER_STEP, 1, tq), F32)],
        compiler_params=_cparams(3),
        name="mla_attention",
    )(qt, k, vt)
    if not with_ctx:
        return y
    ctx_blk = n_lat // n_ctx
    ncb = n_ctx // kb
    return pl.pallas_call(
        _mla_ctx_kernel,
        grid=(bsz, nh // g),
        in_specs=[pl.BlockSpec((1, g, dqk, n_ctx), lambda b, h: (b, h, 0, ctx_blk)),
                  pl.BlockSpec((1, g, n_ctx, dqk), lambda b, h: (b, h, ctx_blk, 0)),
                  pl.BlockSpec((1, g, ncb, MLA_VT_ROWS, kb), lambda b, h: (b, h, ctx_blk, 0, 0)),
                  pl.BlockSpec(memory_space=pl.ANY)],
        out_specs=pl.BlockSpec((1, n_ctx, g * MLA_V), lambda b, h: (b, ctx_blk, h)),
        out_shape=jax.ShapeDtypeStruct(y.shape, y.dtype),
        input_output_aliases={3: 0},
        compiler_params=_cparams(2),
        name="mla_attention_ctx",
    )(qt, k, vt, y)


def _ffn_chunks(hidden):
    step = 512
    return tuple((lo, min(lo + step, hidden)) for lo in range(0, hidden, step))


def _merge_ffn_kernel(x_ref, mod_ref, gates_ref, ys_ref, yw_ref, ym_ref, wps_ref, wpw_ref, wpm_ref,
                      wo_ref, g2_ref, wfi_ref, wfo_ref, o_ref):
    d = x_ref.shape[-1]
    hidden = wfo_ref.shape[0]
    mod = mod_ref[0, 0]
    gt1, sh2, sc2, gt2 = mod[2:3], mod[3:4], mod[4:5], mod[5:6]
    gates = jax.nn.sigmoid(gates_ref[0])
    merged = (gates[:, 0:d] * jnp.dot(ys_ref[0], wps_ref[...], preferred_element_type=F32)
              + gates[:, d:2 * d] * jnp.dot(yw_ref[0], wpw_ref[...], preferred_element_type=F32)
              + gates[:, 2 * d:3 * d] * jnp.dot(ym_ref[0], wpm_ref[...], preferred_element_type=F32))
    x1 = x_ref[0] + gt1 * jnp.dot(merged.astype(BF16), wo_ref[...], preferred_element_type=F32)
    hb = (_rms(x1, g2_ref[...]) * (1.0 + sc2) + sh2).astype(BF16)
    acc = jnp.zeros_like(x1)
    for lo, hi in _ffn_chunks(hidden):
        gate = jnp.dot(hb, wfi_ref[:, lo:hi], preferred_element_type=F32)
        up = jnp.dot(hb, wfi_ref[:, hidden + lo:hidden + hi], preferred_element_type=F32)
        acc = acc + jnp.dot((_silu(gate) * up).astype(BF16), wfo_ref[lo:hi, :], preferred_element_type=F32)
    o_ref[0] = x1 + gt2 * acc


def _merge_ffn(x_all, modsel, gates, y_ssm, y_swa, y_mla, wps, wpw, wpm, wo, norm2_g, wfi, wfo,
               n_ctx_blocks, first_block):
    bsz, t, d = x_all.shape
    tm = TOKEN_BLOCK
    nblk = t // tm - first_block

    def tok(width):
        return pl.BlockSpec((1, tm, width), lambda b, i: (b, i + first_block, 0))

    return pl.pallas_call(
        _merge_ffn_kernel,
        grid=(bsz, nblk),
        in_specs=[tok(d),
                  pl.BlockSpec((1, 1, SUBLANE, d),
                               lambda b, i: (b, jnp.where(i + first_block < n_ctx_blocks, 0, 1), 0, 0)),
                  tok(N_BRANCH * d), tok(y_ssm.shape[-1]), tok(y_swa.shape[-1]),
                  pl.BlockSpec((1, tm, y_mla.shape[-1]),
                               lambda b, i: (b, _latent_first(i + first_block, n_blocks=t // tm,
                                                              n_ctx_blocks=n_ctx_blocks), 0)),
                  _resident(wps.shape), _resident(wpw.shape), _resident(wpm.shape), _resident(wo.shape),
                  _resident((1, d)), _resident(wfi.shape), _resident(wfo.shape)],
        out_specs=pl.BlockSpec((1, tm, d), lambda b, i: (b, i, 0)),
        out_shape=jax.ShapeDtypeStruct((bsz, nblk * tm, d), F32),
        compiler_params=_cparams(2),
        name="merge_ffn",
    )(x_all, modsel, gates, y_ssm, y_swa, y_mla, wps, wpw, wpm, wo, norm2_g.reshape(1, d), wfi, wfo)


def _head_major(w, n_heads, parts):
    k = w.shape[0]
    w = w.reshape(k, n_heads, sum(parts))
    out, lo = [], 0
    for p in parts:
        out.append(w[:, :, lo:lo + p].reshape(k, n_heads * p))
        lo += p
    return jnp.concatenate(out, axis=1)


def kernel(x, c, ctx, c_ctx, w_mod, b_mod, norm1_g, norm2_g, w_in, ssm_conv_w, ssm_conv_b, ssm_dt_bias,
           ssm_a_log, ssm_d, ssm_norm_g, swa_q_norm_g, swa_k_norm_g, swa_sink, mla_q_lat_g, mla_kv_lat_g,
           w_mla_uq, w_mla_ukv, mla_q_norm_g, mla_k_norm_g, w_p_ssm, w_p_swa, w_p_mla, w_out, w_ffn_in,
           w_ffn_out):
    bsz, n_lat, d = x.shape
    n_ctx = ctx.shape[1]
    depth = w_mod.shape[0]
    assert n_ctx % TOKEN_BLOCK == 0 and n_lat % TOKEN_BLOCK == 0 and n_lat % GRID_W == 0
    assert bsz + 1 <= SUBLANE
    n_ctx_blocks = n_ctx // TOKEN_BLOCK

    cvec = jnp.concatenate([c, c_ctx[None], jnp.zeros((SUBLANE - bsz - 1, d), F32)], axis=0)
    mods = _modulation(cvec, w_mod, b_mod).reshape(depth, SUBLANE, 6, d)
    rope_swa = _rope_tables(n_lat, n_ctx, SWA_HEAD_DIM)
    rope_mla = _rope_tables(n_lat, n_ctx, MLA_ROPE)

    x_all = jnp.concatenate([ctx, x], axis=1)
    for i in range(depth):
        last = i == depth - 1
        ctx_mod = jnp.broadcast_to(mods[i, bsz][None], (bsz, 6, d))
        modsel = jnp.pad(jnp.stack([ctx_mod, mods[i, :bsz]], axis=1), ((0, 0), (0, 0), (0, SUBLANE - 6), (0, 0)))

        xbc, small, z, gates, qs, ks, vs, qm, km, vm = _in_projection(
            x_all, modsel, norm1_g[i], w_in[i], w_mla_uq[i], w_mla_ukv[i], swa_q_norm_g[i], swa_k_norm_g[i],
            mla_q_lat_g[i], mla_kv_lat_g[i], mla_q_norm_g[i], mla_k_norm_g[i], rope_swa, rope_mla, n_ctx_blocks)

        y_ssm = _ssd_branch(xbc, small, z, ssm_conv_w[i], ssm_conv_b[i], ssm_dt_bias[i], ssm_a_log[i],
                            ssm_d[i], ssm_norm_g[i], n_ctx)
        y_swa = _swa_attention(qs, ks, vs, swa_sink[i], n_ctx)
        first_tok = n_ctx_blocks if last else 0
        y_mla = _mla_attention(qm, km, vm, n_ctx, with_ctx=not last)

        x_all = _merge_ffn(x_all, modsel, gates, y_ssm, y_swa, y_mla,
                           w_p_ssm[i].astype(BF16), w_p_swa[i].astype(BF16), w_p_mla[i].astype(BF16),
                           w_out[i].astype(BF16), norm2_g[i], w_ffn_in[i].astype(BF16),
                           w_ffn_out[i].astype(BF16), n_ctx_blocks, first_tok)
    return x_all
```
